```python
import jax, jax.numpy as jnp
from jax import lax
import numpy as np

D_MODEL = 1024
BATCH = 8
SEQ = 2048
DEPTH = 2
DEC_BATCH = 128
DEC_SEQ = 1
PAST_LEN = 2048
PAGE_SIZE = 128

D_MIX = D_MODEL
HEAD_DIM = 64
GROUP_W = D_MIX // 4
C_A = GROUP_W
CONV_A_W = 31
H_B = GROUP_W // HEAD_DIM
DV_B = HEAD_DIM
DK_B = HEAD_DIM // 2
GLA_RANK = 16
GLA_TAU = 16.0
GLA_CHUNK = 64
C_C = GROUP_W
CONV_C_W = 3
H_D = GROUP_W // HEAD_DIM
DH_D = HEAD_DIM
Q_BLOCK = 128
FORGET_BIAS_INIT = 3.0
D_FF = ((8 * D_MODEL // 3 + 127) // 128) * 128
EPS = 1e-6
NEG_INF = -1e30

OFF_A = 0
N_A = 2 * C_A
OFF_B = OFF_A + N_A
N_B = 2 * H_B * DK_B + 2 * H_B * DV_B + GLA_RANK
OFF_C = OFF_B + N_B
N_C = 3 * C_C
OFF_D = OFF_C + N_C
N_D = 3 * H_D * DH_D + H_D
N_IN = OFF_D + N_D

kernel_name = 'hymba_style_conformer_gla_shortconv_fox_decoder_step'

F32 = jnp.float32


def rms_norm(x, g):
    xf = x.astype(F32)
    y = xf * lax.rsqrt(jnp.mean(xf * xf, axis=-1, keepdims=True) + EPS)
    return (y * g.astype(F32)).astype(x.dtype)


def layer_norm(x, g, b):
    xf = x.astype(F32)
    mu = jnp.mean(xf, axis=-1, keepdims=True)
    var = jnp.mean(jnp.square(xf - mu), axis=-1, keepdims=True)
    y = (xf - mu) * lax.rsqrt(var + EPS) * g.astype(F32) + b.astype(F32)
    return y.astype(x.dtype)


def swiglu(x, w_gate, w_up, w_down):
    return (jax.nn.silu(x @ w_gate) * (x @ w_up)) @ w_down


def causal_depthwise(xpad, w):
    c = xpad.shape[-1]
    return lax.conv_general_dilated(xpad, w.astype(xpad.dtype)[:, None, :], window_strides=(1,),
                                    padding='VALID', dimension_numbers=('NWC', 'WIO', 'NWC'),
                                    feature_group_count=c)


def conformer_conv(pa, buf, conv_w, conv_b, ln_g, ln_b):
    val, gate = jnp.split(pa, 2, axis=-1)
    a = val * jax.nn.sigmoid(gate)
    xpad = jnp.concatenate([buf.astype(a.dtype), a], axis=1)
    y = causal_depthwise(xpad, conv_w) + conv_b.astype(a.dtype)
    y = jax.nn.silu(layer_norm(y, ln_g, ln_b))
    return y, xpad[:, -(CONV_A_W - 1):]


def gla_chunked(q, k, v, log_a, s0, chunk):
    bsz, length, nh, dk = q.shape
    n = length // chunk

    def to_chunks(t):
        return t.reshape(bsz, n, chunk, *t.shape[2:]).swapaxes(0, 1)

    mask = jnp.tril(jnp.ones((chunk, chunk), bool))[None, :, :, None, None]

    def step(s, inp):
        qc, kc, vc, ac = inp
        b = jnp.cumsum(ac, axis=1)
        o_inter = jnp.einsum('bthk,bhkv->bthv', qc * jnp.exp(b), s)
        diff = b[:, :, None] - b[:, None, :]
        decay = jnp.exp(jnp.where(mask, diff, -jnp.inf))
        scores = jnp.einsum('bthk,bshk,btshk->bhts', qc, kc, decay)
        o_intra = jnp.einsum('bhts,bshv->bthv', scores, vc)
        b_last = b[:, -1]
        k_dec = kc * jnp.exp(b_last[:, None] - b)
        s_new = jnp.exp(b_last)[..., None] * s + jnp.einsum('bshk,bshv->bhkv', k_dec, vc)
        return s_new, o_inter + o_intra

    s_fin, o = lax.scan(step, s0, (to_chunks(q), to_chunks(k), to_chunks(v), to_chunks(log_a)))
    return o.swapaxes(0, 1).reshape(bsz, length, nh, v.shape[-1]), s_fin


def gla_mixer(pb, state, w_a2, b_a, norm_g):
    bsz, length, _ = pb.shape
    kd, vd = H_B * DK_B, H_B * DV_B
    q, k, v, g, a_lr = jnp.split(pb, [kd, 2 * kd, 2 * kd + vd, 2 * kd + 2 * vd], axis=-1)
    log_a = jax.nn.log_sigmoid((a_lr @ w_a2 + b_a).astype(F32)) / GLA_TAU
    q = q.astype(F32).reshape(bsz, length, H_B, DK_B) * (DK_B ** -0.5)
    k = k.astype(F32).reshape(bsz, length, H_B, DK_B)
    v = v.astype(F32).reshape(bsz, length, H_B, DV_B)
    log_a = log_a.reshape(bsz, length, H_B, DK_B)
    chunk = GLA_CHUNK if length % GLA_CHUNK == 0 else length
    o, s_new = gla_chunked(q, k, v, log_a, state.astype(F32), chunk)
    o = o * lax.rsqrt(jnp.mean(o * o, axis=-1, keepdims=True) + EPS)
    o = o.reshape(bsz, length, vd) * norm_g.astype(F32)
    y = o * jax.nn.silu(g.astype(F32))
    return y.astype(pb.dtype), s_new


def short_conv(pc, buf, conv_w):
    bg, cg, xv = jnp.split(pc, 3, axis=-1)
    z = cg * xv
    zpad = jnp.concatenate([buf.astype(z.dtype), z], axis=1)
    y = bg * causal_depthwise(zpad, conv_w)
    return y, zpad[:, -(CONV_C_W - 1):]


def fox_project(pd, b_f):
    bsz, length, _ = pd.shape
    hd = H_D * DH_D
    q = pd[..., :hd].reshape(bsz, length, H_D, DH_D)
    k = pd[..., hd:2 * hd].reshape(bsz, length, H_D, DH_D)
    v = pd[..., 2 * hd:3 * hd].reshape(bsz, length, H_D, DH_D)
    logf = jax.nn.log_sigmoid((pd[..., 3 * hd:] + b_f).astype(F32))
    return q, k, v, logf


def fox_prompt(q, k, v, logf):
    bsz, length, nh, dh = q.shape
    nb = length // Q_BLOCK
    c_keys = jnp.cumsum(logf, axis=1).transpose(0, 2, 1)
    q_blocks = q.reshape(bsz, nb, Q_BLOCK, nh, dh).swapaxes(0, 1)
    c_blocks = c_keys.reshape(bsz, nh, nb, Q_BLOCK).transpose(2, 0, 1, 3)
    k_pos = jnp.arange(length)
    scale = dh ** -0.5

    def one_block(args):
        blk, qb, cb = args
        s = jnp.einsum('bqhd,bkhd->bhqk', qb, k).astype(F32) * scale
        s = s + cb[..., :, None] - c_keys[:, :, None, :]
        q_pos = blk * Q_BLOCK + jnp.arange(Q_BLOCK)
        s = jnp.where(k_pos[None, :] <= q_pos[:, None], s, NEG_INF)
        p = jax.nn.softmax(s, axis=-1).astype(v.dtype)
        return jnp.einsum('bhqk,bkhd->bqhd', p, v)

    o = lax.map(one_block, (jnp.arange(nb), q_blocks, c_blocks))
    return o.swapaxes(0, 1).reshape(bsz, length, nh * dh)


def fox_decode(q, k, v, logf, k_past, v_past, logf_past):
    bsz, t_new, nh, dh = q.shape
    p_len = k_past.shape[1]
    scale = dh ** -0.5
    c_past = jnp.cumsum(logf_past.astype(F32), axis=1)
    c_new = jnp.cumsum(logf, axis=1).transpose(0, 2, 1)
    tail = (c_past[:, -1:] - c_past).transpose(0, 2, 1)
    bias_past = c_new[..., :, None] + tail[:, :, None, :]
    bias_new = c_new[..., :, None] - c_new[..., None, :]
    causal = jnp.tril(jnp.ones((t_new, t_new), bool))
    s_past = jnp.einsum('bthd,bshd->bhts', q, k_past.astype(q.dtype)).astype(F32) * scale + bias_past
    s_new = jnp.einsum('bthd,bshd->bhts', q, k).astype(F32) * scale + bias_new
    s_new = jnp.where(causal, s_new, NEG_INF)
    p = jax.nn.softmax(jnp.concatenate([s_past, s_new], axis=-1), axis=-1).astype(v.dtype)
    o = (jnp.einsum('bhts,bshd->bthd', p[..., :p_len], v_past.astype(v.dtype))
         + jnp.einsum('bhts,bshd->bthd', p[..., p_len:], v))
    return o.reshape(bsz, t_new, nh * dh)


def decoder_layer(x, lw, conv_a_buf, gla_state, conv_c_buf, fox_past):
    (f1_pre, f1_post, f1_wg, f1_wu, f1_wd, m_pre, m_post, w_in, w_out,
     a_cw, a_cb, a_lg, a_lb, b_wa2, b_ba, b_ng, c_cw, d_bf,
     f2_pre, f2_post, f2_wg, f2_wu, f2_wd) = lw
    bsz, length, _ = x.shape
    h = x + 0.5 * rms_norm(swiglu(rms_norm(x, f1_pre), f1_wg, f1_wu, f1_wd), f1_post)
    u = rms_norm(h, m_pre)
    p_all = u @ w_in
    ya, buf_a = conformer_conv(p_all[..., OFF_A:OFF_B], conv_a_buf, a_cw, a_cb, a_lg, a_lb)
    yb, s_b = gla_mixer(p_all[..., OFF_B:OFF_C], gla_state, b_wa2, b_ba, b_ng)
    yc, buf_c = short_conv(p_all[..., OFF_C:OFF_D], conv_c_buf, c_cw)
    q, k, v, logf = fox_project(p_all[..., OFF_D:N_IN], d_bf)
    if fox_past is None:
        yd = fox_prompt(q, k, v, logf)
    else:
        yd = fox_decode(q, k, v, logf, fox_past[0], fox_past[1], fox_past[2])
    y = jnp.concatenate([ya, yb, yc, yd], axis=-1) @ w_out
    h = h + rms_norm(y, m_post)
    h = h + 0.5 * rms_norm(swiglu(rms_norm(h, f2_pre), f2_wg, f2_wu, f2_wd), f2_post)
    return h, (buf_a, s_b, buf_c, k, v, logf)


def setup_inputs(seed: int = 0) -> dict:
    key = jax.random.key(seed)
    ks = jax.random.split(key, 40)
    n_pages = PAST_LEN // PAGE_SIZE
    n_used = DEC_BATCH * n_pages
    n_pool = n_used + max(1, n_used // 4)

    def nrm(k, shape, scale):
        return scale * jax.random.normal(k, shape, F32)

    def gain(k, shape):
        return 1.0 + 0.05 * jax.random.normal(k, shape, F32)

    page_table = jax.random.permutation(ks[8], n_pool)[:n_used].reshape(DEC_BATCH, n_pages).astype(jnp.int32)
    return {
        'x_prompt': nrm(ks[0], (BATCH, SEQ, D_MODEL), 1.0),
        'x_sample': nrm(ks[1], (DEC_BATCH, DEC_SEQ, D_MODEL), 1.0),
        'state_conv_a': nrm(ks[2], (DEPTH, DEC_BATCH, CONV_A_W - 1, C_A), 0.5),
        'state_gla': nrm(ks[3], (DEPTH, DEC_BATCH, H_B, DK_B, DV_B), 0.5),
        'state_conv_c': nrm(ks[4], (DEPTH, DEC_BATCH, CONV_C_W - 1, C_C), 0.5),
        'cache_k': nrm(ks[5], (DEPTH, n_pool, PAGE_SIZE, H_D, DH_D), 1.0),
        'cache_v': nrm(ks[6], (DEPTH, n_pool, PAGE_SIZE, H_D, DH_D), 1.0),
        'cache_logf': jax.nn.log_sigmoid(FORGET_BIAS_INIT + jax.random.normal(ks[7], (DEPTH, n_pool, PAGE_SIZE, H_D), F32)),
        'page_table': page_table,
        'ffn1_pre_g': gain(ks[9], (DEPTH, D_MODEL)),
        'ffn1_post_g': gain(ks[10], (DEPTH, D_MODEL)),
        'ffn1_w_gate': nrm(ks[11], (DEPTH, D_MODEL, D_FF), D_MODEL ** -0.5),
        'ffn1_w_up': nrm(ks[12], (DEPTH, D_MODEL, D_FF), D_MODEL ** -0.5),
        'ffn1_w_down': nrm(ks[13], (DEPTH, D_FF, D_MODEL), D_FF ** -0.5),
        'mix_pre_g': gain(ks[14], (DEPTH, D_MODEL)),
        'mix_post_g': gain(ks[15], (DEPTH, D_MODEL)),
        'w_in': nrm(ks[16], (DEPTH, D_MODEL, N_IN), D_MODEL ** -0.5),
        'w_out': nrm(ks[17], (DEPTH, D_MIX, D_MODEL), D_MIX ** -0.5),
        'a_conv_w': nrm(ks[18], (DEPTH, CONV_A_W, C_A), CONV_A_W ** -0.5),
        'a_conv_b': nrm(ks[19], (DEPTH, C_A), 0.02),
        'a_ln_g': gain(ks[20], (DEPTH, C_A)),
        'a_ln_b': nrm(ks[21], (DEPTH, C_A), 0.02),
        'b_gate_w2': nrm(ks[22], (DEPTH, GLA_RANK, H_B * DK_B), GLA_RANK ** -0.5),
        'b_gate_b': nrm(ks[23], (DEPTH, H_B * DK_B), 0.1),
        'b_out_norm_g': gain(ks[24], (DEPTH, H_B * DV_B)),
        'c_conv_w': nrm(ks[25], (DEPTH, CONV_C_W, C_C), CONV_C_W ** -0.5),
        'd_forget_b': FORGET_BIAS_INIT + nrm(ks[26], (DEPTH, H_D), 0.1),
        'ffn2_pre_g': gain(ks[27], (DEPTH, D_MODEL)),
        'ffn2_post_g': gain(ks[28], (DEPTH, D_MODEL)),
        'ffn2_w_gate': nrm(ks[29], (DEPTH, D_MODEL, D_FF), D_MODEL ** -0.5),
        'ffn2_w_up': nrm(ks[30], (DEPTH, D_MODEL, D_FF), D_MODEL ** -0.5),
        'ffn2_w_down': nrm(ks[31], (DEPTH, D_FF, D_MODEL), D_FF ** -0.5),
    }


def reference(x_prompt, x_sample, state_conv_a, state_gla, state_conv_c, cache_k, cache_v, cache_logf,
              page_table, ffn1_pre_g, ffn1_post_g, ffn1_w_gate, ffn1_w_up, ffn1_w_down, mix_pre_g,
              mix_post_g, w_in, w_out, a_conv_w, a_conv_b, a_ln_g, a_ln_b, b_gate_w2, b_gate_b,
              b_out_norm_g, c_conv_w, d_forget_b, ffn2_pre_g, ffn2_post_g, ffn2_w_gate, ffn2_w_up,
              ffn2_w_down):
    bp = x_prompt.shape[0]
    bd = page_table.shape[0]
    hp, hs = x_prompt, x_sample
    p_ca, p_gla, p_cc, p_k, p_v, p_lf = [], [], [], [], [], []
    s_ca, s_gla, s_cc, s_k, s_v, s_lf = [], [], [], [], [], []
    for l in range(DEPTH):
        lw = (ffn1_pre_g[l], ffn1_post_g[l], ffn1_w_gate[l], ffn1_w_up[l], ffn1_w_down[l],
              mix_pre_g[l], mix_post_g[l], w_in[l], w_out[l],
              a_conv_w[l], a_conv_b[l], a_ln_g[l], a_ln_b[l],
              b_gate_w2[l], b_gate_b[l], b_out_norm_g[l], c_conv_w[l], d_forget_b[l],
              ffn2_pre_g[l], ffn2_post_g[l], ffn2_w_gate[l], ffn2_w_up[l], ffn2_w_down[l])
        hp, st = decoder_layer(
            hp, lw,
            jnp.zeros((bp, CONV_A_W - 1, C_A), hp.dtype),
            jnp.zeros((bp, H_B, DK_B, DV_B), F32),
            jnp.zeros((bp, CONV_C_W - 1, C_C), hp.dtype),
            None)
        p_ca.append(st[0]); p_gla.append(st[1]); p_cc.append(st[2])
        p_k.append(st[3]); p_v.append(st[4]); p_lf.append(st[5])
        k_past = cache_k[l][page_table].reshape(bd, -1, H_D, DH_D)
        v_past = cache_v[l][page_table].reshape(bd, -1, H_D, DH_D)
        lf_past = cache_logf[l][page_table].reshape(bd, -1, H_D)
        hs, st = decoder_layer(hs, lw, state_conv_a[l], state_gla[l], state_conv_c[l],
                               (k_past, v_past, lf_past))
        s_ca.append(st[0]); s_gla.append(st[1]); s_cc.append(st[2])
        s_k.append(st[3]); s_v.append(st[4]); s_lf.append(st[5])
    return (hp, hs,
            jnp.stack(p_ca), jnp.stack(p_gla), jnp.stack(p_cc), jnp.stack(p_k), jnp.stack(p_v), jnp.stack(p_lf),
            jnp.stack(s_ca), jnp.stack(s_gla), jnp.stack(s_cc), jnp.stack(s_k), jnp.stack(s_v), jnp.stack(s_lf))
```

```python
import functools

import jax
import jax.numpy as jnp
from jax import lax
from jax.experimental import pallas as pl
from jax.experimental.pallas import tpu as pltpu

F32 = jnp.float32
BF16 = jnp.bfloat16

D_MODEL = 1024
GROUP_W = D_MODEL // 4
HEAD_DIM = 64
N_HEADS = GROUP_W // HEAD_DIM
DK_B = HEAD_DIM // 2
GLA_RANK = 16
GLA_TAU = 16.0
GLA_CHUNK = 64
CONV_A_W = 31
CONV_C_W = 3
PAGE_SIZE = 128
EPS = 1e-6
NEG_INF = -1e30

LANES = 128
SUBLANES = 8
VMEM_LIMIT_BYTES = 56 * 1024 * 1024

N_PA = 2 * GROUP_W
KD = N_HEADS * DK_B
VD = N_HEADS * HEAD_DIM
N_PB = 2 * KD + 2 * VD + LANES
N_PC = 3 * GROUP_W
N_PD = 3 * VD + LANES


def _cparams(*sem):
    return pltpu.CompilerParams(dimension_semantics=sem, vmem_limit_bytes=VMEM_LIMIT_BYTES)


def _dot(a, b):
    return jnp.dot(a, b, preferred_element_type=F32)


def _dot_nt(a, b):
    return lax.dot_general(a, b, (((1,), (1,)), ((), ())), preferred_element_type=F32)


def _split3(x):
    hi = x.astype(BF16)
    r = x - hi.astype(F32)
    mid = r.astype(BF16)
    lo = (r - mid.astype(F32)).astype(BF16)
    return hi, mid, lo


def _dot_sel(x, sel):
    hi, mid, lo = _split3(x)
    return _dot(hi, sel) + _dot(mid, sel) + _dot(lo, sel)


def _sel_dot(sel, x):
    hi, mid, lo = _split3(x)
    return _dot(sel, hi) + _dot(sel, mid) + _dot(sel, lo)


def _rms(x, g):
    return x * lax.rsqrt(jnp.mean(x * x, axis=-1, keepdims=True) + EPS) * g


def _sigmoid(x):
    return 1.0 / (1.0 + jnp.exp(-x))


def _silu(x):
    return x * _sigmoid(x)


def _log_sigmoid(x):
    return jnp.minimum(x, 0.0) - jnp.log1p(jnp.exp(-jnp.abs(x)))


def _iota(shape, dim):
    return lax.broadcasted_iota(jnp.int32, shape, dim)


def _idiv(x, n):
    assert n & (n - 1) == 0
    return lax.shift_right_logical(x, n.bit_length() - 1)


def _imod(x, n):
    assert n & (n - 1) == 0
    return x & (n - 1)


def _head_mean_matrix():
    r = _idiv(_iota((VD, VD), 0), HEAD_DIM)
    c = _idiv(_iota((VD, VD), 1), HEAD_DIM)
    return (r == c).astype(BF16)


def _swiglu_residual(x, pre_g, post_g, wg_ref, wu_ref, wd_ref, acc_ref, fc):
    xn = _rms(x, pre_g).astype(BF16)
    for c in range(wg_ref.shape[1] // fc):
        sl = slice(c * fc, (c + 1) * fc)
        g = _dot(xn, wg_ref[:, sl])
        u = _dot(xn, wu_ref[:, sl])
        hid = (_silu(g) * u).astype(BF16)
        part = _dot(hid, wd_ref[sl, :])
        if c == 0:
            acc_ref[...] = part
        else:
            acc_ref[...] += part
    return x + 0.5 * _rms(acc_ref[...], post_g)


def _ffn_body(x_ref, pre_ref, post_ref, wg_ref, wu_ref, wd_ref, o_ref, acc_ref, *, fc):
    o_ref[...] = _swiglu_residual(x_ref[...], pre_ref[...], post_ref[...],
                                  wg_ref, wu_ref, wd_ref, acc_ref, fc)


def _mix_ffn_body(h_ref, ya_ref, yb_ref, yc_ref, yd_ref, wo_ref, mpost_ref,
                  pre_ref, post_ref, wg_ref, wu_ref, wd_ref, o_ref, acc_ref, *, fc):
    ycat = jnp.concatenate([ya_ref[...], yb_ref[...], yc_ref[...], yd_ref[...]], axis=1)
    y = _dot(ycat.astype(BF16), wo_ref[...])
    h = h_ref[...] + _rms(y, mpost_ref[...])
    o_ref[...] = _swiglu_residual(h, pre_ref[...], post_ref[...],
                                  wg_ref, wu_ref, wd_ref, acc_ref, fc)


def _row_spec(tm, n):
    return pl.BlockSpec((tm, n), lambda i: (i, 0))


def _full_spec(shape):
    return pl.BlockSpec(shape, lambda i: (0,) * len(shape), pipeline_mode=pl.Buffered(1))


def _ffn_call(x, pre_g, post_g, wg, wu, wd, tm, fc=256):
    m, d = x.shape
    f = wg.shape[1]
    return pl.pallas_call(
        functools.partial(_ffn_body, fc=fc),
        grid=(m // tm,),
        in_specs=[_row_spec(tm, d), _full_spec((1, d)), _full_spec((1, d)),
                  _full_spec((d, f)), _full_spec((d, f)), _full_spec((f, d))],
        out_specs=_row_spec(tm, d),
        out_shape=jax.ShapeDtypeStruct((m, d), F32),
        scratch_shapes=[pltpu.VMEM((tm, d), F32)],
        compiler_params=_cparams("arbitrary"),
        name="ffn",
    )(x, pre_g, post_g, wg, wu, wd)


def _mix_ffn_call(h, ya, yb, yc, yd, wo, mpost, pre_g, post_g, wg, wu, wd, tm, fc=256):
    m, d = h.shape
    f = wg.shape[1]
    return pl.pallas_call(
        functools.partial(_mix_ffn_body, fc=fc),
        grid=(m // tm,),
        in_specs=[_row_spec(tm, d)] + [_row_spec(tm, GROUP_W)] * 4
                 + [_full_spec((d, d)), _full_spec((1, d)), _full_spec((1, d)), _full_spec((1, d)),
                    _full_spec((d, f)), _full_spec((d, f)), _full_spec((f, d))],
        out_specs=_row_spec(tm, d),
        out_shape=jax.ShapeDtypeStruct((m, d), F32),
        scratch_shapes=[pltpu.VMEM((tm, d), F32)],
        compiler_params=_cparams("arbitrary"),
        name="mix_ffn",
    )(h, ya, yb, yc, yd, wo, mpost, pre_g, post_g, wg, wu, wd)


def _inproj_body(h_ref, g_ref, wa_ref, wb_ref, wc_ref, wd_ref, bf_ref,
                 pa_ref, pb_ref, pc_ref, q_ref, k_ref, v_ref, lf_ref):
    u = _rms(h_ref[...], g_ref[...]).astype(BF16)
    pa_ref[...] = _dot(u, wa_ref[...])
    pb_ref[...] = _dot(u, wb_ref[...])
    pc_ref[...] = _dot(u, wc_ref[...])
    pd = _dot(u, wd_ref[...])
    q_ref[...] = pd[:, 0:VD]
    k_ref[...] = pd[:, VD:2 * VD]
    v_ref[...] = pd[:, 2 * VD:3 * VD]
    lf_ref[...] = _log_sigmoid(pd[:, 3 * VD:] + bf_ref[...])


def _inproj_call(h, g, wa, wb, wc, wd, bf, tm):
    m, d = h.shape
    widths = (N_PA, N_PB, N_PC, VD, VD, VD, LANES)
    return pl.pallas_call(
        _inproj_body,
        grid=(m // tm,),
        in_specs=[_row_spec(tm, d), _full_spec((1, d)), _full_spec((d, N_PA)), _full_spec((d, N_PB)),
                  _full_spec((d, N_PC)), _full_spec((d, N_PD)), _full_spec((1, LANES))],
        out_specs=[_row_spec(tm, w) for w in widths],
        out_shape=[jax.ShapeDtypeStruct((m, w), F32) for w in widths],
        compiler_params=_cparams("arbitrary"),
        name="inproj",
    )(h, g, wa, wb, wc, wd, bf)


A_PAD = 32
C_PAD = 8
CONV_ROWS = 128


def _conv_prompt_body(pa_ref, pc_ref, aw_ref, ab_ref, lg_ref, lb_ref, cw_ref,
                      ya_ref, yc_ref, bufa_ref, bufc_ref, apad_ref, zpad_ref):
    seq = pa_ref.shape[1]
    c = GROUP_W
    apad_ref[0:A_PAD, :] = jnp.zeros((A_PAD, c), F32)
    apad_ref[A_PAD:, :] = pa_ref[0, :, 0:c] * _sigmoid(pa_ref[0, :, c:])
    zpad_ref[0:C_PAD, :] = jnp.zeros((C_PAD, c), F32)
    zpad_ref[C_PAD:, :] = pc_ref[0, :, c:2 * c] * pc_ref[0, :, 2 * c:]

    def step(i, carry):
        r0 = pl.multiple_of(i * CONV_ROWS, CONV_ROWS)
        win = apad_ref[pl.ds(r0, CONV_ROWS + A_PAD), :]
        acc = jnp.zeros((CONV_ROWS, c), F32) + ab_ref[...]
        for r in range(SUBLANES):
            nrow = CONV_ROWS if r == 0 else CONV_ROWS + SUBLANES
            u = None
            for a8 in range(0, A_PAD + 1, SUBLANES):
                j = a8 + r - (A_PAD - (CONV_A_W - 1))
                if 0 <= j < CONV_A_W:
                    term = aw_ref[j:j + 1, :] * win[a8:a8 + nrow, :]
                    u = term if u is None else u + term
            acc = acc + u[r:r + CONV_ROWS, :]
        mu = jnp.mean(acc, axis=-1, keepdims=True)
        var = jnp.mean(jnp.square(acc - mu), axis=-1, keepdims=True)
        yn = (acc - mu) * lax.rsqrt(var + EPS) * lg_ref[...] + lb_ref[...]
        ya_ref[0, pl.ds(r0, CONV_ROWS), :] = _silu(yn)
        zwin = zpad_ref[pl.ds(r0, CONV_ROWS + C_PAD), :]
        accc = jnp.zeros((CONV_ROWS, c), F32)
        for j in range(CONV_C_W):
            off = C_PAD - (CONV_C_W - 1) + j
            accc = accc + cw_ref[j:j + 1, :] * zwin[off:off + CONV_ROWS, :]
        yc_ref[0, pl.ds(r0, CONV_ROWS), :] = pc_ref[0, pl.ds(r0, CONV_ROWS), 0:c] * accc
        return carry

    lax.fori_loop(0, seq // CONV_ROWS, step, 0)
    na = CONV_A_W - 1
    nc = CONV_C_W - 1
    bufa_ref[0] = apad_ref[seq:A_PAD + seq, :][A_PAD - na:, :]
    bufc_ref[0] = zpad_ref[seq:C_PAD + seq, :][C_PAD - nc:, :]


def _conv_prompt_call(pa, pc, aw, ab, lg, lb, cw):
    b, seq, _ = pa.shape
    c = GROUP_W
    bspec = lambda n: pl.BlockSpec((1, seq, n), lambda i: (i, 0, 0))
    return pl.pallas_call(
        _conv_prompt_body,
        grid=(b,),
        in_specs=[bspec(N_PA), bspec(N_PC), _full_spec((CONV_A_W, c)), _full_spec((1, c)),
                  _full_spec((1, c)), _full_spec((1, c)), _full_spec((CONV_C_W, c))],
        out_specs=[bspec(c), bspec(c),
                   pl.BlockSpec((1, CONV_A_W - 1, c), lambda i: (i, 0, 0)),
                   pl.BlockSpec((1, CONV_C_W - 1, c), lambda i: (i, 0, 0))],
        out_shape=[jax.ShapeDtypeStruct((b, seq, c), F32), jax.ShapeDtypeStruct((b, seq, c), F32),
                   jax.ShapeDtypeStruct((b, CONV_A_W - 1, c), F32),
                   jax.ShapeDtypeStruct((b, CONV_C_W - 1, c), F32)],
        scratch_shapes=[pltpu.VMEM((A_PAD + seq, c), F32), pltpu.VMEM((C_PAD + seq, c), F32)],
        compiler_params=_cparams("arbitrary"),
        name="conv_prompt",
    )(pa, pc, aw, ab, lg, lb, cw)


def _gla_gate(alr, wa2_ref, ba_ref):
    return _log_sigmoid(_dot(alr.astype(BF16), wa2_ref[...]) + ba_ref[...]) * (1.0 / GLA_TAU)


def _gla_out(o, g, ng, hm):
    ms = _dot_sel(o * o, hm) * (1.0 / HEAD_DIM)
    return o * lax.rsqrt(ms + EPS) * ng * _silu(g)


def _gla_prompt_body(pb_ref, wa2_ref, ba_ref, ng_ref, y_ref, st_ref, la_ref, s_ref):
    seq = pb_ref.shape[1]
    ck = GLA_CHUNK
    la_ref[...] = _gla_gate(pb_ref[0, :, 2 * KD + 2 * VD:], wa2_ref, ba_ref)
    s_ref[...] = jnp.zeros((KD, VD), F32)

    tri = (_iota((ck, ck), 1) <= _iota((ck, ck), 0)).astype(BF16)
    eye = (_iota((KD, KD), 0) == _iota((KD, KD), 1)).astype(F32)
    blk = _idiv(_iota((KD, VD), 0), DK_B) == _idiv(_iota((KD, VD), 1), HEAD_DIM)
    blk_bf = blk.astype(BF16)
    blk_f = blk.astype(F32)
    hm = _head_mean_matrix()
    scale = DK_B ** -0.5

    def chunk(ci, carry):
        r0 = pl.multiple_of(ci * ck, ck)
        rows = pl.ds(r0, ck)
        q = pb_ref[0, rows, 0:KD] * scale
        k = pb_ref[0, rows, KD:2 * KD]
        v = pb_ref[0, rows, 2 * KD:2 * KD + VD]
        g = pb_ref[0, rows, 2 * KD + VD:2 * KD + 2 * VD]
        b = _sel_dot(tri, la_ref[rows, :])
        b_last = b[ck - 1:ck, :]
        s_old = s_ref[...]
        o = _dot((q * jnp.exp(b)).astype(BF16), s_old.astype(BF16))
        parts = []
        for g8 in range(ck // SUBLANES):
            t0 = g8 * SUBLANES
            nt = ck - t0
            bt = b[t0:, :]
            qt = q[t0:, :]
            tpos = _iota((nt, KD), 0) + t0
            es = []
            for s in range(t0, t0 + SUBLANES):
                diff = jnp.where(tpos >= s, bt - b[s:s + 1, :], -jnp.inf)
                es.append((qt * k[s:s + 1, :] * jnp.exp(diff)).astype(BF16))
            r = _dot(jnp.concatenate(es, axis=0), blk_bf)
            acc = jnp.zeros((nt, VD), F32)
            for si in range(SUBLANES):
                acc = acc + r[si * nt:(si + 1) * nt, :] * v[t0 + si:t0 + si + 1, :]
            parts.append(acc)
        rows_out = []
        for t8 in range(ck // SUBLANES):
            tot = None
            for g8 in range(t8 + 1):
                off = (t8 - g8) * SUBLANES
                piece = parts[g8][off:off + SUBLANES, :]
                tot = piece if tot is None else tot + piece
            rows_out.append(tot)
        o = o + jnp.concatenate(rows_out, axis=0)
        y_ref[0, rows, :] = _gla_out(o, g, ng_ref[...], hm)
        k_dec = k * jnp.exp(b_last - b)
        upd = _dot(k_dec.T.astype(BF16), v.astype(BF16))
        decay_col = jnp.exp(jnp.sum(eye * b_last, axis=1, keepdims=True))
        s_ref[...] = decay_col * s_old + upd * blk_f
        return carry

    lax.fori_loop(0, seq // ck, chunk, 0)
    for h in range(N_HEADS):
        st_ref[0, h] = s_ref[h * DK_B:(h + 1) * DK_B, h * HEAD_DIM:(h + 1) * HEAD_DIM]


def _gla_prompt_call(pb, wa2, ba, ng):
    b, seq, _ = pb.shape
    return pl.pallas_call(
        _gla_prompt_body,
        grid=(b,),
        in_specs=[pl.BlockSpec((1, seq, N_PB), lambda i: (i, 0, 0)), _full_spec((LANES, KD)),
                  _full_spec((1, KD)), _full_spec((1, VD))],
        out_specs=[pl.BlockSpec((1, seq, VD), lambda i: (i, 0, 0)),
                   pl.BlockSpec((1, N_HEADS, DK_B, HEAD_DIM), lambda i: (i, 0, 0, 0))],
        out_shape=[jax.ShapeDtypeStruct((b, seq, VD), F32),
                   jax.ShapeDtypeStruct((b, N_HEADS, DK_B, HEAD_DIM), F32)],
        scratch_shapes=[pltpu.VMEM((seq, KD), F32), pltpu.VMEM((KD, VD), F32)],
        compiler_params=_cparams("arbitrary"),
        name="gla_prompt",
    )(pb, wa2, ba, ng)


CUM_ROWS = 256


def _fox_cumsum_body(lf_ref, col_ref, row_ref):
    seq = lf_ref.shape[1]
    tri = (_iota((CUM_ROWS, CUM_ROWS), 1) <= _iota((CUM_ROWS, CUM_ROWS), 0)).astype(BF16)
    carry = jnp.zeros((1, LANES), F32)
    for i in range(seq // CUM_ROWS):
        rows = slice(i * CUM_ROWS, (i + 1) * CUM_ROWS)
        c = _sel_dot(tri, lf_ref[0, rows, :]) + carry
        col_ref[0, rows, :] = c
        row_ref[0, i] = c.T[0:SUBLANES, :]
        carry = c[CUM_ROWS - 1:CUM_ROWS, :]


def _fox_cumsum_call(lf):
    b, seq, _ = lf.shape
    return pl.pallas_call(
        _fox_cumsum_body,
        grid=(b,),
        in_specs=[pl.BlockSpec((1, seq, LANES), lambda i: (i, 0, 0))],
        out_specs=[pl.BlockSpec((1, seq, LANES), lambda i: (i, 0, 0)),
                   pl.BlockSpec((1, seq // CUM_ROWS, SUBLANES, CUM_ROWS), lambda i: (i, 0, 0, 0))],
        out_shape=[jax.ShapeDtypeStruct((b, seq, LANES), F32),
                   jax.ShapeDtypeStruct((b, seq // CUM_ROWS, SUBLANES, CUM_ROWS), F32)],
        compiler_params=_cparams("arbitrary"),
        name="fox_cumsum",
    )(lf)


def _fox_prompt_body(q_ref, k_ref, v_ref, cc_ref, cr_ref, o_ref, *, tq):
    qi = pl.program_id(1)
    scale = HEAD_DIM ** -0.5
    row = _iota((tq, tq), 0)
    col = _iota((tq, tq), 1)
    outs = []
    for h in range(N_HEADS):
        lanes = slice(h * HEAD_DIM, (h + 1) * HEAD_DIM)
        qh = q_ref[0, :, lanes].astype(BF16)
        cq = cc_ref[0, :, h:h + 1]

        def step(ki, carry, masked):
            m, l, acc = carry
            ks = pl.ds(pl.multiple_of(ki * tq, tq), tq)
            kh = k_ref[0, ks, lanes].astype(BF16)
            vh = v_ref[0, ks, lanes].astype(BF16)
            s = _dot_nt(qh, kh) * scale + cq - cr_ref[0, ki, h:h + 1, :]
            if masked:
                s = jnp.where(col <= row, s, NEG_INF)
            m_new = jnp.maximum(m, jnp.max(s, axis=1, keepdims=True))
            alpha = jnp.exp(m - m_new)
            p = jnp.exp(s - m_new)
            l = alpha * l + jnp.sum(p, axis=1, keepdims=True)
            acc = alpha * acc + _dot(p.astype(BF16), vh)
            return m_new, l, acc

        init = (jnp.full((tq, 1), NEG_INF, F32), jnp.zeros((tq, 1), F32),
                jnp.zeros((tq, HEAD_DIM), F32))
        carry = lax.fori_loop(0, qi, functools.partial(step, masked=False), init)
        _, l, acc = step(qi, carry, True)
        outs.append(acc / l)
    o_ref[0] = jnp.concatenate(outs, axis=1)


def _fox_prompt_call(q, k, v, ccol, crow, tq):
    b, seq, _ = q.shape
    full = lambda n: pl.BlockSpec((1, seq, n), lambda i, j: (i, 0, 0))
    return pl.pallas_call(
        functools.partial(_fox_prompt_body, tq=tq),
        grid=(b, seq // tq),
        in_specs=[pl.BlockSpec((1, tq, VD), lambda i, j: (i, j, 0)), full(VD), full(VD),
                  pl.BlockSpec((1, tq, LANES), lambda i, j: (i, j, 0)),
                  pl.BlockSpec((1, seq // tq, SUBLANES, tq), lambda i, j: (i, 0, 0, 0))],
        out_specs=pl.BlockSpec((1, tq, VD), lambda i, j: (i, j, 0)),
        out_shape=jax.ShapeDtypeStruct((b, seq, VD), F32),
        compiler_params=_cparams("arbitrary", "arbitrary"),
        name="fox_prompt",
    )(q, k, v, ccol, crow)


def _mix_sample_body(pa_ref, pb_ref, pc_ref, bufa_ref, sg_ref, bufc_ref,
                     aw_ref, ab_ref, lg_ref, lb_ref, wa2_ref, ba_ref, ng_ref, cw_ref,
                     ya_ref, yb_ref, yc_ref, bufa_o, sg_o, bufc_o):
    c = GROUP_W
    a = pa_ref[:, 0:c] * _sigmoid(pa_ref[:, c:])
    na = CONV_A_W - 1
    acc = aw_ref[na:na + 1, :] * a + ab_ref[...]
    for j in range(na):
        acc = acc + aw_ref[j:j + 1, :] * bufa_ref[:, j * c:(j + 1) * c]
    mu = jnp.mean(acc, axis=-1, keepdims=True)
    var = jnp.mean(jnp.square(acc - mu), axis=-1, keepdims=True)
    ya_ref[...] = _silu((acc - mu) * lax.rsqrt(var + EPS) * lg_ref[...] + lb_ref[...])
    bufa_o[:, 0:(na - 1) * c] = bufa_ref[:, c:]
    bufa_o[:, (na - 1) * c:] = a
    z = pc_ref[:, c:2 * c] * pc_ref[:, 2 * c:]
    conv = cw_ref[0:1, :] * bufc_ref[:, 0:c] + cw_ref[1:2, :] * bufc_ref[:, c:] + cw_ref[2:3, :] * z
    yc_ref[...] = pc_ref[:, 0:c] * conv
    bufc_o[:, 0:c] = bufc_ref[:, c:]
    bufc_o[:, c:] = z
    q = pb_ref[:, 0:KD] * (DK_B ** -0.5)
    k = pb_ref[:, KD:2 * KD]
    v = pb_ref[:, 2 * KD:2 * KD + VD]
    g = pb_ref[:, 2 * KD + VD:2 * KD + 2 * VD]
    decay = jnp.exp(_gla_gate(pb_ref[:, 2 * KD + 2 * VD:], wa2_ref, ba_ref))
    hs = DK_B * HEAD_DIM
    o = jnp.zeros((pb_ref.shape[0], VD), F32)
    for h in range(N_HEADS):
        ek = (_iota((KD, hs), 0) == h * DK_B + _idiv(_iota((KD, hs), 1), HEAD_DIM)).astype(BF16)
        ev = (_iota((VD, hs), 0) == h * HEAD_DIM + _imod(_iota((VD, hs), 1), HEAD_DIM)).astype(BF16)
        evt = (_iota((hs, VD), 1) == h * HEAD_DIM + _imod(_iota((hs, VD), 0), HEAD_DIM)).astype(BF16)
        s_new = (_dot_sel(decay, ek) * sg_ref[:, h * hs:(h + 1) * hs]
                 + _dot_sel(k, ek) * _dot_sel(v, ev))
        sg_o[:, h * hs:(h + 1) * hs] = s_new
        o = o + _dot_sel(_dot_sel(q, ek) * s_new, evt)
    yb_ref[...] = _gla_out(o, g, ng_ref[...], _head_mean_matrix())


def _mix_sample_call(pa, pb, pc, bufa, sg, bufc, aw, ab, lg, lb, wa2, ba, ng, cw):
    n = pa.shape[0]
    c = GROUP_W
    args = (pa, pb, pc, bufa, sg, bufc, aw, ab, lg, lb, wa2, ba, ng, cw)
    out_shape = [jax.ShapeDtypeStruct((n, c), F32)] * 3 + [
        jax.ShapeDtypeStruct(bufa.shape, F32), jax.ShapeDtypeStruct(sg.shape, F32),
        jax.ShapeDtypeStruct(bufc.shape, F32)]
    return pl.pallas_call(
        _mix_sample_body,
        grid=(1,),
        in_specs=[_full_spec(x.shape) for x in args],
        out_specs=[pl.BlockSpec(s.shape, lambda i: (0, 0)) for s in out_shape],
        out_shape=out_shape,
        compiler_params=_cparams("arbitrary"),
        name="mix_sample",
    )(*args)


def _logf_pages_body(x_ref, o_ref):
    n = x_ref.shape[0]
    x = x_ref[...]
    later = (_iota((PAGE_SIZE, PAGE_SIZE), 0) > _iota((PAGE_SIZE, PAGE_SIZE), 1)).astype(BF16)
    ones = jnp.ones((PAGE_SIZE, PAGE_SIZE), BF16)
    within = _dot_sel(x, later)
    total = pltpu.roll(_dot_sel(x, ones), N_HEADS, 0)
    o_ref[...] = jnp.where(_imod(_iota((n, PAGE_SIZE), 0), SUBLANES) < N_HEADS, within, total)


def _logf_pages_call(x, tm):
    m = x.shape[0]
    return pl.pallas_call(
        _logf_pages_body,
        grid=(m // tm,),
        in_specs=[_row_spec(tm, PAGE_SIZE)],
        out_specs=_row_spec(tm, PAGE_SIZE),
        out_shape=jax.ShapeDtypeStruct((m, PAGE_SIZE), F32),
        compiler_params=_cparams("arbitrary"),
        name="logf_pages",
    )(x)


def _fox_decode_body(pt_ref, q_ref, kn_ref, vn_ref, lfn_ref, kc_hbm, vc_hbm, lfc_hbm,
                     o_ref, kbuf, vbuf, lfbuf, sem, *, n_pages):
    b = pl.program_id(0)
    nb = pl.num_programs(0)
    slot = b % 2
    past = n_pages * PAGE_SIZE

    def copies(bb, sl):
        out = []
        for p in range(n_pages):
            page = pt_ref[bb, p]
            rows = pl.ds(p * PAGE_SIZE, PAGE_SIZE)
            out.append(pltpu.make_async_copy(kc_hbm.at[page], kbuf.at[sl, rows, :], sem.at[sl, 0]))
            out.append(pltpu.make_async_copy(vc_hbm.at[page], vbuf.at[sl, rows, :], sem.at[sl, 1]))
            out.append(pltpu.make_async_copy(lfc_hbm.at[page], lfbuf.at[sl, :, rows], sem.at[sl, 2]))
        return out

    @pl.when(b == 0)
    def _():
        for cp in copies(0, 0):
            cp.start()

    @pl.when(b + 1 < nb)
    def _():
        for cp in copies(b + 1, 1 - slot):
            cp.start()

    for cp in copies(b, slot):
        cp.wait()

    scale = HEAD_DIM ** -0.5
    hm = (_idiv(_iota((SUBLANES, VD), 1), HEAD_DIM) == _iota((SUBLANES, VD), 0)).astype(F32)
    qbd = jnp.broadcast_to(q_ref[pl.ds(b, 1), :], (SUBLANES, VD)) * hm
    s = _dot_nt(qbd.astype(BF16), kbuf[slot].astype(BF16)) * scale
    lf = lfbuf[slot]
    incl = pltpu.roll(lf, N_HEADS, 0)
    sh = PAGE_SIZE
    while sh < past:
        incl = incl + jnp.concatenate([incl[:, sh:], jnp.zeros((SUBLANES, sh), F32)], axis=1)
        sh *= 2
    later_pages = jnp.concatenate([incl[:, PAGE_SIZE:], jnp.zeros((SUBLANES, PAGE_SIZE), F32)], axis=1)
    valid = _iota((SUBLANES, past), 0) < N_HEADS
    logits = jnp.where(valid, s + lf + later_pages, 0.0)
    s_self = jnp.sum(qbd * kn_ref[pl.ds(b, 1), :], axis=1, keepdims=True) * scale
    pick = (_iota((SUBLANES, LANES), 1) == _iota((SUBLANES, LANES), 0)).astype(F32)
    c_new = jnp.sum(pick * lfn_ref[pl.ds(b, 1), :], axis=1, keepdims=True)
    self_logit = jnp.where(_iota((SUBLANES, 1), 0) < N_HEADS, s_self - c_new, 0.0)
    m = jnp.maximum(jnp.max(logits, axis=1, keepdims=True), self_logit)
    p = jnp.exp(logits - m)
    p_self = jnp.exp(self_logit - m)
    l = jnp.sum(p, axis=1, keepdims=True) + p_self
    pv = _dot(p.astype(BF16), vbuf[slot].astype(BF16))
    o8 = (pv + p_self * vn_ref[pl.ds(b, 1), :]) / l
    o_ref[0] = jnp.sum(o8 * hm, axis=0, keepdims=True)


def _fox_decode_call(page_table, q, kn, vn, lfn, kc, vc, lfc):
    n, n_pages = page_table.shape
    past = n_pages * PAGE_SIZE
    grid_spec = pltpu.PrefetchScalarGridSpec(
        num_scalar_prefetch=1,
        grid=(n,),
        in_specs=[pl.BlockSpec((n, VD), lambda i, pt: (0, 0)), pl.BlockSpec((n, VD), lambda i, pt: (0, 0)),
                  pl.BlockSpec((n, VD), lambda i, pt: (0, 0)), pl.BlockSpec((n, LANES), lambda i, pt: (0, 0)),
                  pl.BlockSpec(memory_space=pl.ANY), pl.BlockSpec(memory_space=pl.ANY),
                  pl.BlockSpec(memory_space=pl.ANY)],
        out_specs=pl.BlockSpec((1, 1, VD), lambda i, pt: (i, 0, 0)),
        scratch_shapes=[pltpu.VMEM((2, past, VD), F32), pltpu.VMEM((2, past, VD), F32),
                        pltpu.VMEM((2, SUBLANES, past), F32), pltpu.SemaphoreType.DMA((2, 3))],
    )
    return pl.pallas_call(
        functools.partial(_fox_decode_body, n_pages=n_pages),
        grid_spec=grid_spec,
        out_shape=jax.ShapeDtypeStruct((n, 1, VD), F32),
        compiler_params=_cparams("arbitrary"),
        name="fox_decode",
    )(page_table, q, kn, vn, lfn, kc, vc, lfc)


def _pad_cols(w, n):
    return jnp.pad(w, ((0, 0), (0, n - w.shape[1])))


def _row_tile(m):
    for tm in (512, 256, 128, 64, 32, 16, 8):
        if m % tm == 0:
            return tm
    raise ValueError(f"row count {m} is not a multiple of {SUBLANES}")


def kernel(x_prompt, x_sample, state_conv_a, state_gla, state_conv_c, cache_k, cache_v, cache_logf, page_table, ffn1_pre_g, ffn1_post_g, ffn1_w_gate, ffn1_w_up, ffn1_w_down, mix_pre_g, mix_post_g, w_in, w_out, a_conv_w, a_conv_b, a_ln_g, a_ln_b, b_gate_w2, b_gate_b, b_out_norm_g, c_conv_w, d_forget_b, ffn2_pre_g, ffn2_post_g, ffn2_w_gate, ffn2_w_up, ffn2_w_down):
    depth = w_in.shape[0]
    bp, seq, d = x_prompt.shape
    bd = x_sample.shape[0]
    n_pool = cache_k.shape[1]
    assert x_sample.shape[1] == 1 and d == D_MODEL
    assert seq % CUM_ROWS == 0 and seq % GLA_CHUNK == 0 and seq >= CONV_A_W - 1

    hp = x_prompt.reshape(bp * seq, d)
    hs = x_sample.reshape(bd, d)
    tmp = _row_tile(bp * seq)
    tms = _row_tile(bd)
    tq = CUM_ROWS

    off_b = N_PA
    off_c = off_b + 2 * KD + 2 * VD + GLA_RANK
    off_d = off_c + N_PC
    row = lambda v: v.reshape(1, -1)

    lf_t = jnp.pad(cache_logf.transpose(0, 1, 3, 2), ((0, 0), (0, 0), (0, SUBLANES - N_HEADS), (0, 0)))
    lf_rows = depth * n_pool * SUBLANES
    lf_pages = _logf_pages_call(lf_t.reshape(lf_rows, PAGE_SIZE), _row_tile(lf_rows))
    lf_pages = lf_pages.reshape(depth, n_pool, SUBLANES, PAGE_SIZE)

    outs = [[] for _ in range(12)]
    for l in range(depth):
        wg1, wu1, wd1 = (w[l].astype(BF16) for w in (ffn1_w_gate, ffn1_w_up, ffn1_w_down))
        wg2, wu2, wd2 = (w[l].astype(BF16) for w in (ffn2_w_gate, ffn2_w_up, ffn2_w_down))
        wi = w_in[l]
        wa = wi[:, 0:off_b].astype(BF16)
        wb = _pad_cols(wi[:, off_b:off_c], N_PB).astype(BF16)
        wc = wi[:, off_c:off_d].astype(BF16)
        wd = _pad_cols(wi[:, off_d:], N_PD).astype(BF16)
        wo = w_out[l].astype(BF16)
        bf = _pad_cols(row(d_forget_b[l]), LANES)
        wa2 = jnp.pad(b_gate_w2[l], ((0, LANES - GLA_RANK), (0, 0))).astype(BF16)
        mix_w = (a_conv_w[l], row(a_conv_b[l]), row(a_ln_g[l]), row(a_ln_b[l]))
        gla_w = (wa2, row(b_gate_b[l]), row(b_out_norm_g[l]))
        ffn1_w = (row(ffn1_pre_g[l]), row(ffn1_post_g[l]), wg1, wu1, wd1)
        ffn2_w = (row(ffn2_pre_g[l]), row(ffn2_post_g[l]), wg2, wu2, wd2)

        hp = _ffn_call(hp, *ffn1_w, tm=tmp)
        pa, pb, pc, q, k, v, lf = _inproj_call(hp, row(mix_pre_g[l]), wa, wb, wc, wd, bf, tm=tmp)
        r3 = lambda t: t.reshape(bp, seq, t.shape[-1])
        ya, yc, buf_a, buf_c = _conv_prompt_call(r3(pa), r3(pc), *mix_w, c_conv_w[l])
        yb, s_b = _gla_prompt_call(r3(pb), *gla_w)
        ccol, crow = _fox_cumsum_call(r3(lf))
        yd = _fox_prompt_call(r3(q), r3(k), r3(v), ccol, crow, tq)
        f2 = lambda t: t.reshape(bp * seq, t.shape[-1])
        hp = _mix_ffn_call(hp, f2(ya), f2(yb), f2(yc), f2(yd), wo, row(mix_post_g[l]), *ffn2_w, tm=tmp)
        for i, t in enumerate((buf_a, s_b, buf_c,
                               k.reshape(bp, seq, N_HEADS, HEAD_DIM), v.reshape(bp, seq, N_HEADS, HEAD_DIM),
                               lf[:, 0:N_HEADS].reshape(bp, seq, N_HEADS))):
            outs[i].append(t)

        hs = _ffn_call(hs, *ffn1_w, tm=tms)
        pa, pb, pc, q, k, v, lf = _inproj_call(hs, row(mix_pre_g[l]), wa, wb, wc, wd, bf, tm=tms)
        ya, yb, yc, buf_a, s_b, buf_c = _mix_sample_call(
            pa, pb, pc, state_conv_a[l].reshape(bd, -1), state_gla[l].reshape(bd, -1),
            state_conv_c[l].reshape(bd, -1), *mix_w, *gla_w, c_conv_w[l])
        yd = _fox_decode_call(page_table, q, k, v, lf,
                              cache_k[l].reshape(n_pool, PAGE_SIZE, VD),
                              cache_v[l].reshape(n_pool, PAGE_SIZE, VD), lf_pages[l])
        hs = _mix_ffn_call(hs, ya, yb, yc, yd.reshape(bd, VD), wo, row(mix_post_g[l]), *ffn2_w, tm=tms)
        for i, t in enumerate((buf_a.reshape(bd, CONV_A_W - 1, GROUP_W),
                               s_b.reshape(bd, N_HEADS, DK_B, HEAD_DIM),
                               buf_c.reshape(bd, CONV_C_W - 1, GROUP_W),
                               k.reshape(bd, 1, N_HEADS, HEAD_DIM), v.reshape(bd, 1, N_HEADS, HEAD_DIM),
                               lf[:, 0:N_HEADS].reshape(bd, 1, N_HEADS))):
            outs[6 + i].append(t)

    return (hp.reshape(bp, seq, d), hs.reshape(bd, 1, d)) + tuple(jnp.stack(o) for o in outs)
```

```python
import functools

import jax
import jax.numpy as jnp
from jax import lax
from jax.experimental import pallas as pl
from jax.experimental.pallas import tpu as pltpu

F32 = jnp.float32
BF16 = jnp.bfloat16

D_MODEL = 1024
GROUP_W = D_MODEL // 4
HEAD_DIM = 64
N_HEADS = GROUP_W // HEAD_DIM
DK_B = HEAD_DIM // 2
GLA_RANK = 16
GLA_TAU = 16.0
GLA_CHUNK = 64
CONV_A_W = 31
CONV_C_W = 3
PAGE_SIZE = 128
EPS = 1e-6
NEG_INF = -1e30

LANES = 128
SUBLANES = 8
VMEM_LIMIT_BYTES = 56 * 1024 * 1024

N_PA = 2 * GROUP_W
KD = N_HEADS * DK_B
VD = N_HEADS * HEAD_DIM
N_PB = 2 * KD + 2 * VD + LANES
N_PC = 3 * GROUP_W


def _cparams(*sem):
    return pltpu.CompilerParams(dimension_semantics=sem, vmem_limit_bytes=VMEM_LIMIT_BYTES)


def _dot(a, b):
    return jnp.dot(a, b, preferred_element_type=F32)


def _dot_nt(a, b):
    return lax.dot_general(a, b, (((1,), (1,)), ((), ())), preferred_element_type=F32)


def _split3(x):
    hi = x.astype(BF16)
    r = x - hi.astype(F32)
    mid = r.astype(BF16)
    lo = (r - mid.astype(F32)).astype(BF16)
    return hi, mid, lo


def _dot_sel(x, sel):
    hi, mid, lo = _split3(x)
    return _dot(hi, sel) + _dot(mid, sel) + _dot(lo, sel)


def _sel_dot(sel, x):
    hi, mid, lo = _split3(x)
    return _dot(sel, hi) + _dot(sel, mid) + _dot(sel, lo)


def _rms(x, g):
    return x * lax.rsqrt(jnp.mean(x * x, axis=-1, keepdims=True) + EPS) * g


def _sigmoid(x):
    return 1.0 / (1.0 + jnp.exp(-x))


def _silu(x):
    return x * _sigmoid(x)


def _log_sigmoid(x):
    return jnp.minimum(x, 0.0) - jnp.log1p(jnp.exp(-jnp.abs(x)))


def _iota(shape, dim):
    return lax.broadcasted_iota(jnp.int32, shape, dim)


def _idiv(x, n):
    assert n & (n - 1) == 0
    return lax.shift_right_logical(x, n.bit_length() - 1)


def _imod(x, n):
    assert n & (n - 1) == 0
    return x & (n - 1)


def _head_mean_matrix():
    r = _idiv(_iota((VD, VD), 0), HEAD_DIM)
    c = _idiv(_iota((VD, VD), 1), HEAD_DIM)
    return (r == c).astype(BF16)


def _swiglu_residual(x, pre_g, post_g, wg_ref, wu_ref, wd_ref, acc_ref, fc):
    xn = _rms(x, pre_g).astype(BF16)
    for c in range(wg_ref.shape[1] // fc):
        sl = slice(c * fc, (c + 1) * fc)
        g = _dot(xn, wg_ref[:, sl])
        u = _dot(xn, wu_ref[:, sl])
        hid = (_silu(g) * u).astype(BF16)
        part = _dot(hid, wd_ref[sl, :])
        if c == 0:
            acc_ref[...] = part
        else:
            acc_ref[...] += part
    return x + 0.5 * _rms(acc_ref[...], post_g)


def _ffn_body(x_ref, pre_ref, post_ref, wg_ref, wu_ref, wd_ref, o_ref, acc_ref, *, fc):
    o_ref[...] = _swiglu_residual(x_ref[...], pre_ref[...], post_ref[...],
                                  wg_ref, wu_ref, wd_ref, acc_ref, fc)


def _mix_ffn_body(h_ref, ya_ref, yb_ref, yc_ref, yd_ref, wo_ref, mpost_ref,
                  pre_ref, post_ref, wg_ref, wu_ref, wd_ref, o_ref, acc_ref, *, fc):
    ycat = jnp.concatenate([ya_ref[...], yb_ref[...], yc_ref[...], yd_ref[...]], axis=1)
    y = _dot(ycat.astype(BF16), wo_ref[...])
    h = h_ref[...] + _rms(y, mpost_ref[...])
    o_ref[...] = _swiglu_residual(h, pre_ref[...], post_ref[...],
                                  wg_ref, wu_ref, wd_ref, acc_ref, fc)


def _row_spec(tm, n):
    return pl.BlockSpec((tm, n), lambda i: (i, 0))


def _full_spec(shape):
    return pl.BlockSpec(shape, lambda i: (0,) * len(shape), pipeline_mode=pl.Buffered(1))


def _ffn_call(x, pre_g, post_g, wg, wu, wd, tm, fc=256):
    m, d = x.shape
    f = wg.shape[1]
    return pl.pallas_call(
        functools.partial(_ffn_body, fc=fc),
        grid=(m // tm,),
        in_specs=[_row_spec(tm, d), _full_spec((1, d)), _full_spec((1, d)),
                  _full_spec((d, f)), _full_spec((d, f)), _full_spec((f, d))],
        out_specs=_row_spec(tm, d),
        out_shape=jax.ShapeDtypeStruct((m, d), F32),
        scratch_shapes=[pltpu.VMEM((tm, d), F32)],
        compiler_params=_cparams("arbitrary"),
        name="ffn",
    )(x, pre_g, post_g, wg, wu, wd)


def _mix_ffn_call(h, ya, yb, yc, yd, wo, mpost, pre_g, post_g, wg, wu, wd, tm, fc=256):
    m, d = h.shape
    f = wg.shape[1]
    return pl.pallas_call(
        functools.partial(_mix_ffn_body, fc=fc),
        grid=(m // tm,),
        in_specs=[_row_spec(tm, d)] + [_row_spec(tm, GROUP_W)] * 4
                 + [_full_spec((d, d)), _full_spec((1, d)), _full_spec((1, d)), _full_spec((1, d)),
                    _full_spec((d, f)), _full_spec((d, f)), _full_spec((f, d))],
        out_specs=_row_spec(tm, d),
        out_shape=jax.ShapeDtypeStruct((m, d), F32),
        scratch_shapes=[pltpu.VMEM((tm, d), F32)],
        compiler_params=_cparams("arbitrary"),
        name="mix_ffn",
    )(h, ya, yb, yc, yd, wo, mpost, pre_g, post_g, wg, wu, wd)


EXT = LANES
C_KEY = HEAD_DIM
C_QRY = HEAD_DIM + 3


def _inproj_prompt_body(h_ref, g_ref, wa_ref, wb_ref, wc_ref, wqe_ref, wke_ref, wk_ref, wv_ref, wl_ref,
                        bfc_ref, pa_ref, pb_ref, pc_ref, kt_ref, vt_ref, lft_ref,
                        qe_ref, ke_ref, vtb_ref, carry_ref, *, per):
    tm = h_ref.shape[0]
    u = _rms(h_ref[...], g_ref[...]).astype(BF16)
    pa_ref[...] = _dot_nt(u, wa_ref[...])
    pb_ref[...] = _dot_nt(u, wb_ref[...])
    pc_ref[...] = _dot_nt(u, wc_ref[...])
    kt_ref[0] = _dot_nt(wk_ref[...], u)
    vt = _dot_nt(wv_ref[...], u)
    vt_ref[0] = vt
    vtb_ref[0] = vt.astype(BF16)
    lft = _log_sigmoid(_dot_nt(wl_ref[0:SUBLANES, :], u) + bfc_ref[:, 0:1])
    lft_ref[0] = lft

    first = pl.program_id(0) % per == 0
    carry = jnp.where(first, 0.0, carry_ref[:, 0:1])
    upto = (_iota((tm, tm), 0) <= _iota((tm, tm), 1)).astype(BF16)
    lfm = jnp.where(_iota((SUBLANES, tm), 0) < N_HEADS, lft, 0.0)
    c = _dot_sel(lfm, upto) + carry
    carry_ref[...] = jnp.broadcast_to(c[:, tm - 1:tm], carry_ref.shape)
    ccol = jnp.concatenate([c, jnp.zeros((LANES - SUBLANES, tm), F32)], axis=0).T

    hi, mid, lo = _split3(ccol)
    pieces = (hi.astype(F32) + pltpu.roll(mid.astype(F32), N_HEADS, 1)
              + pltpu.roll(lo.astype(F32), 2 * N_HEADS, 1)).astype(BF16)
    r = _iota((LANES, N_HEADS * EXT), 0)
    col = _iota((LANES, N_HEADS * EXT), 1)
    j = _imod(col, EXT) - C_KEY
    place_k = ((j >= 0) & (j < 3) & (r == N_HEADS * j + _idiv(col, EXT))).astype(BF16)
    lane = _imod(_iota((1, N_HEADS * EXT), 1), EXT)
    ones_k = ((lane >= C_QRY) & (lane < C_QRY + 3)).astype(F32)
    ke = _dot_nt(u, wke_ref[...]) + _dot(pieces, place_k) + ones_k
    for h in range(N_HEADS):
        ke_ref[0, h] = ke[:, h * EXT:(h + 1) * EXT].astype(BF16)

    hi, mid, lo = _split3(c)
    pieces_t = jnp.concatenate([hi.astype(F32), mid.astype(F32), lo.astype(F32),
                                jnp.zeros((LANES - 3 * SUBLANES, tm), F32)], axis=0).astype(BF16)
    r = _iota((N_HEADS * EXT, LANES), 0)
    col = _iota((N_HEADS * EXT, LANES), 1)
    j = _imod(r, EXT) - C_QRY
    place_q = ((j >= 0) & (j < 3) & (col == SUBLANES * j + _idiv(r, EXT))).astype(BF16)
    rowi = _imod(_iota((N_HEADS * EXT, 1), 0), EXT)
    neg_q = jnp.where((rowi >= C_KEY) & (rowi < C_KEY + 3), -1.0, 0.0)
    qe = _dot_nt(wqe_ref[...], u) * (HEAD_DIM ** -0.5) + _dot(place_q, pieces_t) + neg_q
    for h in range(N_HEADS):
        qe_ref[0, h] = qe[h * EXT:(h + 1) * EXT, :].astype(BF16)


def _inproj_sample_body(h_ref, g_ref, wa_ref, wb_ref, wc_ref, wq_ref, wk_ref, wv_ref, wl_ref,
                        bfc_ref, bfr_ref,
                        pa_ref, pbt_ref, pc_ref, q_ref, k_ref, v_ref, lf_ref, kt_ref, vt_ref, lft_ref):
    u = _rms(h_ref[...], g_ref[...]).astype(BF16)
    pa_ref[...] = _dot_nt(u, wa_ref[...])
    pbt_ref[...] = _dot_nt(wb_ref[...], u)
    pc_ref[...] = _dot_nt(u, wc_ref[...])
    q_ref[...] = _dot_nt(u, wq_ref[...])
    k_ref[...] = _dot_nt(u, wk_ref[...])
    v_ref[...] = _dot_nt(u, wv_ref[...])
    lf_ref[...] = _log_sigmoid(_dot_nt(u, wl_ref[...]) + bfr_ref[...])
    kt_ref[...] = _dot_nt(wk_ref[...], u)
    vt_ref[...] = _dot_nt(wv_ref[...], u)
    lft_ref[...] = _log_sigmoid(_dot_nt(wl_ref[0:SUBLANES, :], u) + bfc_ref[:, 0:1])


def _inproj_weight_specs(d):
    return [_full_spec((n, d)) for n in (N_PA, N_PB, N_PC, VD, VD, VD, LANES)]


def _inproj_prompt_call(h, g, ws, bfc, bp, seq, tm):
    m, d = h.shape
    per = seq // tm
    tspec = lambda n: pl.BlockSpec((1, n, tm), lambda i: (i // per, 0, i % per))
    hx = N_HEADS * EXT
    return pl.pallas_call(
        functools.partial(_inproj_prompt_body, per=per),
        grid=(m // tm,),
        in_specs=[_row_spec(tm, d), _full_spec((1, d))]
                 + [_full_spec((n, d)) for n in (N_PA, N_PB, N_PC, hx, hx, VD, VD, LANES)]
                 + [_full_spec((SUBLANES, LANES))],
        out_specs=[_row_spec(tm, N_PA), _row_spec(tm, N_PB), _row_spec(tm, N_PC),
                   tspec(VD), tspec(VD), tspec(SUBLANES),
                   pl.BlockSpec((1, N_HEADS, EXT, tm), lambda i: (i // per, 0, 0, i % per)),
                   pl.BlockSpec((1, N_HEADS, tm, EXT), lambda i: (i // per, 0, i % per, 0)),
                   tspec(VD)],
        out_shape=[jax.ShapeDtypeStruct((m, N_PA), F32), jax.ShapeDtypeStruct((m, N_PB), F32),
                   jax.ShapeDtypeStruct((m, N_PC), F32),
                   jax.ShapeDtypeStruct((bp, VD, seq), F32), jax.ShapeDtypeStruct((bp, VD, seq), F32),
                   jax.ShapeDtypeStruct((bp, SUBLANES, seq), F32),
                   jax.ShapeDtypeStruct((bp, N_HEADS, EXT, seq), BF16),
                   jax.ShapeDtypeStruct((bp, N_HEADS, seq, EXT), BF16),
                   jax.ShapeDtypeStruct((bp, VD, seq), BF16)],
        scratch_shapes=[pltpu.VMEM((SUBLANES, LANES), F32)],
        compiler_params=_cparams("arbitrary"),
        name="inproj_prompt",
    )(h, g, *ws, bfc)


def _inproj_sample_call(h, g, ws, bfc, bfr):
    n, d = h.shape
    shapes = [(n, N_PA), (N_PB, n), (n, N_PC), (n, VD), (n, VD), (n, VD), (n, LANES),
              (VD, n), (VD, n), (SUBLANES, n)]
    return pl.pallas_call(
        _inproj_sample_body,
        grid=(1,),
        in_specs=[_full_spec((n, d)), _full_spec((1, d))] + _inproj_weight_specs(d)
                 + [_full_spec((SUBLANES, LANES)), _full_spec((1, LANES))],
        out_specs=[pl.BlockSpec(s, lambda i: (0, 0)) for s in shapes],
        out_shape=[jax.ShapeDtypeStruct(s, F32) for s in shapes],
        compiler_params=_cparams("arbitrary"),
        name="inproj_sample",
    )(h, g, *ws, bfc, bfr)


A_PAD = 32
C_PAD = 8
CONV_ROWS = 128


def _conv_prompt_body(pa_ref, pc_ref, aw_ref, ab_ref, lg_ref, lb_ref, cw_ref,
                      ya_ref, yc_ref, bufa_ref, bufc_ref, apad_ref, zpad_ref):
    seq = pa_ref.shape[1]
    c = GROUP_W
    apad_ref[0:A_PAD, :] = jnp.zeros((A_PAD, c), F32)
    apad_ref[A_PAD:, :] = pa_ref[0, :, 0:c] * _sigmoid(pa_ref[0, :, c:])
    zpad_ref[0:C_PAD, :] = jnp.zeros((C_PAD, c), F32)
    zpad_ref[C_PAD:, :] = pc_ref[0, :, c:2 * c] * pc_ref[0, :, 2 * c:]

    def step(i, carry):
        r0 = pl.multiple_of(i * CONV_ROWS, CONV_ROWS)
        win = apad_ref[pl.ds(r0, CONV_ROWS + A_PAD), :]
        acc = jnp.zeros((CONV_ROWS, c), F32) + ab_ref[...]
        for r in range(SUBLANES):
            nrow = CONV_ROWS if r == 0 else CONV_ROWS + SUBLANES
            u = None
            for a8 in range(0, A_PAD + 1, SUBLANES):
                j = a8 + r - (A_PAD - (CONV_A_W - 1))
                if 0 <= j < CONV_A_W:
                    term = aw_ref[j:j + 1, :] * win[a8:a8 + nrow, :]
                    u = term if u is None else u + term
            acc = acc + u[r:r + CONV_ROWS, :]
        mu = jnp.mean(acc, axis=-1, keepdims=True)
        var = jnp.mean(jnp.square(acc - mu), axis=-1, keepdims=True)
        yn = (acc - mu) * lax.rsqrt(var + EPS) * lg_ref[...] + lb_ref[...]
        ya_ref[0, pl.ds(r0, CONV_ROWS), :] = _silu(yn)
        zwin = zpad_ref[pl.ds(r0, CONV_ROWS + C_PAD), :]
        accc = jnp.zeros((CONV_ROWS, c), F32)
        for j in range(CONV_C_W):
            off = C_PAD - (CONV_C_W - 1) + j
            accc = accc + cw_ref[j:j + 1, :] * zwin[off:off + CONV_ROWS, :]
        yc_ref[0, pl.ds(r0, CONV_ROWS), :] = pc_ref[0, pl.ds(r0, CONV_ROWS), 0:c] * accc
        return carry

    lax.fori_loop(0, seq // CONV_ROWS, step, 0)
    na = CONV_A_W - 1
    nc = CONV_C_W - 1
    bufa_ref[0] = apad_ref[seq:A_PAD + seq, :][A_PAD - na:, :]
    bufc_ref[0] = zpad_ref[seq:C_PAD + seq, :][C_PAD - nc:, :]


def _conv_prompt_call(pa, pc, aw, ab, lg, lb, cw):
    b, seq, _ = pa.shape
    c = GROUP_W
    bspec = lambda n: pl.BlockSpec((1, seq, n), lambda i: (i, 0, 0))
    return pl.pallas_call(
        _conv_prompt_body,
        grid=(b,),
        in_specs=[bspec(N_PA), bspec(N_PC), _full_spec((CONV_A_W, c)), _full_spec((1, c)),
                  _full_spec((1, c)), _full_spec((1, c)), _full_spec((CONV_C_W, c))],
        out_specs=[bspec(c), bspec(c),
                   pl.BlockSpec((1, CONV_A_W - 1, c), lambda i: (i, 0, 0)),
                   pl.BlockSpec((1, CONV_C_W - 1, c), lambda i: (i, 0, 0))],
        out_shape=[jax.ShapeDtypeStruct((b, seq, c), F32), jax.ShapeDtypeStruct((b, seq, c), F32),
                   jax.ShapeDtypeStruct((b, CONV_A_W - 1, c), F32),
                   jax.ShapeDtypeStruct((b, CONV_C_W - 1, c), F32)],
        scratch_shapes=[pltpu.VMEM((A_PAD + seq, c), F32), pltpu.VMEM((C_PAD + seq, c), F32)],
        compiler_params=_cparams("arbitrary"),
        name="conv_prompt",
    )(pa, pc, aw, ab, lg, lb, cw)


def _gla_gate(alr, wa2_ref, ba_ref):
    return _log_sigmoid(_dot(alr.astype(BF16), wa2_ref[...]) + ba_ref[...]) * (1.0 / GLA_TAU)


def _gla_out(o, g, ng, hm):
    ms = _dot_sel(o * o, hm) * (1.0 / HEAD_DIM)
    return o * lax.rsqrt(ms + EPS) * ng * _silu(g)


def _gla_prompt_body(pb_ref, wa2_ref, ba_ref, ng_ref, y_ref, st_ref, la_ref, s_ref):
    seq = pb_ref.shape[1]
    ck = GLA_CHUNK
    la_ref[...] = _gla_gate(pb_ref[0, :, 2 * KD + 2 * VD:], wa2_ref, ba_ref)
    s_ref[...] = jnp.zeros((KD, VD), F32)

    tri = (_iota((ck, ck), 1) <= _iota((ck, ck), 0)).astype(BF16)
    eye = (_iota((KD, KD), 0) == _iota((KD, KD), 1)).astype(F32)
    blk = _idiv(_iota((KD, VD), 0), DK_B) == _idiv(_iota((KD, VD), 1), HEAD_DIM)
    blk_bf = blk.astype(BF16)
    blk_f = blk.astype(F32)
    hm = _head_mean_matrix()
    scale = DK_B ** -0.5

    def chunk(ci, carry):
        r0 = pl.multiple_of(ci * ck, ck)
        rows = pl.ds(r0, ck)
        q = pb_ref[0, rows, 0:KD] * scale
        k = pb_ref[0, rows, KD:2 * KD]
        v = pb_ref[0, rows, 2 * KD:2 * KD + VD]
        g = pb_ref[0, rows, 2 * KD + VD:2 * KD + 2 * VD]
        b = _sel_dot(tri, la_ref[rows, :])
        b_last = b[ck - 1:ck, :]
        s_old = s_ref[...]
        o = _dot((q * jnp.exp(b)).astype(BF16), s_old.astype(BF16))
        parts = []
        for g8 in range(ck // SUBLANES):
            t0 = g8 * SUBLANES
            nt = ck - t0
            bt = b[t0:, :]
            qt = q[t0:, :]
            tpos = _iota((nt, KD), 0) + t0
            es = []
            for s in range(t0, t0 + SUBLANES):
                diff = jnp.where(tpos >= s, bt - b[s:s + 1, :], -jnp.inf)
                es.append((qt * k[s:s + 1, :] * jnp.exp(diff)).astype(BF16))
            r = _dot(jnp.concatenate(es, axis=0), blk_bf)
            acc = jnp.zeros((nt, VD), F32)
            for si in range(SUBLANES):
                acc = acc + r[si * nt:(si + 1) * nt, :] * v[t0 + si:t0 + si + 1, :]
            parts.append(acc)
        rows_out = []
        for t8 in range(ck // SUBLANES):
            tot = None
            for g8 in range(t8 + 1):
                off = (t8 - g8) * SUBLANES
                piece = parts[g8][off:off + SUBLANES, :]
                tot = piece if tot is None else tot + piece
            rows_out.append(tot)
        o = o + jnp.concatenate(rows_out, axis=0)
        y_ref[0, rows, :] = _gla_out(o, g, ng_ref[...], hm)
        k_dec = k * jnp.exp(b_last - b)
        upd = _dot(k_dec.T.astype(BF16), v.astype(BF16))
        decay_col = jnp.exp(jnp.sum(eye * b_last, axis=1, keepdims=True))
        s_ref[...] = decay_col * s_old + upd * blk_f
        return carry

    lax.fori_loop(0, seq // ck, chunk, 0)
    for h in range(N_HEADS):
        st_ref[0, h] = s_ref[h * DK_B:(h + 1) * DK_B, h * HEAD_DIM:(h + 1) * HEAD_DIM]


def _gla_prompt_call(pb, wa2, ba, ng):
    b, seq, _ = pb.shape
    return pl.pallas_call(
        _gla_prompt_body,
        grid=(b,),
        in_specs=[pl.BlockSpec((1, seq, N_PB), lambda i: (i, 0, 0)), _full_spec((LANES, KD)),
                  _full_spec((1, KD)), _full_spec((1, VD))],
        out_specs=[pl.BlockSpec((1, seq, VD), lambda i: (i, 0, 0)),
                   pl.BlockSpec((1, N_HEADS, DK_B, HEAD_DIM), lambda i: (i, 0, 0, 0))],
        out_shape=[jax.ShapeDtypeStruct((b, seq, VD), F32),
                   jax.ShapeDtypeStruct((b, N_HEADS, DK_B, HEAD_DIM), F32)],
        scratch_shapes=[pltpu.VMEM((seq, KD), F32), pltpu.VMEM((KD, VD), F32)],
        compiler_params=_cparams("arbitrary"),
        name="gla_prompt",
    )(pb, wa2, ba, ng)


def _fox_prompt_body(qe_ref, ke_ref, vtb_ref, o_ref, *, tq):
    qi = pl.program_id(1)
    key_pos = _iota((tq, tq), 0)
    qry_pos = _iota((tq, tq), 1)

    def step(ki, carry, masked):
        ks = pl.ds(pl.multiple_of(ki * tq, tq), tq)
        scores = [_dot(ke_ref[0, h, ks, :], qe_ref[0, h]) for h in range(N_HEADS)]
        soft = []
        for h in range(N_HEADS):
            m, l, _ = carry[h]
            s = scores[h]
            if masked:
                s = jnp.where(key_pos <= qry_pos, s, NEG_INF)
            m_new = jnp.maximum(m, jnp.max(s, axis=0, keepdims=True))
            alpha = jnp.exp(m - m_new)
            p = jnp.exp(s - m_new)
            soft.append((m_new, alpha * l + jnp.sum(p, axis=0, keepdims=True), alpha, p.astype(BF16)))
        out = []
        for h in range(N_HEADS):
            m_new, l, alpha, p = soft[h]
            vh = vtb_ref[0, h * HEAD_DIM:(h + 1) * HEAD_DIM, ks]
            out.append((m_new, l, alpha * carry[h][2] + _dot(vh, p)))
        return tuple(out)

    init = tuple((jnp.full((1, tq), NEG_INF, F32), jnp.zeros((1, tq), F32),
                  jnp.zeros((HEAD_DIM, tq), F32)) for _ in range(N_HEADS))
    carry = lax.fori_loop(0, qi, functools.partial(step, masked=False), init)
    carry = step(qi, carry, True)
    o_ref[0] = jnp.concatenate([acc / l for _, l, acc in carry], axis=0).T


def _fox_prompt_call(qe, ke, vtb, tq):
    b, _, _, seq = qe.shape
    return pl.pallas_call(
        functools.partial(_fox_prompt_body, tq=tq),
        grid=(b, seq // tq),
        in_specs=[pl.BlockSpec((1, N_HEADS, EXT, tq), lambda i, j: (i, 0, 0, j)),
                  pl.BlockSpec((1, N_HEADS, seq, EXT), lambda i, j: (i, 0, 0, 0)),
                  pl.BlockSpec((1, VD, seq), lambda i, j: (i, 0, 0))],
        out_specs=pl.BlockSpec((1, tq, VD), lambda i, j: (i, j, 0)),
        out_shape=jax.ShapeDtypeStruct((b, seq, VD), F32),
        compiler_params=_cparams("arbitrary", "arbitrary"),
        name="fox_prompt",
    )(qe, ke, vtb)


def _mix_sample_body(pa_ref, pbt_ref, pc_ref, bufa_ref, sg_ref, bufc_ref,
                     aw_ref, ab_ref, lg_ref, lb_ref, wa2t_ref, bac_ref, ngc_ref, cw_ref,
                     ya_ref, yb_ref, yc_ref, bufa_o, sg_o, bufc_o, q_s, k_s, dec_s):
    c = GROUP_W
    a = pa_ref[:, 0:c] * _sigmoid(pa_ref[:, c:])
    na = CONV_A_W - 1
    acc = aw_ref[na:na + 1, :] * a + ab_ref[...]
    for j in range(na):
        acc = acc + aw_ref[j:j + 1, :] * bufa_ref[0, j]
    mu = jnp.mean(acc, axis=-1, keepdims=True)
    var = jnp.mean(jnp.square(acc - mu), axis=-1, keepdims=True)
    ya_ref[...] = _silu((acc - mu) * lax.rsqrt(var + EPS) * lg_ref[...] + lb_ref[...])
    for j in range(na - 1):
        bufa_o[j] = bufa_ref[0, j + 1]
    bufa_o[na - 1] = a
    z = pc_ref[:, c:2 * c] * pc_ref[:, 2 * c:]
    conv = (cw_ref[0:1, :] * bufc_ref[0, :, 0:c] + cw_ref[1:2, :] * bufc_ref[0, :, c:]
            + cw_ref[2:3, :] * z)
    yc_ref[...] = pc_ref[:, 0:c] * conv
    bufc_o[:, 0:c] = bufc_ref[0, :, c:]
    bufc_o[:, c:] = z
    q_s[...] = pbt_ref[0:KD, :] * (DK_B ** -0.5)
    k_s[...] = pbt_ref[KD:2 * KD, :]
    gate = _dot(wa2t_ref[...], pbt_ref[2 * KD + 2 * VD:, :].astype(BF16)) + bac_ref[...]
    dec_s[...] = jnp.exp(_log_sigmoid(gate) * (1.0 / GLA_TAU))
    n = pa_ref.shape[0]
    ys = []
    for h in range(N_HEADS):
        vrows = slice(2 * KD + h * HEAD_DIM, 2 * KD + (h + 1) * HEAD_DIM)
        grows = slice(2 * KD + VD + h * HEAD_DIM, 2 * KD + VD + (h + 1) * HEAD_DIM)
        vh = pbt_ref[vrows, :]

        def key_step(kk, o, h=h, vh=vh):
            hk = h * DK_B + kk
            one = pl.ds(hk, 1)
            s_new = dec_s[one, :] * sg_ref[0, hk] + k_s[one, :] * vh
            sg_o[hk] = s_new
            return o + q_s[one, :] * s_new

        o = lax.fori_loop(0, DK_B, key_step, jnp.zeros((HEAD_DIM, n), F32))
        ms = jnp.mean(o * o, axis=0, keepdims=True)
        ng = ngc_ref[h * HEAD_DIM:(h + 1) * HEAD_DIM, :]
        ys.append(o * lax.rsqrt(ms + EPS) * ng * _silu(pbt_ref[grows, :]))
    yb_ref[...] = jnp.concatenate(ys, axis=0).T


def _layer_spec(shape, l):
    return pl.BlockSpec((1,) + tuple(shape[1:]), lambda i: (l,) + (0,) * (len(shape) - 1),
                        pipeline_mode=pl.Buffered(1))


def _mix_sample_call(l, pa, pbt, pc, bufa, sg, bufc, aw, ab, lg, lb, wa2t, bac, ngc, cw):
    n = pa.shape[0]
    c = GROUP_W
    small = (aw, ab, lg, lb, wa2t, bac, ngc, cw)
    out_shape = [jax.ShapeDtypeStruct((n, c), F32)] * 3 + [
        jax.ShapeDtypeStruct(bufa.shape[1:], F32), jax.ShapeDtypeStruct(sg.shape[1:], F32),
        jax.ShapeDtypeStruct(bufc.shape[1:], F32)]
    return pl.pallas_call(
        _mix_sample_body,
        grid=(1,),
        in_specs=[_full_spec(pa.shape), _full_spec(pbt.shape), _full_spec(pc.shape),
                  _layer_spec(bufa.shape, l), _layer_spec(sg.shape, l), _layer_spec(bufc.shape, l)]
                 + [_full_spec(x.shape) for x in small],
        out_specs=[pl.BlockSpec(s.shape, lambda i, nd=len(s.shape): (0,) * nd) for s in out_shape],
        out_shape=out_shape,
        scratch_shapes=[pltpu.VMEM((KD, n), F32)] * 3,
        compiler_params=_cparams("arbitrary"),
        name="mix_sample",
    )(pa, pbt, pc, bufa, sg, bufc, *small)


def _logf_pages_body(x_ref, o_ref):
    n = x_ref.shape[0]
    x = x_ref[...]
    later = (_iota((PAGE_SIZE, PAGE_SIZE), 0) > _iota((PAGE_SIZE, PAGE_SIZE), 1)).astype(BF16)
    ones = jnp.ones((PAGE_SIZE, PAGE_SIZE), BF16)
    within = _dot_sel(x, later)
    total = pltpu.roll(_dot_sel(x, ones), N_HEADS, 0)
    o_ref[...] = jnp.where(_imod(_iota((n, PAGE_SIZE), 0), SUBLANES) < N_HEADS, within, total)


def _logf_pages_call(x, tm):
    m = x.shape[0]
    return pl.pallas_call(
        _logf_pages_body,
        grid=(m // tm,),
        in_specs=[_row_spec(tm, PAGE_SIZE)],
        out_specs=_row_spec(tm, PAGE_SIZE),
        out_shape=jax.ShapeDtypeStruct((m, PAGE_SIZE), F32),
        compiler_params=_cparams("arbitrary"),
        name="logf_pages",
    )(x)


def _fox_decode_body(pt_ref, q_ref, kn_ref, vn_ref, lfn_ref, kc_hbm, vc_hbm, lfc_hbm,
                     o_ref, kbuf, vbuf, lfbuf, sem, *, n_pages, layer):
    b = pl.program_id(0)
    nb = pl.num_programs(0)
    slot = b % 2
    past = n_pages * PAGE_SIZE

    def copies(bb, sl):
        out = []
        for p in range(n_pages):
            page = pt_ref[bb, p]
            cols = pl.ds(p * PAGE_SIZE, PAGE_SIZE)
            out.append(pltpu.make_async_copy(kc_hbm.at[layer, page], kbuf.at[sl, :, cols], sem.at[sl, 0]))
            out.append(pltpu.make_async_copy(vc_hbm.at[layer, page], vbuf.at[sl, :, cols], sem.at[sl, 1]))
            out.append(pltpu.make_async_copy(lfc_hbm.at[layer, page], lfbuf.at[sl, :, cols], sem.at[sl, 2]))
        return out

    @pl.when(b == 0)
    def _():
        for cp in copies(0, 0):
            cp.start()

    @pl.when(b + 1 < nb)
    def _():
        for cp in copies(b + 1, 1 - slot):
            cp.start()

    for cp in copies(b, slot):
        cp.wait()

    scale = HEAD_DIM ** -0.5
    hm = (_idiv(_iota((SUBLANES, VD), 1), HEAD_DIM) == _iota((SUBLANES, VD), 0)).astype(F32)
    qbd = jnp.broadcast_to(q_ref[pl.ds(b, 1), :], (SUBLANES, VD)) * hm
    s = _dot(qbd.astype(BF16), kbuf[slot].astype(BF16)) * scale
    lf = lfbuf[slot]
    incl = pltpu.roll(lf, N_HEADS, 0)
    sh = PAGE_SIZE
    while sh < past:
        incl = incl + jnp.concatenate([incl[:, sh:], jnp.zeros((SUBLANES, sh), F32)], axis=1)
        sh *= 2
    later_pages = jnp.concatenate([incl[:, PAGE_SIZE:], jnp.zeros((SUBLANES, PAGE_SIZE), F32)], axis=1)
    valid = _iota((SUBLANES, past), 0) < N_HEADS
    logits = jnp.where(valid, s + lf + later_pages, 0.0)
    s_self = jnp.sum(qbd * kn_ref[pl.ds(b, 1), :], axis=1, keepdims=True) * scale
    pick = (_iota((SUBLANES, LANES), 1) == _iota((SUBLANES, LANES), 0)).astype(F32)
    c_new = jnp.sum(pick * lfn_ref[pl.ds(b, 1), :], axis=1, keepdims=True)
    self_logit = jnp.where(_iota((SUBLANES, 1), 0) < N_HEADS, s_self - c_new, 0.0)
    m = jnp.maximum(jnp.max(logits, axis=1, keepdims=True), self_logit)
    p = jnp.exp(logits - m)
    p_self = jnp.exp(self_logit - m)
    l = jnp.sum(p, axis=1, keepdims=True) + p_self
    pv = _dot_nt(p.astype(BF16), vbuf[slot].astype(BF16))
    o8 = (pv + p_self * vn_ref[pl.ds(b, 1), :]) / l
    o_ref[0] = jnp.sum(o8 * hm, axis=0, keepdims=True)


def _fox_decode_call(layer, page_table, q, kn, vn, lfn, kc, vc, lfc):
    n, n_pages = page_table.shape
    past = n_pages * PAGE_SIZE
    grid_spec = pltpu.PrefetchScalarGridSpec(
        num_scalar_prefetch=1,
        grid=(n,),
        in_specs=[pl.BlockSpec((n, VD), lambda i, pt: (0, 0)), pl.BlockSpec((n, VD), lambda i, pt: (0, 0)),
                  pl.BlockSpec((n, VD), lambda i, pt: (0, 0)), pl.BlockSpec((n, LANES), lambda i, pt: (0, 0)),
                  pl.BlockSpec(memory_space=pl.ANY), pl.BlockSpec(memory_space=pl.ANY),
                  pl.BlockSpec(memory_space=pl.ANY)],
        out_specs=pl.BlockSpec((1, 1, VD), lambda i, pt: (i, 0, 0)),
        scratch_shapes=[pltpu.VMEM((2, VD, past), F32), pltpu.VMEM((2, VD, past), F32),
                        pltpu.VMEM((2, SUBLANES, past), F32), pltpu.SemaphoreType.DMA((2, 3))],
    )
    return pl.pallas_call(
        functools.partial(_fox_decode_body, n_pages=n_pages, layer=layer),
        grid_spec=grid_spec,
        out_shape=jax.ShapeDtypeStruct((n, 1, VD), F32),
        compiler_params=_cparams("arbitrary"),
        name="fox_decode",
    )(page_table, q, kn, vn, lfn, kc, vc, lfc)


ATTN_TILE = 256


def _row_tile(m):
    for tm in (512, 256, 128, 64, 32, 16, 8):
        if m % tm == 0:
            return tm
    raise ValueError(f"row count {m} is not a multiple of {SUBLANES}")


def kernel(x_prompt, x_sample, state_conv_a, state_gla, state_conv_c, cache_k, cache_v, cache_logf, page_table, ffn1_pre_g, ffn1_post_g, ffn1_w_gate, ffn1_w_up, ffn1_w_down, mix_pre_g, mix_post_g, w_in, w_out, a_conv_w, a_conv_b, a_ln_g, a_ln_b, b_gate_w2, b_gate_b, b_out_norm_g, c_conv_w, d_forget_b, ffn2_pre_g, ffn2_post_g, ffn2_w_gate, ffn2_w_up, ffn2_w_down):
    depth = w_in.shape[0]
    bp, seq, d = x_prompt.shape
    bd = x_sample.shape[0]
    n_pool = cache_k.shape[1]
    assert x_sample.shape[1] == 1 and d == D_MODEL
    assert seq % ATTN_TILE == 0 and seq % GLA_CHUNK == 0 and seq >= CONV_A_W - 1

    hp = x_prompt.reshape(bp * seq, d)
    hs = x_sample.reshape(bd, d)
    tmp = _row_tile(bp * seq)
    tms = _row_tile(bd)
    tq = ATTN_TILE

    off_b = N_PA
    off_c = off_b + 2 * KD + 2 * VD + GLA_RANK
    off_d = off_c + N_PC
    row = lambda v: v.reshape(1, -1)

    lanes_of = lambda v, n=LANES: jnp.broadcast_to(v[:, None], (v.shape[0], n))

    ck = cache_k.transpose(0, 1, 3, 4, 2).reshape(depth, n_pool, VD, PAGE_SIZE)
    cv = cache_v.transpose(0, 1, 3, 4, 2).reshape(depth, n_pool, VD, PAGE_SIZE)
    sca = state_conv_a.transpose(0, 2, 1, 3)
    sgl = state_gla.transpose(0, 2, 3, 4, 1).reshape(depth, KD, HEAD_DIM, bd)
    scc = state_conv_c.reshape(depth, bd, (CONV_C_W - 1) * GROUP_W)
    w_in_t = w_in.transpose(0, 2, 1)

    lf_t = jnp.pad(cache_logf.transpose(0, 1, 3, 2), ((0, 0), (0, 0), (0, SUBLANES - N_HEADS), (0, 0)))
    lf_rows = depth * n_pool * SUBLANES
    lf_pages = _logf_pages_call(lf_t.reshape(lf_rows, PAGE_SIZE), _row_tile(lf_rows))
    lf_pages = lf_pages.reshape(depth, n_pool, SUBLANES, PAGE_SIZE)

    outs = [[] for _ in range(12)]
    for l in range(depth):
        wg1, wu1, wd1 = (w[l].astype(BF16) for w in (ffn1_w_gate, ffn1_w_up, ffn1_w_down))
        wg2, wu2, wd2 = (w[l].astype(BF16) for w in (ffn2_w_gate, ffn2_w_up, ffn2_w_down))
        wt = w_in_t[l].astype(BF16)
        pad_rows = lambda w, n: jnp.pad(w, ((0, n - w.shape[0]), (0, 0)))
        wq, wk, wv = (wt[off_d + i * VD:off_d + (i + 1) * VD] for i in range(3))
        in_w = (wt[0:off_b], pad_rows(wt[off_b:off_c], N_PB), wt[off_c:off_d],
                wq, wk, wv, pad_rows(wt[off_d + 3 * VD:], LANES))
        slabs = lambda w: jnp.pad(w.reshape(N_HEADS, HEAD_DIM, d),
                                  ((0, 0), (0, EXT - HEAD_DIM), (0, 0))).reshape(N_HEADS * EXT, d)
        in_w_prompt = in_w[0:3] + (slabs(wq), slabs(wk), wk, wv, in_w[6])
        wo = w_out[l].astype(BF16)
        bf_pad = jnp.pad(d_forget_b[l], (0, LANES - N_HEADS))
        bfr = row(bf_pad)
        bfc = lanes_of(bf_pad[0:SUBLANES])
        wa2 = jnp.pad(b_gate_w2[l], ((0, LANES - GLA_RANK), (0, 0))).astype(BF16)
        mix_w = (a_conv_w[l], row(a_conv_b[l]), row(a_ln_g[l]), row(a_ln_b[l]))
        ffn1_w = (row(ffn1_pre_g[l]), row(ffn1_post_g[l]), wg1, wu1, wd1)
        ffn2_w = (row(ffn2_pre_g[l]), row(ffn2_post_g[l]), wg2, wu2, wd2)

        hp = _ffn_call(hp, *ffn1_w, tm=tmp)
        pa, pb, pc, kt, vt, lft, qe, ke, vtb = _inproj_prompt_call(
            hp, row(mix_pre_g[l]), in_w_prompt, bfc, bp, seq, tm=tmp)
        r3 = lambda t: t.reshape(bp, seq, t.shape[-1])
        ya, yc, buf_a, buf_c = _conv_prompt_call(r3(pa), r3(pc), *mix_w, c_conv_w[l])
        yb, s_b = _gla_prompt_call(r3(pb), wa2, row(b_gate_b[l]), row(b_out_norm_g[l]))
        yd = _fox_prompt_call(qe, ke, vtb, tq)
        f2 = lambda t: t.reshape(bp * seq, t.shape[-1])
        hp = _mix_ffn_call(hp, f2(ya), f2(yb), f2(yc), f2(yd), wo, row(mix_post_g[l]), *ffn2_w, tm=tmp)
        for i, t in enumerate((buf_a, s_b, buf_c, kt, vt, lft[:, 0:N_HEADS, :])):
            outs[i].append(t)

        hs = _ffn_call(hs, *ffn1_w, tm=tms)
        pa, pbt, pc, q, k, v, lf, kt, vt, lft = _inproj_sample_call(hs, row(mix_pre_g[l]), in_w, bfc, bfr)
        ya, yb, yc, buf_a, s_b, buf_c = _mix_sample_call(
            l, pa, pbt, pc, sca, sgl, scc, *mix_w, wa2.T, lanes_of(b_gate_b[l], bd),
            lanes_of(b_out_norm_g[l], bd), c_conv_w[l])
        yd = _fox_decode_call(l, page_table, q, k, v, lf, ck, cv, lf_pages)
        hs = _mix_ffn_call(hs, ya, yb, yc, yd.reshape(bd, VD), wo, row(mix_post_g[l]), *ffn2_w, tm=tms)
        for i, t in enumerate((buf_a, s_b, buf_c, kt, vt, lft[0:N_HEADS, :])):
            outs[6 + i].append(t)

    p_ca, p_gla, p_cc, p_kt, p_vt, p_lft, s_ca, s_gla, s_cc, s_kt, s_vt, s_lft = (jnp.stack(o) for o in outs)
    heads = lambda t: t.reshape(t.shape[:-2] + (N_HEADS, HEAD_DIM, t.shape[-1]))
    return (hp.reshape(bp, seq, d), hs.reshape(bd, 1, d),
            p_ca, p_gla, p_cc,
            heads(p_kt).transpose(0, 1, 4, 2, 3), heads(p_vt).transpose(0, 1, 4, 2, 3),
            p_lft.transpose(0, 1, 3, 2),
            s_ca.transpose(0, 2, 1, 3),
            s_gla.reshape(depth, N_HEADS, DK_B, HEAD_DIM, bd).transpose(0, 4, 1, 2, 3),
            s_cc.reshape(depth, bd, CONV_C_W - 1, GROUP_W),
            heads(s_kt).transpose(0, 3, 1, 2)[:, :, None], heads(s_vt).transpose(0, 3, 1, 2)[:, :, None],
            s_lft.transpose(0, 2, 1)[:, :, None])
```

```python
import functools

import jax
import jax.numpy as jnp
from jax import lax
from jax.experimental import pallas as pl
from jax.experimental.pallas import tpu as pltpu

F32 = jnp.float32
BF16 = jnp.bfloat16

D_MODEL = 1024
GROUP_W = D_MODEL // 4
HEAD_DIM = 64
N_HEADS = GROUP_W // HEAD_DIM
DK_B = HEAD_DIM // 2
GLA_RANK = 16
GLA_TAU = 16.0
GLA_CHUNK = 64
CONV_A_W = 31
CONV_C_W = 3
PAGE_SIZE = 128
EPS = 1e-6
NEG_INF = -1e30

LANES = 128
SUBLANES = 8
VMEM_LIMIT_BYTES = 56 * 1024 * 1024

N_PA = 2 * GROUP_W
KD = N_HEADS * DK_B
VD = N_HEADS * HEAD_DIM
N_PB = 2 * KD + 2 * VD + LANES
N_PC = 3 * GROUP_W


def _cparams(*sem):
    return pltpu.CompilerParams(dimension_semantics=sem, vmem_limit_bytes=VMEM_LIMIT_BYTES)


def _dot(a, b):
    return jnp.dot(a, b, preferred_element_type=F32)


def _dot_nt(a, b):
    return lax.dot_general(a, b, (((1,), (1,)), ((), ())), preferred_element_type=F32)


def _split3(x):
    hi = x.astype(BF16)
    r = x - hi.astype(F32)
    mid = r.astype(BF16)
    lo = (r - mid.astype(F32)).astype(BF16)
    return hi, mid, lo


def _dot_sel(x, sel):
    hi, mid, lo = _split3(x)
    return _dot(hi, sel) + _dot(mid, sel) + _dot(lo, sel)


def _sel_dot(sel, x):
    hi, mid, lo = _split3(x)
    return _dot(sel, hi) + _dot(sel, mid) + _dot(sel, lo)


def _rms(x, g):
    return x * lax.rsqrt(jnp.mean(x * x, axis=-1, keepdims=True) + EPS) * g


def _sigmoid(x):
    return 1.0 / (1.0 + jnp.exp(-x))


def _silu(x):
    return x * _sigmoid(x)


def _log_sigmoid(x):
    return jnp.minimum(x, 0.0) - jnp.log1p(jnp.exp(-jnp.abs(x)))


def _iota(shape, dim):
    return lax.broadcasted_iota(jnp.int32, shape, dim)


def _idiv(x, n):
    assert n & (n - 1) == 0
    return lax.shift_right_logical(x, n.bit_length() - 1)


def _imod(x, n):
    assert n & (n - 1) == 0
    return x & (n - 1)


def _head_mean_matrix():
    r = _idiv(_iota((VD, VD), 0), HEAD_DIM)
    c = _idiv(_iota((VD, VD), 1), HEAD_DIM)
    return (r == c).astype(BF16)


def _swiglu_residual(x, pre_g, post_g, wg_ref, wu_ref, wd_ref, acc_ref, fc):
    xn = _rms(x, pre_g).astype(BF16)
    for c in range(wg_ref.shape[1] // fc):
        sl = slice(c * fc, (c + 1) * fc)
        g = _dot(xn, wg_ref[:, sl])
        u = _dot(xn, wu_ref[:, sl])
        hid = (_silu(g) * u).astype(BF16)
        part = _dot(hid, wd_ref[sl, :])
        if c == 0:
            acc_ref[...] = part
        else:
            acc_ref[...] += part
    return x + 0.5 * _rms(acc_ref[...], post_g)


def _ffn_body(x_ref, pre_ref, post_ref, wg_ref, wu_ref, wd_ref, o_ref, acc_ref, *, fc):
    o_ref[...] = _swiglu_residual(x_ref[...], pre_ref[...], post_ref[...],
                                  wg_ref, wu_ref, wd_ref, acc_ref, fc)


def _mix_ffn_body(h_ref, ya_ref, yb_ref, yc_ref, yd_ref, wo_ref, mpost_ref,
                  pre_ref, post_ref, wg_ref, wu_ref, wd_ref, o_ref, acc_ref, *, fc):
    ycat = jnp.concatenate([ya_ref[...], yb_ref[...], yc_ref[...], yd_ref[...]], axis=1)
    y = _dot(ycat.astype(BF16), wo_ref[...])
    h = h_ref[...] + _rms(y, mpost_ref[...])
    o_ref[...] = _swiglu_residual(h, pre_ref[...], post_ref[...],
                                  wg_ref, wu_ref, wd_ref, acc_ref, fc)


def _row_spec(tm, n):
    return pl.BlockSpec((tm, n), lambda i: (i, 0))


def _full_spec(shape):
    return pl.BlockSpec(shape, lambda i: (0,) * len(shape), pipeline_mode=pl.Buffered(1))


def _ffn_call(x, pre_g, post_g, wg, wu, wd, tm, fc=256):
    m, d = x.shape
    f = wg.shape[1]
    return pl.pallas_call(
        functools.partial(_ffn_body, fc=fc),
        grid=(m // tm,),
        in_specs=[_row_spec(tm, d), _full_spec((1, d)), _full_spec((1, d)),
                  _full_spec((d, f)), _full_spec((d, f)), _full_spec((f, d))],
        out_specs=_row_spec(tm, d),
        out_shape=jax.ShapeDtypeStruct((m, d), F32),
        scratch_shapes=[pltpu.VMEM((tm, d), F32)],
        compiler_params=_cparams("arbitrary"),
        name="ffn",
    )(x, pre_g, post_g, wg, wu, wd)


def _mix_ffn_call(h, ya, yb, yc, yd, wo, mpost, pre_g, post_g, wg, wu, wd, tm, fc=256):
    m, d = h.shape
    f = wg.shape[1]
    return pl.pallas_call(
        functools.partial(_mix_ffn_body, fc=fc),
        grid=(m // tm,),
        in_specs=[_row_spec(tm, d)] + [_row_spec(tm, GROUP_W)] * 4
                 + [_full_spec((d, d)), _full_spec((1, d)), _full_spec((1, d)), _full_spec((1, d)),
                    _full_spec((d, f)), _full_spec((d, f)), _full_spec((f, d))],
        out_specs=_row_spec(tm, d),
        out_shape=jax.ShapeDtypeStruct((m, d), F32),
        scratch_shapes=[pltpu.VMEM((tm, d), F32)],
        compiler_params=_cparams("arbitrary"),
        name="mix_ffn",
    )(h, ya, yb, yc, yd, wo, mpost, pre_g, post_g, wg, wu, wd)


EXT = LANES
C_KEY = HEAD_DIM
C_QRY = HEAD_DIM + 3


def _inproj_prompt_body(h_ref, g_ref, wa_ref, wb_ref, wc_ref, wq_ref, wke_ref, wk_ref, wv_ref, wl_ref,
                        bfc_ref, pa_ref, pb_ref, pc_ref, kt_ref, vt_ref, lft_ref,
                        qe_ref, ke_ref, vtb_ref, carry_ref, *, per):
    tm = h_ref.shape[0]
    u = _rms(h_ref[...], g_ref[...]).astype(BF16)
    pa_ref[...] = _dot_nt(u, wa_ref[...])
    pb_ref[...] = _dot_nt(u, wb_ref[...])
    pc_ref[...] = _dot_nt(u, wc_ref[...])
    kt_ref[0] = _dot_nt(wk_ref[...], u)
    vt = _dot_nt(wv_ref[...], u)
    vt_ref[0] = vt
    vtb_ref[0] = vt.astype(BF16)
    lft = _log_sigmoid(_dot_nt(wl_ref[0:SUBLANES, :], u) + bfc_ref[:, 0:1])
    lft_ref[0] = lft

    first = pl.program_id(0) % per == 0
    carry = jnp.where(first, 0.0, carry_ref[:, 0:1])
    upto = (_iota((tm, tm), 0) <= _iota((tm, tm), 1)).astype(BF16)
    lfm = jnp.where(_iota((SUBLANES, tm), 0) < N_HEADS, lft, 0.0)
    c = _dot_sel(lfm, upto) + carry
    carry_ref[...] = jnp.broadcast_to(c[:, tm - 1:tm], carry_ref.shape)
    ccol = jnp.concatenate([c, jnp.zeros((LANES - SUBLANES, tm), F32)], axis=0).T

    hi, mid, lo = _split3(ccol)
    pieces = (hi.astype(F32) + pltpu.roll(mid.astype(F32), N_HEADS, 1)
              + pltpu.roll(lo.astype(F32), 2 * N_HEADS, 1)).astype(BF16)
    r = _iota((LANES, N_HEADS * EXT), 0)
    col = _iota((LANES, N_HEADS * EXT), 1)
    j = _imod(col, EXT) - C_KEY
    place_k = ((j >= 0) & (j < 3) & (r == N_HEADS * j + _idiv(col, EXT))).astype(BF16)
    lane = _imod(_iota((1, N_HEADS * EXT), 1), EXT)
    ones_k = ((lane >= C_QRY) & (lane < C_QRY + 3)).astype(F32)
    ke = _dot_nt(u, wke_ref[...]) + _dot(pieces, place_k) + ones_k
    for h in range(N_HEADS):
        ke_ref[0, h] = ke[:, h * EXT:(h + 1) * EXT].astype(BF16)

    hi, mid, lo = _split3(c)
    pieces_t = jnp.concatenate([hi.astype(F32), mid.astype(F32), lo.astype(F32),
                                jnp.zeros((LANES - 3 * SUBLANES, tm), F32)], axis=0).astype(BF16)
    r = _iota((VD, LANES), 0)
    col = _iota((VD, LANES), 1)
    j = _imod(r, HEAD_DIM) - (C_QRY - HEAD_DIM)
    place_q = ((j >= 0) & (j < 3) & (col == SUBLANES * j + _idiv(r, HEAD_DIM))).astype(BF16)
    rowi = _imod(_iota((VD, 1), 0), HEAD_DIM)
    neg_q = jnp.where(rowi < 3, -1.0, 0.0)
    extra = (_dot(place_q, pieces_t) + neg_q).astype(BF16)
    qt = (_dot_nt(wq_ref[...], u) * (HEAD_DIM ** -0.5)).astype(BF16)
    for h in range(N_HEADS):
        rows = slice(h * HEAD_DIM, (h + 1) * HEAD_DIM)
        qe_ref[0, h, 0:HEAD_DIM, :] = qt[rows, :]
        qe_ref[0, h, HEAD_DIM:, :] = extra[rows, :]


def _inproj_sample_body(h_ref, g_ref, wa_ref, wb_ref, wc_ref, wq_ref, wk_ref, wv_ref, wl_ref,
                        bfc_ref, bfr_ref,
                        pa_ref, pbt_ref, pc_ref, q_ref, k_ref, v_ref, lf_ref, kt_ref, vt_ref, lft_ref):
    u = _rms(h_ref[...], g_ref[...]).astype(BF16)
    pa_ref[...] = _dot_nt(u, wa_ref[...])
    pbt_ref[...] = _dot_nt(wb_ref[...], u)
    pc_ref[...] = _dot_nt(u, wc_ref[...])
    q_ref[...] = _dot_nt(u, wq_ref[...])
    k_ref[...] = _dot_nt(u, wk_ref[...])
    v_ref[...] = _dot_nt(u, wv_ref[...])
    lf_ref[...] = _log_sigmoid(_dot_nt(u, wl_ref[...]) + bfr_ref[...])
    kt_ref[...] = _dot_nt(wk_ref[...], u)
    vt_ref[...] = _dot_nt(wv_ref[...], u)
    lft_ref[...] = _log_sigmoid(_dot_nt(wl_ref[0:SUBLANES, :], u) + bfc_ref[:, 0:1])


def _inproj_weight_specs(d):
    return [_full_spec((n, d)) for n in (N_PA, N_PB, N_PC, VD, VD, VD, LANES)]


def _inproj_prompt_call(h, g, ws, bfc, bp, seq, tm):
    m, d = h.shape
    per = seq // tm
    tspec = lambda n: pl.BlockSpec((1, n, tm), lambda i: (i // per, 0, i % per))
    hx = N_HEADS * EXT
    return pl.pallas_call(
        functools.partial(_inproj_prompt_body, per=per),
        grid=(m // tm,),
        in_specs=[_row_spec(tm, d), _full_spec((1, d))]
                 + [_full_spec((n, d)) for n in (N_PA, N_PB, N_PC, VD, hx, VD, VD, LANES)]
                 + [_full_spec((SUBLANES, LANES))],
        out_specs=[_row_spec(tm, N_PA), _row_spec(tm, N_PB), _row_spec(tm, N_PC),
                   tspec(VD), tspec(VD), tspec(SUBLANES),
                   pl.BlockSpec((1, N_HEADS, EXT, tm), lambda i: (i // per, 0, 0, i % per)),
                   pl.BlockSpec((1, N_HEADS, tm, EXT), lambda i: (i // per, 0, i % per, 0)),
                   tspec(VD)],
        out_shape=[jax.ShapeDtypeStruct((m, N_PA), F32), jax.ShapeDtypeStruct((m, N_PB), F32),
                   jax.ShapeDtypeStruct((m, N_PC), F32),
                   jax.ShapeDtypeStruct((bp, VD, seq), F32), jax.ShapeDtypeStruct((bp, VD, seq), F32),
                   jax.ShapeDtypeStruct((bp, SUBLANES, seq), F32),
                   jax.ShapeDtypeStruct((bp, N_HEADS, EXT, seq), BF16),
                   jax.ShapeDtypeStruct((bp, N_HEADS, seq, EXT), BF16),
                   jax.ShapeDtypeStruct((bp, VD, seq), BF16)],
        scratch_shapes=[pltpu.VMEM((SUBLANES, LANES), F32)],
        compiler_params=_cparams("arbitrary"),
        name="inproj_prompt",
    )(h, g, *ws, bfc)


def _inproj_sample_call(h, g, ws, bfc, bfr):
    n, d = h.shape
    shapes = [(n, N_PA), (N_PB, n), (n, N_PC), (n, VD), (n, VD), (n, VD), (n, LANES),
              (VD, n), (VD, n), (SUBLANES, n)]
    return pl.pallas_call(
        _inproj_sample_body,
        grid=(1,),
        in_specs=[_full_spec((n, d)), _full_spec((1, d))] + _inproj_weight_specs(d)
                 + [_full_spec((SUBLANES, LANES)), _full_spec((1, LANES))],
        out_specs=[pl.BlockSpec(s, lambda i: (0, 0)) for s in shapes],
        out_shape=[jax.ShapeDtypeStruct(s, F32) for s in shapes],
        compiler_params=_cparams("arbitrary"),
        name="inproj_sample",
    )(h, g, *ws, bfc, bfr)


A_PAD = 32
C_PAD = 8
CONV_ROWS = 128


def _conv_prompt_body(pa_ref, pc_ref, aw_ref, ab_ref, lg_ref, lb_ref, cw_ref,
                      ya_ref, yc_ref, bufa_ref, bufc_ref, apad_ref, zpad_ref):
    seq = pa_ref.shape[1]
    c = GROUP_W
    apad_ref[0:A_PAD, :] = jnp.zeros((A_PAD, c), F32)
    apad_ref[A_PAD:, :] = pa_ref[0, :, 0:c] * _sigmoid(pa_ref[0, :, c:])
    zpad_ref[0:C_PAD, :] = jnp.zeros((C_PAD, c), F32)
    zpad_ref[C_PAD:, :] = pc_ref[0, :, c:2 * c] * pc_ref[0, :, 2 * c:]

    def step(i, carry):
        r0 = pl.multiple_of(i * CONV_ROWS, CONV_ROWS)
        win = apad_ref[pl.ds(r0, CONV_ROWS + A_PAD), :]
        acc = jnp.zeros((CONV_ROWS, c), F32) + ab_ref[...]
        for r in range(SUBLANES):
            nrow = CONV_ROWS if r == 0 else CONV_ROWS + SUBLANES
            u = None
            for a8 in range(0, A_PAD + 1, SUBLANES):
                j = a8 + r - (A_PAD - (CONV_A_W - 1))
                if 0 <= j < CONV_A_W:
                    term = aw_ref[j:j + 1, :] * win[a8:a8 + nrow, :]
                    u = term if u is None else u + term
            acc = acc + u[r:r + CONV_ROWS, :]
        mu = jnp.mean(acc, axis=-1, keepdims=True)
        var = jnp.mean(jnp.square(acc - mu), axis=-1, keepdims=True)
        yn = (acc - mu) * lax.rsqrt(var + EPS) * lg_ref[...] + lb_ref[...]
        ya_ref[0, pl.ds(r0, CONV_ROWS), :] = _silu(yn)
        zwin = zpad_ref[pl.ds(r0, CONV_ROWS + C_PAD), :]
        accc = jnp.zeros((CONV_ROWS, c), F32)
        for j in range(CONV_C_W):
            off = C_PAD - (CONV_C_W - 1) + j
            accc = accc + cw_ref[j:j + 1, :] * zwin[off:off + CONV_ROWS, :]
        yc_ref[0, pl.ds(r0, CONV_ROWS), :] = pc_ref[0, pl.ds(r0, CONV_ROWS), 0:c] * accc
        return carry

    lax.fori_loop(0, seq // CONV_ROWS, step, 0)
    na = CONV_A_W - 1
    nc = CONV_C_W - 1
    bufa_ref[0] = apad_ref[seq:A_PAD + seq, :][A_PAD - na:, :]
    bufc_ref[0] = zpad_ref[seq:C_PAD + seq, :][C_PAD - nc:, :]


def _conv_prompt_call(pa, pc, aw, ab, lg, lb, cw):
    b, seq, _ = pa.shape
    c = GROUP_W
    bspec = lambda n: pl.BlockSpec((1, seq, n), lambda i: (i, 0, 0))
    return pl.pallas_call(
        _conv_prompt_body,
        grid=(b,),
        in_specs=[bspec(N_PA), bspec(N_PC), _full_spec((CONV_A_W, c)), _full_spec((1, c)),
                  _full_spec((1, c)), _full_spec((1, c)), _full_spec((CONV_C_W, c))],
        out_specs=[bspec(c), bspec(c),
                   pl.BlockSpec((1, CONV_A_W - 1, c), lambda i: (i, 0, 0)),
                   pl.BlockSpec((1, CONV_C_W - 1, c), lambda i: (i, 0, 0))],
        out_shape=[jax.ShapeDtypeStruct((b, seq, c), F32), jax.ShapeDtypeStruct((b, seq, c), F32),
                   jax.ShapeDtypeStruct((b, CONV_A_W - 1, c), F32),
                   jax.ShapeDtypeStruct((b, CONV_C_W - 1, c), F32)],
        scratch_shapes=[pltpu.VMEM((A_PAD + seq, c), F32), pltpu.VMEM((C_PAD + seq, c), F32)],
        compiler_params=_cparams("arbitrary"),
        name="conv_prompt",
    )(pa, pc, aw, ab, lg, lb, cw)


def _gla_gate(alr, wa2_ref, ba_ref):
    return _log_sigmoid(_dot(alr.astype(BF16), wa2_ref[...]) + ba_ref[...]) * (1.0 / GLA_TAU)


def _gla_out(o, g, ng, hm):
    ms = _dot_sel(o * o, hm) * (1.0 / HEAD_DIM)
    return o * lax.rsqrt(ms + EPS) * ng * _silu(g)


GLA_SLAB = 512


def _group_row(x, g, r):
    n, c = x.shape
    x3 = x.reshape(n // g, g, c)
    return jnp.broadcast_to(x3[:, r:r + 1, :], (n // g, g, c)).reshape(n, c)


def _gla_prompt_body(pb_ref, wa2_ref, ba_ref, ng_ref, y_ref, st_ref, la_ref, s_ref):
    seq = pb_ref.shape[1]
    ck = GLA_CHUNK
    la_ref[...] = _gla_gate(pb_ref[0, :, 2 * KD + 2 * VD:], wa2_ref, ba_ref)
    s_ref[...] = jnp.zeros((KD, VD), F32)

    rs = min(GLA_SLAB, seq)
    nch = rs // ck
    tt = _iota((rs, rs), 0)
    ss = _iota((rs, rs), 1)
    tri = ((ss <= tt) & (_idiv(ss, ck) == _idiv(tt, ck))).astype(BF16)
    blk = _idiv(_iota((KD, VD), 0), DK_B) == _idiv(_iota((KD, VD), 1), HEAD_DIM)
    blk_bf = blk.astype(BF16)
    blk_f = blk.astype(F32)
    hm = _head_mean_matrix()
    scale = DK_B ** -0.5
    levels = [g for g in (2 * SUBLANES, 4 * SUBLANES, 8 * SUBLANES) if g <= ck]
    assert ck == 8 * SUBLANES
    key_head = _idiv(_iota((1, KD), 1), DK_B)
    val_head = _idiv(_iota((1, VD), 1), HEAD_DIM)
    tloc = _iota((rs, KD), 0)

    def slab(si, carry):
        r0 = pl.multiple_of(si * rs, rs)
        rows = pl.ds(r0, rs)
        q = pb_ref[0, rows, 0:KD] * scale
        k = pb_ref[0, rows, KD:2 * KD]
        v = pb_ref[0, rows, 2 * KD:2 * KD + VD]
        g = pb_ref[0, rows, 2 * KD + VD:2 * KD + 2 * VD]
        b = _sel_dot(tri, la_ref[rows, :])
        vb3 = v.astype(BF16).reshape(nch, ck, VD)

        t8 = _imod(tloc, SUBLANES)
        acc = jnp.zeros((rs, VD), F32)
        for s in range(SUBLANES):
            diff = jnp.where(t8 >= s, b - _group_row(b, SUBLANES, s), -jnp.inf)
            e = (q * _group_row(k, SUBLANES, s) * jnp.exp(diff)).astype(BF16)
            acc = acc + _dot(e, blk_bf) * _group_row(v, SUBLANES, s)

        sc = jnp.zeros((nch, N_HEADS * ck, ck), F32)
        for gsz in levels:
            r = _group_row(b, gsz, gsz // 2)
            upper = _imod(tloc, gsz) >= gsz // 2
            qf = q * jnp.exp(jnp.where(upper, b - r, -jnp.inf))
            kf = (k * jnp.exp(jnp.where(upper, -jnp.inf, r - b))).astype(BF16).reshape(nch, ck, KD)
            qstack = jnp.concatenate(
                [jnp.where(key_head == h, qf, 0.0).astype(BF16).reshape(nch, ck, KD)
                 for h in range(N_HEADS)], axis=1)
            sc_g = jnp.einsum('cqk,csk->cqs', qstack, kf, preferred_element_type=F32)
            same_group = _idiv(_imod(_iota((N_HEADS * ck, ck), 0), ck), gsz) == _idiv(_iota((N_HEADS * ck, ck), 1), gsz)
            sc = sc + jnp.where(same_group[None], sc_g, 0.0)
        res = jnp.einsum('cqs,csv->cqv', sc.astype(BF16), vb3, preferred_element_type=F32)
        for h in range(N_HEADS):
            part = res[:, h * ck:(h + 1) * ck, :].reshape(rs, VD)
            acc = acc + jnp.where(val_head == h, part, 0.0)

        b_last = _group_row(b, ck, ck - 1)
        kdec = k * jnp.exp(b_last - b)
        lastb = jnp.concatenate([b[(c + 1) * ck - 1:(c + 1) * ck, :] for c in range(nch)]
                                + [jnp.zeros((LANES - nch, KD), F32)], axis=0)
        dcols = jnp.exp(lastb.T)
        s_cur = s_ref[...]
        s_before = []
        for c in range(nch):
            s_before.append(s_cur.astype(BF16))
            crow = slice(c * ck, (c + 1) * ck)
            upd = _dot(kdec[crow, :].T.astype(BF16), vb3[c])
            s_cur = dcols[:, c:c + 1] * s_cur + upd * blk_f
        s_ref[...] = s_cur
        qb3 = (q * jnp.exp(b)).astype(BF16).reshape(nch, ck, KD)
        inter = jnp.einsum('cqk,ckv->cqv', qb3, jnp.stack(s_before), preferred_element_type=F32)
        o = inter.reshape(rs, VD) + acc
        y_ref[0, rows, :] = _gla_out(o, g, ng_ref[...], hm)
        return carry

    lax.fori_loop(0, seq // rs, slab, 0)
    for h in range(N_HEADS):
        st_ref[0, h] = s_ref[h * DK_B:(h + 1) * DK_B, h * HEAD_DIM:(h + 1) * HEAD_DIM]


def _gla_prompt_call(pb, wa2, ba, ng):
    b, seq, _ = pb.shape
    return pl.pallas_call(
        _gla_prompt_body,
        grid=(b,),
        in_specs=[pl.BlockSpec((1, seq, N_PB), lambda i: (i, 0, 0)), _full_spec((LANES, KD)),
                  _full_spec((1, KD)), _full_spec((1, VD))],
        out_specs=[pl.BlockSpec((1, seq, VD), lambda i: (i, 0, 0)),
                   pl.BlockSpec((1, N_HEADS, DK_B, HEAD_DIM), lambda i: (i, 0, 0, 0))],
        out_shape=[jax.ShapeDtypeStruct((b, seq, VD), F32),
                   jax.ShapeDtypeStruct((b, N_HEADS, DK_B, HEAD_DIM), F32)],
        scratch_shapes=[pltpu.VMEM((seq, KD), F32), pltpu.VMEM((KD, VD), F32)],
        compiler_params=_cparams("arbitrary"),
        name="gla_prompt",
    )(pb, wa2, ba, ng)


def _fox_prompt_body(qe_ref, ke_ref, vtb_ref, o_ref, *, tq):
    qi = pl.program_id(1)
    key_pos = _iota((tq, tq), 0)
    qry_pos = _iota((tq, tq), 1)

    def step(ki, carry, masked):
        ks = pl.ds(pl.multiple_of(ki * tq, tq), tq)
        scores = [_dot(ke_ref[0, h, ks, :], qe_ref[0, h]) for h in range(N_HEADS)]
        soft = []
        for h in range(N_HEADS):
            m, l, _ = carry[h]
            s = scores[h]
            if masked:
                s = jnp.where(key_pos <= qry_pos, s, NEG_INF)
            m_new = jnp.maximum(m, jnp.max(s, axis=0, keepdims=True))
            alpha = jnp.exp(m - m_new)
            p = jnp.exp(s - m_new)
            soft.append((m_new, alpha * l + jnp.sum(p, axis=0, keepdims=True), alpha, p.astype(BF16)))
        out = []
        for h in range(N_HEADS):
            m_new, l, alpha, p = soft[h]
            vh = vtb_ref[0, h * HEAD_DIM:(h + 1) * HEAD_DIM, ks]
            out.append((m_new, l, alpha * carry[h][2] + _dot(vh, p)))
        return tuple(out)

    init = tuple((jnp.full((1, tq), NEG_INF, F32), jnp.zeros((1, tq), F32),
                  jnp.zeros((HEAD_DIM, tq), F32)) for _ in range(N_HEADS))
    carry = lax.fori_loop(0, qi, functools.partial(step, masked=False), init)
    carry = step(qi, carry, True)
    o_ref[0] = jnp.concatenate([acc / l for _, l, acc in carry], axis=0).T


def _fox_prompt_call(qe, ke, vtb, tq):
    b, _, _, seq = qe.shape
    return pl.pallas_call(
        functools.partial(_fox_prompt_body, tq=tq),
        grid=(b, seq // tq),
        in_specs=[pl.BlockSpec((1, N_HEADS, EXT, tq), lambda i, j: (i, 0, 0, j)),
                  pl.BlockSpec((1, N_HEADS, seq, EXT), lambda i, j: (i, 0, 0, 0)),
                  pl.BlockSpec((1, VD, seq), lambda i, j: (i, 0, 0))],
        out_specs=pl.BlockSpec((1, tq, VD), lambda i, j: (i, j, 0)),
        out_shape=jax.ShapeDtypeStruct((b, seq, VD), F32),
        compiler_params=_cparams("arbitrary", "arbitrary"),
        name="fox_prompt",
    )(qe, ke, vtb)


def _mix_sample_body(pa_ref, pbt_ref, pc_ref, bufa_ref, sg_ref, bufc_ref,
                     aw_ref, ab_ref, lg_ref, lb_ref, wa2t_ref, bac_ref, ngc_ref, cw_ref,
                     ya_ref, yb_ref, yc_ref, bufa_o, sg_o, bufc_o, q_s, k_s, dec_s):
    c = GROUP_W
    a = pa_ref[:, 0:c] * _sigmoid(pa_ref[:, c:])
    na = CONV_A_W - 1
    acc = aw_ref[na:na + 1, :] * a + ab_ref[...]
    for j in range(na):
        acc = acc + aw_ref[j:j + 1, :] * bufa_ref[0, j]
    mu = jnp.mean(acc, axis=-1, keepdims=True)
    var = jnp.mean(jnp.square(acc - mu), axis=-1, keepdims=True)
    ya_ref[...] = _silu((acc - mu) * lax.rsqrt(var + EPS) * lg_ref[...] + lb_ref[...])
    for j in range(na - 1):
        bufa_o[j] = bufa_ref[0, j + 1]
    bufa_o[na - 1] = a
    z = pc_ref[:, c:2 * c] * pc_ref[:, 2 * c:]
    conv = (cw_ref[0:1, :] * bufc_ref[0, :, 0:c] + cw_ref[1:2, :] * bufc_ref[0, :, c:]
            + cw_ref[2:3, :] * z)
    yc_ref[...] = pc_ref[:, 0:c] * conv
    bufc_o[:, 0:c] = bufc_ref[0, :, c:]
    bufc_o[:, c:] = z
    q_s[...] = pbt_ref[0:KD, :] * (DK_B ** -0.5)
    k_s[...] = pbt_ref[KD:2 * KD, :]
    gate = _dot(wa2t_ref[...], pbt_ref[2 * KD + 2 * VD:, :].astype(BF16)) + bac_ref[...]
    dec_s[...] = jnp.exp(_log_sigmoid(gate) * (1.0 / GLA_TAU))
    n = pa_ref.shape[0]
    ys = []
    for h in range(N_HEADS):
        vrows = slice(2 * KD + h * HEAD_DIM, 2 * KD + (h + 1) * HEAD_DIM)
        grows = slice(2 * KD + VD + h * HEAD_DIM, 2 * KD + VD + (h + 1) * HEAD_DIM)
        vh = pbt_ref[vrows, :]

        def key_step(kk, o, h=h, vh=vh):
            hk = h * DK_B + kk
            one = pl.ds(hk, 1)
            s_new = dec_s[one, :] * sg_ref[0, hk] + k_s[one, :] * vh
            sg_o[hk] = s_new
            return o + q_s[one, :] * s_new

        o = lax.fori_loop(0, DK_B, key_step, jnp.zeros((HEAD_DIM, n), F32))
        ms = jnp.mean(o * o, axis=0, keepdims=True)
        ng = ngc_ref[h * HEAD_DIM:(h + 1) * HEAD_DIM, :]
        ys.append(o * lax.rsqrt(ms + EPS) * ng * _silu(pbt_ref[grows, :]))
    yb_ref[...] = jnp.concatenate(ys, axis=0).T


def _layer_spec(shape, l):
    return pl.BlockSpec((1,) + tuple(shape[1:]), lambda i: (l,) + (0,) * (len(shape) - 1),
                        pipeline_mode=pl.Buffered(1))


def _mix_sample_call(l, pa, pbt, pc, bufa, sg, bufc, aw, ab, lg, lb, wa2t, bac, ngc, cw):
    n = pa.shape[0]
    c = GROUP_W
    small = (aw, ab, lg, lb, wa2t, bac, ngc, cw)
    out_shape = [jax.ShapeDtypeStruct((n, c), F32)] * 3 + [
        jax.ShapeDtypeStruct(bufa.shape[1:], F32), jax.ShapeDtypeStruct(sg.shape[1:], F32),
        jax.ShapeDtypeStruct(bufc.shape[1:], F32)]
    return pl.pallas_call(
        _mix_sample_body,
        grid=(1,),
        in_specs=[_full_spec(pa.shape), _full_spec(pbt.shape), _full_spec(pc.shape),
                  _layer_spec(bufa.shape, l), _layer_spec(sg.shape, l), _layer_spec(bufc.shape, l)]
                 + [_full_spec(x.shape) for x in small],
        out_specs=[pl.BlockSpec(s.shape, lambda i, nd=len(s.shape): (0,) * nd) for s in out_shape],
        out_shape=out_shape,
        scratch_shapes=[pltpu.VMEM((KD, n), F32)] * 3,
        compiler_params=_cparams("arbitrary"),
        name="mix_sample",
    )(pa, pbt, pc, bufa, sg, bufc, *small)


def _logf_pages_body(x_ref, o_ref):
    n = x_ref.shape[0]
    x = x_ref[...]
    later = (_iota((PAGE_SIZE, PAGE_SIZE), 0) > _iota((PAGE_SIZE, PAGE_SIZE), 1)).astype(BF16)
    ones = jnp.ones((PAGE_SIZE, PAGE_SIZE), BF16)
    within = _dot_sel(x, later)
    total = pltpu.roll(_dot_sel(x, ones), N_HEADS, 0)
    o_ref[...] = jnp.where(_imod(_iota((n, PAGE_SIZE), 0), SUBLANES) < N_HEADS, within, total)


def _logf_pages_call(x, tm):
    m = x.shape[0]
    return pl.pallas_call(
        _logf_pages_body,
        grid=(m // tm,),
        in_specs=[_row_spec(tm, PAGE_SIZE)],
        out_specs=_row_spec(tm, PAGE_SIZE),
        out_shape=jax.ShapeDtypeStruct((m, PAGE_SIZE), F32),
        compiler_params=_cparams("arbitrary"),
        name="logf_pages",
    )(x)


def _fox_decode_body(pt_ref, q_ref, kn_ref, vn_ref, lfn_ref, kc_hbm, vc_hbm, lfc_hbm,
                     o_ref, kbuf, vbuf, lfbuf, sem, *, n_pages, layer):
    b = pl.program_id(0)
    nb = pl.num_programs(0)
    slot = b % 2
    past = n_pages * PAGE_SIZE

    def copies(bb, sl):
        out = []
        for p in range(n_pages):
            page = pt_ref[bb, p]
            cols = pl.ds(p * PAGE_SIZE, PAGE_SIZE)
            out.append(pltpu.make_async_copy(kc_hbm.at[layer, page], kbuf.at[sl, p], sem.at[sl, 0]))
            out.append(pltpu.make_async_copy(vc_hbm.at[layer, page], vbuf.at[sl, p], sem.at[sl, 1]))
            out.append(pltpu.make_async_copy(lfc_hbm.at[layer, page], lfbuf.at[sl, :, cols], sem.at[sl, 2]))
        return out

    @pl.when(b == 0)
    def _():
        for cp in copies(0, 0):
            cp.start()

    @pl.when(b + 1 < nb)
    def _():
        for cp in copies(b + 1, 1 - slot):
            cp.start()

    for cp in copies(b, slot):
        cp.wait()

    scale = HEAD_DIM ** -0.5
    hm = (_idiv(_iota((SUBLANES, VD), 1), HEAD_DIM) == _iota((SUBLANES, VD), 0)).astype(F32)
    qbd = jnp.broadcast_to(q_ref[pl.ds(b, 1), :], (SUBLANES, VD)) * hm
    qb = qbd.astype(BF16)
    s = jnp.concatenate([_dot(qb, kbuf[slot, p].astype(BF16)) for p in range(n_pages)],
                        axis=1) * scale
    lf = lfbuf[slot]
    incl = pltpu.roll(lf, N_HEADS, 0)
    sh = PAGE_SIZE
    while sh < past:
        incl = incl + jnp.concatenate([incl[:, sh:], jnp.zeros((SUBLANES, sh), F32)], axis=1)
        sh *= 2
    later_pages = jnp.concatenate([incl[:, PAGE_SIZE:], jnp.zeros((SUBLANES, PAGE_SIZE), F32)], axis=1)
    valid = _iota((SUBLANES, past), 0) < N_HEADS
    logits = jnp.where(valid, s + lf + later_pages, 0.0)
    s_self = jnp.sum(qbd * kn_ref[pl.ds(b, 1), :], axis=1, keepdims=True) * scale
    pick = (_iota((SUBLANES, LANES), 1) == _iota((SUBLANES, LANES), 0)).astype(F32)
    c_new = jnp.sum(pick * lfn_ref[pl.ds(b, 1), :], axis=1, keepdims=True)
    self_logit = jnp.where(_iota((SUBLANES, 1), 0) < N_HEADS, s_self - c_new, 0.0)
    m = jnp.maximum(jnp.max(logits, axis=1, keepdims=True), self_logit)
    p = jnp.exp(logits - m)
    p_self = jnp.exp(self_logit - m)
    l = jnp.sum(p, axis=1, keepdims=True) + p_self
    pb = p.astype(BF16)
    pv = jnp.zeros((SUBLANES, VD), F32)
    for pg in range(n_pages):
        pv = pv + _dot_nt(pb[:, pg * PAGE_SIZE:(pg + 1) * PAGE_SIZE], vbuf[slot, pg].astype(BF16))
    o8 = (pv + p_self * vn_ref[pl.ds(b, 1), :]) / l
    o_ref[0] = jnp.sum(o8 * hm, axis=0, keepdims=True)


def _fox_decode_call(layer, page_table, q, kn, vn, lfn, kc, vc, lfc):
    n, n_pages = page_table.shape
    past = n_pages * PAGE_SIZE
    grid_spec = pltpu.PrefetchScalarGridSpec(
        num_scalar_prefetch=1,
        grid=(n,),
        in_specs=[pl.BlockSpec((n, VD), lambda i, pt: (0, 0)), pl.BlockSpec((n, VD), lambda i, pt: (0, 0)),
                  pl.BlockSpec((n, VD), lambda i, pt: (0, 0)), pl.BlockSpec((n, LANES), lambda i, pt: (0, 0)),
                  pl.BlockSpec(memory_space=pl.ANY), pl.BlockSpec(memory_space=pl.ANY),
                  pl.BlockSpec(memory_space=pl.ANY)],
        out_specs=pl.BlockSpec((1, 1, VD), lambda i, pt: (i, 0, 0)),
        scratch_shapes=[pltpu.VMEM((2, n_pages, VD, PAGE_SIZE), F32),
                        pltpu.VMEM((2, n_pages, VD, PAGE_SIZE), F32),
                        pltpu.VMEM((2, SUBLANES, past), F32), pltpu.SemaphoreType.DMA((2, 3))],
    )
    return pl.pallas_call(
        functools.partial(_fox_decode_body, n_pages=n_pages, layer=layer),
        grid_spec=grid_spec,
        out_shape=jax.ShapeDtypeStruct((n, 1, VD), F32),
        compiler_params=_cparams("arbitrary"),
        name="fox_decode",
    )(page_table, q, kn, vn, lfn, kc, vc, lfc)


ATTN_TILE = 256


def _row_tile(m, cap=512):
    tm = cap
    while tm >= SUBLANES:
        if m % tm == 0:
            return tm
        tm //= 2
    raise ValueError(f"row count {m} is not a multiple of {SUBLANES}")


def kernel(x_prompt, x_sample, state_conv_a, state_gla, state_conv_c, cache_k, cache_v, cache_logf, page_table, ffn1_pre_g, ffn1_post_g, ffn1_w_gate, ffn1_w_up, ffn1_w_down, mix_pre_g, mix_post_g, w_in, w_out, a_conv_w, a_conv_b, a_ln_g, a_ln_b, b_gate_w2, b_gate_b, b_out_norm_g, c_conv_w, d_forget_b, ffn2_pre_g, ffn2_post_g, ffn2_w_gate, ffn2_w_up, ffn2_w_down):
    depth = w_in.shape[0]
    bp, seq, d = x_prompt.shape
    bd = x_sample.shape[0]
    n_pool = cache_k.shape[1]
    assert x_sample.shape[1] == 1 and d == D_MODEL
    assert seq % ATTN_TILE == 0 and seq % GLA_CHUNK == 0 and seq >= CONV_A_W - 1

    hp = x_prompt.reshape(bp * seq, d)
    hs = x_sample.reshape(bd, d)
    tmp = _row_tile(bp * seq)
    tms = _row_tile(bd)
    tq = ATTN_TILE

    off_b = N_PA
    off_c = off_b + 2 * KD + 2 * VD + GLA_RANK
    off_d = off_c + N_PC
    row = lambda v: v.reshape(1, -1)

    lanes_of = lambda v, n=LANES: jnp.broadcast_to(v[:, None], (v.shape[0], n))

    ck = cache_k.transpose(0, 1, 3, 4, 2).reshape(depth, n_pool, VD, PAGE_SIZE)
    cv = cache_v.transpose(0, 1, 3, 4, 2).reshape(depth, n_pool, VD, PAGE_SIZE)
    sca = state_conv_a.transpose(0, 2, 1, 3)
    sgl = state_gla.transpose(0, 2, 3, 4, 1).reshape(depth, KD, HEAD_DIM, bd)
    scc = state_conv_c.reshape(depth, bd, (CONV_C_W - 1) * GROUP_W)
    w_in_t = w_in.transpose(0, 2, 1)

    lf_t = jnp.pad(cache_logf.transpose(0, 1, 3, 2), ((0, 0), (0, 0), (0, SUBLANES - N_HEADS), (0, 0)))
    lf_rows = depth * n_pool * SUBLANES
    lf_pages = _logf_pages_call(lf_t.reshape(lf_rows, PAGE_SIZE), _row_tile(lf_rows, cap=4096))
    lf_pages = lf_pages.reshape(depth, n_pool, SUBLANES, PAGE_SIZE)

    outs = [[] for _ in range(12)]
    for l in range(depth):
        wg1, wu1, wd1 = (w[l].astype(BF16) for w in (ffn1_w_gate, ffn1_w_up, ffn1_w_down))
        wg2, wu2, wd2 = (w[l].astype(BF16) for w in (ffn2_w_gate, ffn2_w_up, ffn2_w_down))
        wt = w_in_t[l].astype(BF16)
        pad_rows = lambda w, n: jnp.pad(w, ((0, n - w.shape[0]), (0, 0)))
        wq, wk, wv = (wt[off_d + i * VD:off_d + (i + 1) * VD] for i in range(3))
        in_w = (wt[0:off_b], pad_rows(wt[off_b:off_c], N_PB), wt[off_c:off_d],
                wq, wk, wv, pad_rows(wt[off_d + 3 * VD:], LANES))
        slabs = lambda w: jnp.pad(w.reshape(N_HEADS, HEAD_DIM, d),
                                  ((0, 0), (0, EXT - HEAD_DIM), (0, 0))).reshape(N_HEADS * EXT, d)
        in_w_prompt = in_w[0:3] + (wq, slabs(wk), wk, wv, in_w[6])
        wo = w_out[l].astype(BF16)
        bf_pad = jnp.pad(d_forget_b[l], (0, LANES - N_HEADS))
        bfr = row(bf_pad)
        bfc = lanes_of(bf_pad[0:SUBLANES])
        wa2 = jnp.pad(b_gate_w2[l], ((0, LANES - GLA_RANK), (0, 0))).astype(BF16)
        mix_w = (a_conv_w[l], row(a_conv_b[l]), row(a_ln_g[l]), row(a_ln_b[l]))
        ffn1_w = (row(ffn1_pre_g[l]), row(ffn1_post_g[l]), wg1, wu1, wd1)
        ffn2_w = (row(ffn2_pre_g[l]), row(ffn2_post_g[l]), wg2, wu2, wd2)

        hp = _ffn_call(hp, *ffn1_w, tm=tmp)
        pa, pb, pc, kt, vt, lft, qe, ke, vtb = _inproj_prompt_call(
            hp, row(mix_pre_g[l]), in_w_prompt, bfc, bp, seq, tm=tmp)
        r3 = lambda t: t.reshape(bp, seq, t.shape[-1])
        ya, yc, buf_a, buf_c = _conv_prompt_call(r3(pa), r3(pc), *mix_w, c_conv_w[l])
        yb, s_b = _gla_prompt_call(r3(pb), wa2, row(b_gate_b[l]), row(b_out_norm_g[l]))
        yd = _fox_prompt_call(qe, ke, vtb, tq)
        f2 = lambda t: t.reshape(bp * seq, t.shape[-1])
        hp = _mix_ffn_call(hp, f2(ya), f2(yb), f2(yc), f2(yd), wo, row(mix_post_g[l]), *ffn2_w, tm=tmp)
        for i, t in enumerate((buf_a, s_b, buf_c, kt, vt, lft[:, 0:N_HEADS, :])):
            outs[i].append(t)

        hs = _ffn_call(hs, *ffn1_w, tm=tms)
        pa, pbt, pc, q, k, v, lf, kt, vt, lft = _inproj_sample_call(hs, row(mix_pre_g[l]), in_w, bfc, bfr)
        ya, yb, yc, buf_a, s_b, buf_c = _mix_sample_call(
            l, pa, pbt, pc, sca, sgl, scc, *mix_w, wa2.T, lanes_of(b_gate_b[l], bd),
            lanes_of(b_out_norm_g[l], bd), c_conv_w[l])
        yd = _fox_decode_call(l, page_table, q, k, v, lf, ck, cv, lf_pages)
        hs = _mix_ffn_call(hs, ya, yb, yc, yd.reshape(bd, VD), wo, row(mix_post_g[l]), *ffn2_w, tm=tms)
        for i, t in enumerate((buf_a, s_b, buf_c, kt, vt, lft[0:N_HEADS, :])):
            outs[6 + i].append(t)

    p_ca, p_gla, p_cc, p_kt, p_vt, p_lft, s_ca, s_gla, s_cc, s_kt, s_vt, s_lft = (jnp.stack(o) for o in outs)
    heads = lambda t: t.reshape(t.shape[:-2] + (N_HEADS, HEAD_DIM, t.shape[-1]))
    return (hp.reshape(bp, seq, d), hs.reshape(bd, 1, d),
            p_ca, p_gla, p_cc,
            heads(p_kt).transpose(0, 1, 4, 2, 3), heads(p_vt).transpose(0, 1, 4, 2, 3),
            p_lft.transpose(0, 1, 3, 2),
            s_ca.transpose(0, 2, 1, 3),
            s_gla.reshape(depth, N_HEADS, DK_B, HEAD_DIM, bd).transpose(0, 4, 1, 2, 3),
            s_cc.reshape(depth, bd, CONV_C_W - 1, GROUP_W),
            heads(s_kt).transpose(0, 3, 1, 2)[:, :, None], heads(s_vt).transpose(0, 3, 1, 2)[:, :, None],
            s_lft.transpose(0, 2, 1)[:, :, None])
```

```python
import functools

import jax
import jax.numpy as jnp
from jax import lax
from jax.experimental import pallas as pl
from jax.experimental.pallas import tpu as pltpu

F32 = jnp.float32
BF16 = jnp.bfloat16

D_MODEL = 1024
GROUP_W = D_MODEL // 4
HEAD_DIM = 64
N_HEADS = GROUP_W // HEAD_DIM
DK_B = HEAD_DIM // 2
GLA_RANK = 16
GLA_TAU = 16.0
GLA_CHUNK = 64
CONV_A_W = 31
CONV_C_W = 3
PAGE_SIZE = 128
EPS = 1e-6
NEG_INF = -1e30

LANES = 128
SUBLANES = 8
VMEM_LIMIT_BYTES = 56 * 1024 * 1024

N_PA = 2 * GROUP_W
KD = N_HEADS * DK_B
VD = N_HEADS * HEAD_DIM
N_PB = 2 * KD + 2 * VD + LANES
N_PC = 3 * GROUP_W


def _cparams(*sem):
    return pltpu.CompilerParams(dimension_semantics=sem, vmem_limit_bytes=VMEM_LIMIT_BYTES)


def _dot(a, b):
    return jnp.dot(a, b, preferred_element_type=F32)


def _dot_nt(a, b):
    return lax.dot_general(a, b, (((1,), (1,)), ((), ())), preferred_element_type=F32)


def _split3(x):
    hi = x.astype(BF16)
    r = x - hi.astype(F32)
    mid = r.astype(BF16)
    lo = (r - mid.astype(F32)).astype(BF16)
    return hi, mid, lo


def _dot_sel(x, sel):
    hi, mid, lo = _split3(x)
    return _dot(hi, sel) + _dot(mid, sel) + _dot(lo, sel)


def _sel_dot(sel, x):
    hi, mid, lo = _split3(x)
    return _dot(sel, hi) + _dot(sel, mid) + _dot(sel, lo)


def _rms(x, g):
    return x * lax.rsqrt(jnp.mean(x * x, axis=-1, keepdims=True) + EPS) * g


def _sigmoid(x):
    return 1.0 / (1.0 + jnp.exp(-x))


def _silu(x):
    return x * _sigmoid(x)


def _log_sigmoid(x):
    return jnp.minimum(x, 0.0) - jnp.log1p(jnp.exp(-jnp.abs(x)))


def _iota(shape, dim):
    return lax.broadcasted_iota(jnp.int32, shape, dim)


def _idiv(x, n):
    assert n & (n - 1) == 0
    return lax.shift_right_logical(x, n.bit_length() - 1)


def _imod(x, n):
    assert n & (n - 1) == 0
    return x & (n - 1)


def _head_mean_matrix():
    r = _idiv(_iota((VD, VD), 0), HEAD_DIM)
    c = _idiv(_iota((VD, VD), 1), HEAD_DIM)
    return (r == c).astype(BF16)


def _swiglu_residual(x, pre_g, post_g, wg_ref, wu_ref, wd_ref, acc_ref, fc):
    xn = _rms(x, pre_g).astype(BF16)
    for c in range(wg_ref.shape[1] // fc):
        sl = slice(c * fc, (c + 1) * fc)
        g = _dot(xn, wg_ref[:, sl])
        u = _dot(xn, wu_ref[:, sl])
        hid = (_silu(g) * u).astype(BF16)
        part = _dot(hid, wd_ref[sl, :])
        if c == 0:
            acc_ref[...] = part
        else:
            acc_ref[...] += part
    return x + 0.5 * _rms(acc_ref[...], post_g)


def _ffn_body(x_ref, pre_ref, post_ref, wg_ref, wu_ref, wd_ref, o_ref, acc_ref, *, fc):
    o_ref[...] = _swiglu_residual(x_ref[...], pre_ref[...], post_ref[...],
                                  wg_ref.at[0], wu_ref.at[0], wd_ref.at[0], acc_ref, fc)


def _mix_ffn_body(h_ref, ya_ref, yb_ref, yc_ref, yd_ref, wo_ref, mpost_ref,
                  pre_ref, post_ref, wg_ref, wu_ref, wd_ref, o_ref, acc_ref, *, fc):
    ycat = jnp.concatenate([ya_ref[...], yb_ref[...], yc_ref[...], yd_ref[...]], axis=1)
    y = _dot(ycat.astype(BF16), wo_ref[0])
    h = h_ref[...] + _rms(y, mpost_ref[...])
    o_ref[...] = _swiglu_residual(h, pre_ref[...], post_ref[...],
                                  wg_ref.at[0], wu_ref.at[0], wd_ref.at[0], acc_ref, fc)


def _row_spec(tm, n):
    return pl.BlockSpec((tm, n), lambda i: (i, 0))


def _full_spec(shape):
    return pl.BlockSpec(shape, lambda i: (0,) * len(shape), pipeline_mode=pl.Buffered(1))


def _ffn_call(layer, x, pre_g, post_g, wg, wu, wd, tm, fc=256):
    m, d = x.shape
    return pl.pallas_call(
        functools.partial(_ffn_body, fc=fc),
        grid=(m // tm,),
        in_specs=[_row_spec(tm, d), _full_spec((1, d)), _full_spec((1, d)),
                  _layer_spec(wg.shape, layer), _layer_spec(wu.shape, layer), _layer_spec(wd.shape, layer)],
        out_specs=_row_spec(tm, d),
        out_shape=jax.ShapeDtypeStruct((m, d), F32),
        scratch_shapes=[pltpu.VMEM((tm, d), F32)],
        compiler_params=_cparams("arbitrary"),
        name="ffn",
    )(x, pre_g, post_g, wg, wu, wd)


def _mix_ffn_call(layer, h, ya, yb, yc, yd, wo, mpost, pre_g, post_g, wg, wu, wd, tm, fc=256):
    m, d = h.shape
    return pl.pallas_call(
        functools.partial(_mix_ffn_body, fc=fc),
        grid=(m // tm,),
        in_specs=[_row_spec(tm, d)] + [_row_spec(tm, GROUP_W)] * 4
                 + [_layer_spec(wo.shape, layer), _full_spec((1, d)), _full_spec((1, d)), _full_spec((1, d)),
                    _layer_spec(wg.shape, layer), _layer_spec(wu.shape, layer), _layer_spec(wd.shape, layer)],
        out_specs=_row_spec(tm, d),
        out_shape=jax.ShapeDtypeStruct((m, d), F32),
        scratch_shapes=[pltpu.VMEM((tm, d), F32)],
        compiler_params=_cparams("arbitrary"),
        name="mix_ffn",
    )(h, ya, yb, yc, yd, wo, mpost, pre_g, post_g, wg, wu, wd)


EXT = LANES
C_KEY = HEAD_DIM
C_QRY = HEAD_DIM + 3


def _inproj_prompt_body(h_ref, g_ref, wa_ref, wb_ref, wc_ref, wq_ref, wke_ref, wk_ref, wv_ref, wl_ref,
                        bfc_ref, pa_ref, pb_ref, pc_ref, kt_ref, vt_ref, lft_ref,
                        qe_ref, ke_ref, vtb_ref, carry_ref, *, per):
    tm = h_ref.shape[0]
    u = _rms(h_ref[...], g_ref[...]).astype(BF16)
    pa_ref[...] = _dot_nt(u, wa_ref[...])
    pb_ref[...] = _dot_nt(u, wb_ref[...])
    pc_ref[...] = _dot_nt(u, wc_ref[...])
    kt_ref[0] = _dot_nt(wk_ref[...], u)
    vt = _dot_nt(wv_ref[...], u)
    vt_ref[0] = vt
    vtb_ref[0] = vt.astype(BF16)
    lft = _log_sigmoid(_dot_nt(wl_ref[0:SUBLANES, :], u) + bfc_ref[:, 0:1])
    lft_ref[0] = lft

    first = pl.program_id(0) % per == 0
    carry = jnp.where(first, 0.0, carry_ref[:, 0:1])
    upto = (_iota((tm, tm), 0) <= _iota((tm, tm), 1)).astype(BF16)
    lfm = jnp.where(_iota((SUBLANES, tm), 0) < N_HEADS, lft, 0.0)
    c = _dot_sel(lfm, upto) + carry
    carry_ref[...] = jnp.broadcast_to(c[:, tm - 1:tm], carry_ref.shape)
    ccol = jnp.concatenate([c, jnp.zeros((LANES - SUBLANES, tm), F32)], axis=0).T

    hi, mid, lo = _split3(ccol)
    pieces = (hi.astype(F32) + pltpu.roll(mid.astype(F32), N_HEADS, 1)
              + pltpu.roll(lo.astype(F32), 2 * N_HEADS, 1)).astype(BF16)
    r = _iota((LANES, N_HEADS * EXT), 0)
    col = _iota((LANES, N_HEADS * EXT), 1)
    j = _imod(col, EXT) - C_KEY
    place_k = ((j >= 0) & (j < 3) & (r == N_HEADS * j + _idiv(col, EXT))).astype(BF16)
    lane = _imod(_iota((1, N_HEADS * EXT), 1), EXT)
    ones_k = ((lane >= C_QRY) & (lane < C_QRY + 3)).astype(F32)
    ke = _dot_nt(u, wke_ref[...]) + _dot(pieces, place_k) + ones_k
    for h in range(N_HEADS):
        ke_ref[0, h] = ke[:, h * EXT:(h + 1) * EXT].astype(BF16)

    hi, mid, lo = _split3(c)
    pieces_t = jnp.concatenate([hi.astype(F32), mid.astype(F32), lo.astype(F32),
                                jnp.zeros((LANES - 3 * SUBLANES, tm), F32)], axis=0).astype(BF16)
    r = _iota((VD, LANES), 0)
    col = _iota((VD, LANES), 1)
    j = _imod(r, HEAD_DIM) - (C_QRY - HEAD_DIM)
    place_q = ((j >= 0) & (j < 3) & (col == SUBLANES * j + _idiv(r, HEAD_DIM))).astype(BF16)
    rowi = _imod(_iota((VD, 1), 0), HEAD_DIM)
    neg_q = jnp.where(rowi < 3, -1.0, 0.0)
    extra = (_dot(place_q, pieces_t) + neg_q).astype(BF16)
    qt = (_dot_nt(wq_ref[...], u) * (HEAD_DIM ** -0.5)).astype(BF16)
    for h in range(N_HEADS):
        rows = slice(h * HEAD_DIM, (h + 1) * HEAD_DIM)
        qe_ref[0, h, 0:HEAD_DIM, :] = qt[rows, :]
        qe_ref[0, h, HEAD_DIM:, :] = extra[rows, :]


def _inproj_sample_body(h_ref, g_ref, wa_ref, wb_ref, wc_ref, wq_ref, wk_ref, wv_ref, wl_ref,
                        bfc_ref, bfr_ref,
                        pa_ref, pbt_ref, pc_ref, q_ref, k_ref, v_ref, lf_ref, kt_ref, vt_ref, lft_ref):
    u = _rms(h_ref[...], g_ref[...]).astype(BF16)
    pa_ref[...] = _dot_nt(u, wa_ref[...])
    pbt_ref[...] = _dot_nt(wb_ref[...], u)
    pc_ref[...] = _dot_nt(u, wc_ref[...])
    q_ref[...] = _dot_nt(u, wq_ref[...])
    k_ref[...] = _dot_nt(u, wk_ref[...])
    v_ref[...] = _dot_nt(u, wv_ref[...])
    lf_ref[...] = _log_sigmoid(_dot_nt(u, wl_ref[...]) + bfr_ref[...])
    kt_ref[...] = _dot_nt(wk_ref[...], u)
    vt_ref[...] = _dot_nt(wv_ref[...], u)
    lft_ref[...] = _log_sigmoid(_dot_nt(wl_ref[0:SUBLANES, :], u) + bfc_ref[:, 0:1])


def _inproj_weight_specs(d):
    return [_full_spec((n, d)) for n in (N_PA, N_PB, N_PC, VD, VD, VD, LANES)]


def _inproj_prompt_call(h, g, ws, bfc, bp, seq, tm):
    m, d = h.shape
    per = seq // tm
    tspec = lambda n: pl.BlockSpec((1, n, tm), lambda i: (i // per, 0, i % per))
    hx = N_HEADS * EXT
    return pl.pallas_call(
        functools.partial(_inproj_prompt_body, per=per),
        grid=(m // tm,),
        in_specs=[_row_spec(tm, d), _full_spec((1, d))]
                 + [_full_spec((n, d)) for n in (N_PA, N_PB, N_PC, VD, hx, VD, VD, LANES)]
                 + [_full_spec((SUBLANES, LANES))],
        out_specs=[_row_spec(tm, N_PA), _row_spec(tm, N_PB), _row_spec(tm, N_PC),
                   tspec(VD), tspec(VD), tspec(SUBLANES),
                   pl.BlockSpec((1, N_HEADS, EXT, tm), lambda i: (i // per, 0, 0, i % per)),
                   pl.BlockSpec((1, N_HEADS, tm, EXT), lambda i: (i // per, 0, i % per, 0)),
                   tspec(VD)],
        out_shape=[jax.ShapeDtypeStruct((m, N_PA), F32), jax.ShapeDtypeStruct((m, N_PB), F32),
                   jax.ShapeDtypeStruct((m, N_PC), F32),
                   jax.ShapeDtypeStruct((bp, VD, seq), F32), jax.ShapeDtypeStruct((bp, VD, seq), F32),
                   jax.ShapeDtypeStruct((bp, SUBLANES, seq), F32),
                   jax.ShapeDtypeStruct((bp, N_HEADS, EXT, seq), BF16),
                   jax.ShapeDtypeStruct((bp, N_HEADS, seq, EXT), BF16),
                   jax.ShapeDtypeStruct((bp, VD, seq), BF16)],
        scratch_shapes=[pltpu.VMEM((SUBLANES, LANES), F32)],
        compiler_params=_cparams("arbitrary"),
        name="inproj_prompt",
    )(h, g, *ws, bfc)


def _inproj_sample_call(h, g, ws, bfc, bfr):
    n, d = h.shape
    shapes = [(n, N_PA), (N_PB, n), (n, N_PC), (n, VD), (n, VD), (n, VD), (n, LANES),
              (VD, n), (VD, n), (SUBLANES, n)]
    return pl.pallas_call(
        _inproj_sample_body,
        grid=(1,),
        in_specs=[_full_spec((n, d)), _full_spec((1, d))] + _inproj_weight_specs(d)
                 + [_full_spec((SUBLANES, LANES)), _full_spec((1, LANES))],
        out_specs=[pl.BlockSpec(s, lambda i: (0, 0)) for s in shapes],
        out_shape=[jax.ShapeDtypeStruct(s, F32) for s in shapes],
        compiler_params=_cparams("arbitrary"),
        name="inproj_sample",
    )(h, g, *ws, bfc, bfr)


A_PAD = 32
C_PAD = 8
CONV_ROWS = 128


def _conv_prompt_body(pa_ref, pc_ref, aw_ref, ab_ref, lg_ref, lb_ref, cw_ref,
                      ya_ref, yc_ref, bufa_ref, bufc_ref, apad_ref, zpad_ref):
    seq = pa_ref.shape[1]
    c = GROUP_W
    apad_ref[0:A_PAD, :] = jnp.zeros((A_PAD, c), F32)
    apad_ref[A_PAD:, :] = pa_ref[0, :, 0:c] * _sigmoid(pa_ref[0, :, c:])
    zpad_ref[0:C_PAD, :] = jnp.zeros((C_PAD, c), F32)
    zpad_ref[C_PAD:, :] = pc_ref[0, :, c:2 * c] * pc_ref[0, :, 2 * c:]

    def step(i, carry):
        r0 = pl.multiple_of(i * CONV_ROWS, CONV_ROWS)
        win = apad_ref[pl.ds(r0, CONV_ROWS + A_PAD), :]
        acc = jnp.zeros((CONV_ROWS, c), F32) + ab_ref[...]
        for r in range(SUBLANES):
            nrow = CONV_ROWS if r == 0 else CONV_ROWS + SUBLANES
            u = None
            for a8 in range(0, A_PAD + 1, SUBLANES):
                j = a8 + r - (A_PAD - (CONV_A_W - 1))
                if 0 <= j < CONV_A_W:
                    term = aw_ref[j:j + 1, :] * win[a8:a8 + nrow, :]
                    u = term if u is None else u + term
            acc = acc + u[r:r + CONV_ROWS, :]
        mu = jnp.mean(acc, axis=-1, keepdims=True)
        var = jnp.mean(jnp.square(acc - mu), axis=-1, keepdims=True)
        yn = (acc - mu) * lax.rsqrt(var + EPS) * lg_ref[...] + lb_ref[...]
        ya_ref[0, pl.ds(r0, CONV_ROWS), :] = _silu(yn)
        zwin = zpad_ref[pl.ds(r0, CONV_ROWS + C_PAD), :]
        accc = jnp.zeros((CONV_ROWS, c), F32)
        for j in range(CONV_C_W):
            off = C_PAD - (CONV_C_W - 1) + j
            accc = accc + cw_ref[j:j + 1, :] * zwin[off:off + CONV_ROWS, :]
        yc_ref[0, pl.ds(r0, CONV_ROWS), :] = pc_ref[0, pl.ds(r0, CONV_ROWS), 0:c] * accc
        return carry

    lax.fori_loop(0, seq // CONV_ROWS, step, 0)
    na = CONV_A_W - 1
    nc = CONV_C_W - 1
    bufa_ref[0] = apad_ref[seq:A_PAD + seq, :][A_PAD - na:, :]
    bufc_ref[0] = zpad_ref[seq:C_PAD + seq, :][C_PAD - nc:, :]


def _conv_prompt_call(pa, pc, aw, ab, lg, lb, cw):
    b, seq, _ = pa.shape
    c = GROUP_W
    bspec = lambda n: pl.BlockSpec((1, seq, n), lambda i: (i, 0, 0))
    return pl.pallas_call(
        _conv_prompt_body,
        grid=(b,),
        in_specs=[bspec(N_PA), bspec(N_PC), _full_spec((CONV_A_W, c)), _full_spec((1, c)),
                  _full_spec((1, c)), _full_spec((1, c)), _full_spec((CONV_C_W, c))],
        out_specs=[bspec(c), bspec(c),
                   pl.BlockSpec((1, CONV_A_W - 1, c), lambda i: (i, 0, 0)),
                   pl.BlockSpec((1, CONV_C_W - 1, c), lambda i: (i, 0, 0))],
        out_shape=[jax.ShapeDtypeStruct((b, seq, c), F32), jax.ShapeDtypeStruct((b, seq, c), F32),
                   jax.ShapeDtypeStruct((b, CONV_A_W - 1, c), F32),
                   jax.ShapeDtypeStruct((b, CONV_C_W - 1, c), F32)],
        scratch_shapes=[pltpu.VMEM((A_PAD + seq, c), F32), pltpu.VMEM((C_PAD + seq, c), F32)],
        compiler_params=_cparams("arbitrary"),
        name="conv_prompt",
    )(pa, pc, aw, ab, lg, lb, cw)


def _gla_gate(alr, wa2_ref, ba_ref):
    return _log_sigmoid(_dot(alr.astype(BF16), wa2_ref[...]) + ba_ref[...]) * (1.0 / GLA_TAU)


def _gla_out(o, g, ng, hm):
    ms = _dot_sel(o * o, hm) * (1.0 / HEAD_DIM)
    return o * lax.rsqrt(ms + EPS) * ng * _silu(g)


GLA_SLAB = 512


def _group_row(x, g, r):
    n, c = x.shape
    x3 = x.reshape(n // g, g, c)
    return jnp.broadcast_to(x3[:, r:r + 1, :], (n // g, g, c)).reshape(n, c)


def _gla_prompt_body(pb_ref, wa2_ref, ba_ref, ng_ref, y_ref, st_ref, la_ref, s_ref):
    seq = pb_ref.shape[1]
    ck = GLA_CHUNK
    la_ref[...] = _gla_gate(pb_ref[0, :, 2 * KD + 2 * VD:], wa2_ref, ba_ref)
    s_ref[...] = jnp.zeros((KD, VD), F32)

    rs = min(GLA_SLAB, seq)
    nch = rs // ck
    tt = _iota((rs, rs), 0)
    ss = _iota((rs, rs), 1)
    tri = ((ss <= tt) & (_idiv(ss, ck) == _idiv(tt, ck))).astype(BF16)
    blk = _idiv(_iota((KD, VD), 0), DK_B) == _idiv(_iota((KD, VD), 1), HEAD_DIM)
    blk_bf = blk.astype(BF16)
    blk_f = blk.astype(F32)
    hm = _head_mean_matrix()
    scale = DK_B ** -0.5
    levels = [g for g in (2 * SUBLANES, 4 * SUBLANES, 8 * SUBLANES) if g <= ck]
    assert ck == 8 * SUBLANES
    key_head = _idiv(_iota((1, KD), 1), DK_B)
    val_head = _idiv(_iota((1, VD), 1), HEAD_DIM)
    tloc = _iota((rs, KD), 0)

    def slab(si, carry):
        r0 = pl.multiple_of(si * rs, rs)
        rows = pl.ds(r0, rs)
        q = pb_ref[0, rows, 0:KD] * scale
        k = pb_ref[0, rows, KD:2 * KD]
        v = pb_ref[0, rows, 2 * KD:2 * KD + VD]
        g = pb_ref[0, rows, 2 * KD + VD:2 * KD + 2 * VD]
        b = _sel_dot(tri, la_ref[rows, :])
        vb3 = v.astype(BF16).reshape(nch, ck, VD)

        t8 = _imod(tloc, SUBLANES)
        acc = jnp.zeros((rs, VD), F32)
        for s in range(SUBLANES):
            diff = jnp.where(t8 >= s, b - _group_row(b, SUBLANES, s), -jnp.inf)
            e = (q * _group_row(k, SUBLANES, s) * jnp.exp(diff)).astype(BF16)
            acc = acc + _dot(e, blk_bf) * _group_row(v, SUBLANES, s)

        sc = jnp.zeros((nch, N_HEADS * ck, ck), F32)
        for gsz in levels:
            r = _group_row(b, gsz, gsz // 2)
            upper = _imod(tloc, gsz) >= gsz // 2
            qf = q * jnp.exp(jnp.where(upper, b - r, -jnp.inf))
            kf = (k * jnp.exp(jnp.where(upper, -jnp.inf, r - b))).astype(BF16).reshape(nch, ck, KD)
            qstack = jnp.concatenate(
                [jnp.where(key_head == h, qf, 0.0).astype(BF16).reshape(nch, ck, KD)
                 for h in range(N_HEADS)], axis=1)
            sc_g = jnp.einsum('cqk,csk->cqs', qstack, kf, preferred_element_type=F32)
            same_group = _idiv(_imod(_iota((N_HEADS * ck, ck), 0), ck), gsz) == _idiv(_iota((N_HEADS * ck, ck), 1), gsz)
            sc = sc + jnp.where(same_group[None], sc_g, 0.0)
        res = jnp.einsum('cqs,csv->cqv', sc.astype(BF16), vb3, preferred_element_type=F32)
        for h in range(N_HEADS):
            part = res[:, h * ck:(h + 1) * ck, :].reshape(rs, VD)
            acc = acc + jnp.where(val_head == h, part, 0.0)

        b_last = _group_row(b, ck, ck - 1)
        kdec = k * jnp.exp(b_last - b)
        lastb = jnp.concatenate([b[(c + 1) * ck - 1:(c + 1) * ck, :] for c in range(nch)]
                                + [jnp.zeros((LANES - nch, KD), F32)], axis=0)
        dcols = jnp.exp(lastb.T)
        s_cur = s_ref[...]
        s_before = []
        for c in range(nch):
            s_before.append(s_cur.astype(BF16))
            crow = slice(c * ck, (c + 1) * ck)
            upd = _dot(kdec[crow, :].T.astype(BF16), vb3[c])
            s_cur = dcols[:, c:c + 1] * s_cur + upd * blk_f
        s_ref[...] = s_cur
        qb3 = (q * jnp.exp(b)).astype(BF16).reshape(nch, ck, KD)
        inter = jnp.einsum('cqk,ckv->cqv', qb3, jnp.stack(s_before), preferred_element_type=F32)
        o = inter.reshape(rs, VD) + acc
        y_ref[0, rows, :] = _gla_out(o, g, ng_ref[...], hm)
        return carry

    lax.fori_loop(0, seq // rs, slab, 0)
    for h in range(N_HEADS):
        st_ref[0, h] = s_ref[h * DK_B:(h + 1) * DK_B, h * HEAD_DIM:(h + 1) * HEAD_DIM]


def _gla_prompt_call(pb, wa2, ba, ng):
    b, seq, _ = pb.shape
    return pl.pallas_call(
        _gla_prompt_body,
        grid=(b,),
        in_specs=[pl.BlockSpec((1, seq, N_PB), lambda i: (i, 0, 0)), _full_spec((LANES, KD)),
                  _full_spec((1, KD)), _full_spec((1, VD))],
        out_specs=[pl.BlockSpec((1, seq, VD), lambda i: (i, 0, 0)),
                   pl.BlockSpec((1, N_HEADS, DK_B, HEAD_DIM), lambda i: (i, 0, 0, 0))],
        out_shape=[jax.ShapeDtypeStruct((b, seq, VD), F32),
                   jax.ShapeDtypeStruct((b, N_HEADS, DK_B, HEAD_DIM), F32)],
        scratch_shapes=[pltpu.VMEM((seq, KD), F32), pltpu.VMEM((KD, VD), F32)],
        compiler_params=_cparams("arbitrary"),
        name="gla_prompt",
    )(pb, wa2, ba, ng)


def _fox_prompt_body(qe_ref, ke_ref, vtb_ref, o_ref, *, tq):
    qi = pl.program_id(1)
    key_pos = _iota((tq, tq), 0)
    qry_pos = _iota((tq, tq), 1)

    def step(ki, carry, masked):
        ks = pl.ds(pl.multiple_of(ki * tq, tq), tq)
        scores = [_dot(ke_ref[0, h, ks, :], qe_ref[0, h]) for h in range(N_HEADS)]
        soft = []
        for h in range(N_HEADS):
            m, l, _ = carry[h]
            s = scores[h]
            if masked:
                s = jnp.where(key_pos <= qry_pos, s, NEG_INF)
            m_new = jnp.maximum(m, jnp.max(s, axis=0, keepdims=True))
            alpha = jnp.exp(m - m_new)
            p = jnp.exp(s - m_new)
            soft.append((m_new, alpha * l + jnp.sum(p, axis=0, keepdims=True), alpha, p.astype(BF16)))
        out = []
        for h in range(N_HEADS):
            m_new, l, alpha, p = soft[h]
            vh = vtb_ref[0, h * HEAD_DIM:(h + 1) * HEAD_DIM, ks]
            out.append((m_new, l, alpha * carry[h][2] + _dot(vh, p)))
        return tuple(out)

    init = tuple((jnp.full((1, tq), NEG_INF, F32), jnp.zeros((1, tq), F32),
                  jnp.zeros((HEAD_DIM, tq), F32)) for _ in range(N_HEADS))
    carry = lax.fori_loop(0, qi, functools.partial(step, masked=False), init)
    carry = step(qi, carry, True)
    o_ref[0] = jnp.concatenate([acc / l for _, l, acc in carry], axis=0).T


def _fox_prompt_call(qe, ke, vtb, tq):
    b, _, _, seq = qe.shape
    return pl.pallas_call(
        functools.partial(_fox_prompt_body, tq=tq),
        grid=(b, seq // tq),
        in_specs=[pl.BlockSpec((1, N_HEADS, EXT, tq), lambda i, j: (i, 0, 0, j)),
                  pl.BlockSpec((1, N_HEADS, seq, EXT), lambda i, j: (i, 0, 0, 0)),
                  pl.BlockSpec((1, VD, seq), lambda i, j: (i, 0, 0))],
        out_specs=pl.BlockSpec((1, tq, VD), lambda i, j: (i, j, 0)),
        out_shape=jax.ShapeDtypeStruct((b, seq, VD), F32),
        compiler_params=_cparams("arbitrary", "arbitrary"),
        name="fox_prompt",
    )(qe, ke, vtb)


def _mix_sample_body(pa_ref, pbt_ref, pc_ref, bufa_ref, sg_ref, bufc_ref,
                     aw_ref, ab_ref, lg_ref, lb_ref, wa2t_ref, bac_ref, ngc_ref, cw_ref,
                     ya_ref, yb_ref, yc_ref, bufa_o, sg_o, bufc_o, q_s, k_s, dec_s):
    c = GROUP_W
    a = pa_ref[:, 0:c] * _sigmoid(pa_ref[:, c:])
    na = CONV_A_W - 1
    acc = aw_ref[na:na + 1, :] * a + ab_ref[...]
    for j in range(na):
        acc = acc + aw_ref[j:j + 1, :] * bufa_ref[0, j]
    mu = jnp.mean(acc, axis=-1, keepdims=True)
    var = jnp.mean(jnp.square(acc - mu), axis=-1, keepdims=True)
    ya_ref[...] = _silu((acc - mu) * lax.rsqrt(var + EPS) * lg_ref[...] + lb_ref[...])
    for j in range(na - 1):
        bufa_o[j] = bufa_ref[0, j + 1]
    bufa_o[na - 1] = a
    z = pc_ref[:, c:2 * c] * pc_ref[:, 2 * c:]
    conv = (cw_ref[0:1, :] * bufc_ref[0, :, 0:c] + cw_ref[1:2, :] * bufc_ref[0, :, c:]
            + cw_ref[2:3, :] * z)
    yc_ref[...] = pc_ref[:, 0:c] * conv
    bufc_o[:, 0:c] = bufc_ref[0, :, c:]
    bufc_o[:, c:] = z
    q_s[...] = pbt_ref[0:KD, :] * (DK_B ** -0.5)
    k_s[...] = pbt_ref[KD:2 * KD, :]
    gate = _dot(wa2t_ref[...], pbt_ref[2 * KD + 2 * VD:, :].astype(BF16)) + bac_ref[...]
    dec_s[...] = jnp.exp(_log_sigmoid(gate) * (1.0 / GLA_TAU))
    n = pa_ref.shape[0]
    ys = []
    for h in range(N_HEADS):
        vrows = slice(2 * KD + h * HEAD_DIM, 2 * KD + (h + 1) * HEAD_DIM)
        grows = slice(2 * KD + VD + h * HEAD_DIM, 2 * KD + VD + (h + 1) * HEAD_DIM)
        vh = pbt_ref[vrows, :]

        def key_step(kk, o, h=h, vh=vh):
            hk = h * DK_B + kk
            one = pl.ds(hk, 1)
            s_new = dec_s[one, :] * sg_ref[0, hk] + k_s[one, :] * vh
            sg_o[hk] = s_new
            return o + q_s[one, :] * s_new

        o = lax.fori_loop(0, DK_B, key_step, jnp.zeros((HEAD_DIM, n), F32))
        ms = jnp.mean(o * o, axis=0, keepdims=True)
        ng = ngc_ref[h * HEAD_DIM:(h + 1) * HEAD_DIM, :]
        ys.append(o * lax.rsqrt(ms + EPS) * ng * _silu(pbt_ref[grows, :]))
    yb_ref[...] = jnp.concatenate(ys, axis=0).T


def _layer_spec(shape, l):
    return pl.BlockSpec((1,) + tuple(shape[1:]), lambda i: (l,) + (0,) * (len(shape) - 1),
                        pipeline_mode=pl.Buffered(1))


def _mix_sample_call(l, pa, pbt, pc, bufa, sg, bufc, aw, ab, lg, lb, wa2t, bac, ngc, cw):
    n = pa.shape[0]
    c = GROUP_W
    small = (aw, ab, lg, lb, wa2t, bac, ngc, cw)
    out_shape = [jax.ShapeDtypeStruct((n, c), F32)] * 3 + [
        jax.ShapeDtypeStruct(bufa.shape[1:], F32), jax.ShapeDtypeStruct(sg.shape[1:], F32),
        jax.ShapeDtypeStruct(bufc.shape[1:], F32)]
    return pl.pallas_call(
        _mix_sample_body,
        grid=(1,),
        in_specs=[_full_spec(pa.shape), _full_spec(pbt.shape), _full_spec(pc.shape),
                  _layer_spec(bufa.shape, l), _layer_spec(sg.shape, l), _layer_spec(bufc.shape, l)]
                 + [_full_spec(x.shape) for x in small],
        out_specs=[pl.BlockSpec(s.shape, lambda i, nd=len(s.shape): (0,) * nd) for s in out_shape],
        out_shape=out_shape,
        scratch_shapes=[pltpu.VMEM((KD, n), F32)] * 3,
        compiler_params=_cparams("arbitrary"),
        name="mix_sample",
    )(pa, pbt, pc, bufa, sg, bufc, *small)


def _logf_pages_body(x_ref, o_ref):
    n = x_ref.shape[0]
    x = x_ref[...]
    later = (_iota((PAGE_SIZE, PAGE_SIZE), 0) > _iota((PAGE_SIZE, PAGE_SIZE), 1)).astype(BF16)
    ones = jnp.ones((PAGE_SIZE, PAGE_SIZE), BF16)
    within = _dot_sel(x, later)
    total = pltpu.roll(_dot_sel(x, ones), N_HEADS, 0)
    o_ref[...] = jnp.where(_imod(_iota((n, PAGE_SIZE), 0), SUBLANES) < N_HEADS, within, total)


def _logf_pages_call(x, tm):
    m = x.shape[0]
    return pl.pallas_call(
        _logf_pages_body,
        grid=(m // tm,),
        in_specs=[_row_spec(tm, PAGE_SIZE)],
        out_specs=_row_spec(tm, PAGE_SIZE),
        out_shape=jax.ShapeDtypeStruct((m, PAGE_SIZE), F32),
        compiler_params=_cparams("arbitrary"),
        name="logf_pages",
    )(x)


def _decode_sample(b, slot, q_ref, kn_ref, vn_ref, lfn_ref, kbuf, vbuf, lfbuf, n_pages):
    past = n_pages * PAGE_SIZE
    scale = HEAD_DIM ** -0.5
    hm = (_idiv(_iota((SUBLANES, VD), 1), HEAD_DIM) == _iota((SUBLANES, VD), 0)).astype(F32)
    qbd = jnp.broadcast_to(q_ref[pl.ds(b, 1), :], (SUBLANES, VD)) * hm
    qb = qbd.astype(BF16)
    s = jnp.concatenate([_dot(qb, kbuf[slot, p].astype(BF16)) for p in range(n_pages)],
                        axis=1) * scale
    lf = lfbuf[slot]
    incl = pltpu.roll(lf, N_HEADS, 0)
    sh = PAGE_SIZE
    while sh < past:
        incl = incl + jnp.concatenate([incl[:, sh:], jnp.zeros((SUBLANES, sh), F32)], axis=1)
        sh *= 2
    later_pages = jnp.concatenate([incl[:, PAGE_SIZE:], jnp.zeros((SUBLANES, PAGE_SIZE), F32)], axis=1)
    valid = _iota((SUBLANES, past), 0) < N_HEADS
    logits = jnp.where(valid, s + lf + later_pages, 0.0)
    s_self = jnp.sum(qbd * kn_ref[pl.ds(b, 1), :], axis=1, keepdims=True) * scale
    pick = (_iota((SUBLANES, LANES), 1) == _iota((SUBLANES, LANES), 0)).astype(F32)
    c_new = jnp.sum(pick * lfn_ref[pl.ds(b, 1), :], axis=1, keepdims=True)
    self_logit = jnp.where(_iota((SUBLANES, 1), 0) < N_HEADS, s_self - c_new, 0.0)
    m = jnp.maximum(jnp.max(logits, axis=1, keepdims=True), self_logit)
    p = jnp.exp(logits - m)
    p_self = jnp.exp(self_logit - m)
    l = jnp.sum(p, axis=1, keepdims=True) + p_self
    pb = p.astype(BF16)
    pv = jnp.zeros((SUBLANES, VD), F32)
    for pg in range(n_pages):
        pv = pv + _dot_nt(pb[:, pg * PAGE_SIZE:(pg + 1) * PAGE_SIZE], vbuf[slot, pg].astype(BF16))
    o8 = (pv + p_self * vn_ref[pl.ds(b, 1), :]) / l
    return jnp.sum(o8 * hm, axis=0, keepdims=True)


def _ffn_decode_body(pt_ref, x_ref, pre_ref, post_ref, wg_ref, wu_ref, wd_ref,
                     q_ref, kn_ref, vn_ref, lfn_ref, kc_hbm, vc_hbm, lfc_hbm,
                     o_ref, yd_ref, acc_ref, kbuf, vbuf, lfbuf, sem, *, fc, n_pages, layer, per_step):
    i = pl.program_id(0)
    n_steps = pl.num_programs(0)
    n_samples = q_ref.shape[0]

    def copies(sample, slot):
        out = []
        for p in range(n_pages):
            page = pt_ref[sample, p]
            cols = pl.ds(p * PAGE_SIZE, PAGE_SIZE)
            out.append(pltpu.make_async_copy(kc_hbm.at[layer, page], kbuf.at[slot, p], sem.at[slot, 0]))
            out.append(pltpu.make_async_copy(vc_hbm.at[layer, page], vbuf.at[slot, p], sem.at[slot, 1]))
            out.append(pltpu.make_async_copy(lfc_hbm.at[layer, page], lfbuf.at[slot, :, cols], sem.at[slot, 2]))
        return out

    @pl.when(i == 0)
    def _():
        for j in range(per_step):
            for cp in copies(j, j):
                cp.start()

    first = i * per_step
    for j in range(per_step):
        for cp in copies(first + j, j):
            cp.wait()
    for j in range(per_step):
        yd_ref[j] = _decode_sample(first + j, j, q_ref, kn_ref, vn_ref, lfn_ref, kbuf, vbuf, lfbuf, n_pages)
    for j in range(per_step):
        for cp in copies(jnp.minimum(first + per_step + j, n_samples - 1), j):
            cp.start()

    o_ref[...] = _swiglu_residual(x_ref[...], pre_ref[...], post_ref[...],
                                  wg_ref.at[0], wu_ref.at[0], wd_ref.at[0], acc_ref, fc)

    @pl.when(i == n_steps - 1)
    def _():
        for j in range(per_step):
            for cp in copies(n_samples - 1, j):
                cp.wait()


def _ffn_decode_call(layer, x, pre_g, post_g, wg, wu, wd, page_table, q, kn, vn, lfn, kc, vc, lfc, tm, fc=256):
    m, d = x.shape
    f = wg.shape[2]
    n, n_pages = page_table.shape
    steps = m // tm
    assert n % steps == 0
    per_step = n // steps
    past = n_pages * PAGE_SIZE
    const = lambda shape: pl.BlockSpec(shape, lambda i, pt: (0,) * len(shape), pipeline_mode=pl.Buffered(1))
    lyr = lambda shape: pl.BlockSpec((1,) + shape, lambda i, pt: (layer, 0, 0), pipeline_mode=pl.Buffered(1))
    anyspace = pl.BlockSpec(memory_space=pl.ANY)
    grid_spec = pltpu.PrefetchScalarGridSpec(
        num_scalar_prefetch=1,
        grid=(steps,),
        in_specs=[pl.BlockSpec((tm, d), lambda i, pt: (i, 0)), const((1, d)), const((1, d)),
                  lyr((d, f)), lyr((d, f)), lyr((f, d)),
                  const((n, VD)), const((n, VD)), const((n, VD)), const((n, LANES)),
                  anyspace, anyspace, anyspace],
        out_specs=[pl.BlockSpec((tm, d), lambda i, pt: (i, 0)),
                   pl.BlockSpec((per_step, 1, VD), lambda i, pt: (i, 0, 0))],
        scratch_shapes=[pltpu.VMEM((tm, d), F32),
                        pltpu.VMEM((per_step, n_pages, VD, PAGE_SIZE), F32),
                        pltpu.VMEM((per_step, n_pages, VD, PAGE_SIZE), F32),
                        pltpu.VMEM((per_step, SUBLANES, past), F32),
                        pltpu.SemaphoreType.DMA((per_step, 3))],
    )
    return pl.pallas_call(
        functools.partial(_ffn_decode_body, fc=fc, n_pages=n_pages, layer=layer, per_step=per_step),
        grid_spec=grid_spec,
        out_shape=[jax.ShapeDtypeStruct((m, d), F32), jax.ShapeDtypeStruct((n, 1, VD), F32)],
        compiler_params=_cparams("arbitrary"),
        name="ffn_decode",
    )(page_table, x, pre_g, post_g, wg, wu, wd, q, kn, vn, lfn, kc, vc, lfc)


ATTN_TILE = 256


def _row_tile(m, cap=512):
    tm = cap
    while tm >= SUBLANES:
        if m % tm == 0:
            return tm
        tm //= 2
    raise ValueError(f"row count {m} is not a multiple of {SUBLANES}")


def kernel(x_prompt, x_sample, state_conv_a, state_gla, state_conv_c, cache_k, cache_v, cache_logf, page_table, ffn1_pre_g, ffn1_post_g, ffn1_w_gate, ffn1_w_up, ffn1_w_down, mix_pre_g, mix_post_g, w_in, w_out, a_conv_w, a_conv_b, a_ln_g, a_ln_b, b_gate_w2, b_gate_b, b_out_norm_g, c_conv_w, d_forget_b, ffn2_pre_g, ffn2_post_g, ffn2_w_gate, ffn2_w_up, ffn2_w_down):
    depth = w_in.shape[0]
    bp, seq, d = x_prompt.shape
    bd = x_sample.shape[0]
    n_pool = cache_k.shape[1]
    assert x_sample.shape[1] == 1 and d == D_MODEL
    assert seq % ATTN_TILE == 0 and seq % GLA_CHUNK == 0 and seq >= CONV_A_W - 1

    hp = x_prompt.reshape(bp * seq, d)
    hs = x_sample.reshape(bd, d)
    tmp = _row_tile(bp * seq)
    tms = _row_tile(bd)
    tq = ATTN_TILE

    off_b = N_PA
    off_c = off_b + 2 * KD + 2 * VD + GLA_RANK
    off_d = off_c + N_PC
    row = lambda v: v.reshape(1, -1)

    lanes_of = lambda v, n=LANES: jnp.broadcast_to(v[:, None], (v.shape[0], n))

    ck = cache_k.transpose(0, 1, 3, 4, 2).reshape(depth, n_pool, VD, PAGE_SIZE)
    cv = cache_v.transpose(0, 1, 3, 4, 2).reshape(depth, n_pool, VD, PAGE_SIZE)
    sca = state_conv_a.transpose(0, 2, 1, 3)
    sgl = state_gla.transpose(0, 2, 3, 4, 1).reshape(depth, KD, HEAD_DIM, bd)
    scc = state_conv_c.reshape(depth, bd, (CONV_C_W - 1) * GROUP_W)
    w_in_t = w_in.transpose(0, 2, 1)

    lf_t = jnp.pad(cache_logf.transpose(0, 1, 3, 2), ((0, 0), (0, 0), (0, SUBLANES - N_HEADS), (0, 0)))
    lf_rows = depth * n_pool * SUBLANES
    lf_pages = _logf_pages_call(lf_t.reshape(lf_rows, PAGE_SIZE), _row_tile(lf_rows, cap=4096))
    lf_pages = lf_pages.reshape(depth, n_pool, SUBLANES, PAGE_SIZE)

    ffn1_ws = tuple(w.astype(BF16) for w in (ffn1_w_gate, ffn1_w_up, ffn1_w_down))
    ffn2_ws = tuple(w.astype(BF16) for w in (ffn2_w_gate, ffn2_w_up, ffn2_w_down))
    wo = w_out.astype(BF16)

    outs = [[] for _ in range(12)]
    for l in range(depth):
        wt = w_in_t[l].astype(BF16)
        pad_rows = lambda w, n: jnp.pad(w, ((0, n - w.shape[0]), (0, 0)))
        wq, wk, wv = (wt[off_d + i * VD:off_d + (i + 1) * VD] for i in range(3))
        in_w = (wt[0:off_b], pad_rows(wt[off_b:off_c], N_PB), wt[off_c:off_d],
                wq, wk, wv, pad_rows(wt[off_d + 3 * VD:], LANES))
        slabs = lambda w: jnp.pad(w.reshape(N_HEADS, HEAD_DIM, d),
                                  ((0, 0), (0, EXT - HEAD_DIM), (0, 0))).reshape(N_HEADS * EXT, d)
        in_w_prompt = in_w[0:3] + (wq, slabs(wk), wk, wv, in_w[6])
        bf_pad = jnp.pad(d_forget_b[l], (0, LANES - N_HEADS))
        bfr = row(bf_pad)
        bfc = lanes_of(bf_pad[0:SUBLANES])
        wa2 = jnp.pad(b_gate_w2[l], ((0, LANES - GLA_RANK), (0, 0))).astype(BF16)
        mix_w = (a_conv_w[l], row(a_conv_b[l]), row(a_ln_g[l]), row(a_ln_b[l]))
        ffn1_w = (row(ffn1_pre_g[l]), row(ffn1_post_g[l])) + ffn1_ws
        ffn2_w = (row(ffn2_pre_g[l]), row(ffn2_post_g[l])) + ffn2_ws

        hs = _ffn_call(l, hs, *ffn1_w, tm=tms)
        s_pa, s_pbt, s_pc, s_q, s_k, s_v, s_lf, s_kt, s_vt, s_lft = _inproj_sample_call(
            hs, row(mix_pre_g[l]), in_w, bfc, bfr)
        s_ya, s_yb, s_yc, s_bufa, s_state, s_bufc = _mix_sample_call(
            l, s_pa, s_pbt, s_pc, sca, sgl, scc, *mix_w, wa2.T, lanes_of(b_gate_b[l], bd),
            lanes_of(b_out_norm_g[l], bd), c_conv_w[l])

        hp, s_yd = _ffn_decode_call(l, hp, *ffn1_w, page_table, s_q, s_k, s_v, s_lf, ck, cv, lf_pages, tm=tmp)
        hs = _mix_ffn_call(l, hs, s_ya, s_yb, s_yc, s_yd.reshape(bd, VD), wo, row(mix_post_g[l]), *ffn2_w, tm=tms)
        for i, t in enumerate((s_bufa, s_state, s_bufc, s_kt, s_vt, s_lft[0:N_HEADS, :])):
            outs[6 + i].append(t)
        pa, pb, pc, kt, vt, lft, qe, ke, vtb = _inproj_prompt_call(
            hp, row(mix_pre_g[l]), in_w_prompt, bfc, bp, seq, tm=tmp)
        r3 = lambda t: t.reshape(bp, seq, t.shape[-1])
        ya, yc, buf_a, buf_c = _conv_prompt_call(r3(pa), r3(pc), *mix_w, c_conv_w[l])
        yb, s_b = _gla_prompt_call(r3(pb), wa2, row(b_gate_b[l]), row(b_out_norm_g[l]))
        yd = _fox_prompt_call(qe, ke, vtb, tq)
        f2 = lambda t: t.reshape(bp * seq, t.shape[-1])
        hp = _mix_ffn_call(l, hp, f2(ya), f2(yb), f2(yc), f2(yd), wo, row(mix_post_g[l]), *ffn2_w, tm=tmp)
        for i, t in enumerate((buf_a, s_b, buf_c, kt, vt, lft[:, 0:N_HEADS, :])):
            outs[i].append(t)

    p_ca, p_gla, p_cc, p_kt, p_vt, p_lft, s_ca, s_gla, s_cc, s_kt, s_vt, s_lft = (jnp.stack(o) for o in outs)
    heads = lambda t: t.reshape(t.shape[:-2] + (N_HEADS, HEAD_DIM, t.shape[-1]))
    return (hp.reshape(bp, seq, d), hs.reshape(bd, 1, d),
            p_ca, p_gla, p_cc,
            heads(p_kt).transpose(0, 1, 4, 2, 3), heads(p_vt).transpose(0, 1, 4, 2, 3),
            p_lft.transpose(0, 1, 3, 2),
            s_ca.transpose(0, 2, 1, 3),
            s_gla.reshape(depth, N_HEADS, DK_B, HEAD_DIM, bd).transpose(0, 4, 1, 2, 3),
            s_cc.reshape(depth, bd, CONV_C_W - 1, GROUP_W),
            heads(s_kt).transpose(0, 3, 1, 2)[:, :, None], heads(s_vt).transpose(0, 3, 1, 2)[:, :, None],
            s_lft.transpose(0, 2, 1)[:, :, None])
```

```python
import functools

import jax
import jax.numpy as jnp
from jax import lax
from jax.experimental import pallas as pl
from jax.experimental.pallas import tpu as pltpu

F32 = jnp.float32
BF16 = jnp.bfloat16

D_MODEL = 1024
GROUP_W = D_MODEL // 4
HEAD_DIM = 64
N_HEADS = GROUP_W // HEAD_DIM
DK_B = HEAD_DIM // 2
GLA_RANK = 16
GLA_TAU = 16.0
GLA_CHUNK = 64
CONV_A_W = 31
CONV_C_W = 3
PAGE_SIZE = 128
EPS = 1e-6
NEG_INF = -1e30

LANES = 128
SUBLANES = 8
VMEM_LIMIT_BYTES = 56 * 1024 * 1024

N_PA = 2 * GROUP_W
KD = N_HEADS * DK_B
VD = N_HEADS * HEAD_DIM
N_PB = 2 * KD + 2 * VD + LANES
N_PC = 3 * GROUP_W


def _cparams(*sem):
    return pltpu.CompilerParams(dimension_semantics=sem, vmem_limit_bytes=VMEM_LIMIT_BYTES)


def _dot(a, b):
    return jnp.dot(a, b, preferred_element_type=F32)


def _dot_nt(a, b):
    return lax.dot_general(a, b, (((1,), (1,)), ((), ())), preferred_element_type=F32)


def _split3(x):
    hi = x.astype(BF16)
    r = x - hi.astype(F32)
    mid = r.astype(BF16)
    lo = (r - mid.astype(F32)).astype(BF16)
    return hi, mid, lo


def _dot_sel(x, sel):
    hi, mid, lo = _split3(x)
    return _dot(hi, sel) + _dot(mid, sel) + _dot(lo, sel)


def _sel_dot(sel, x):
    hi, mid, lo = _split3(x)
    return _dot(sel, hi) + _dot(sel, mid) + _dot(sel, lo)


def _rms(x, g):
    return x * lax.rsqrt(jnp.mean(x * x, axis=-1, keepdims=True) + EPS) * g


def _sigmoid(x):
    return 1.0 / (1.0 + jnp.exp(-x))


def _silu(x):
    return x * _sigmoid(x)


def _log_sigmoid(x):
    return jnp.minimum(x, 0.0) - jnp.log1p(jnp.exp(-jnp.abs(x)))


def _iota(shape, dim):
    return lax.broadcasted_iota(jnp.int32, shape, dim)


def _idiv(x, n):
    assert n & (n - 1) == 0
    return lax.shift_right_logical(x, n.bit_length() - 1)


def _imod(x, n):
    assert n & (n - 1) == 0
    return x & (n - 1)


def _head_mean_matrix():
    r = _idiv(_iota((VD, VD), 0), HEAD_DIM)
    c = _idiv(_iota((VD, VD), 1), HEAD_DIM)
    return (r == c).astype(BF16)


def _swiglu_residual(x, pre_g, post_g, wg_ref, wu_ref, wd_ref, acc_ref, fc):
    xn = _rms(x, pre_g).astype(BF16)
    for c in range(wg_ref.shape[1] // fc):
        sl = slice(c * fc, (c + 1) * fc)
        g = _dot(xn, wg_ref[:, sl])
        u = _dot(xn, wu_ref[:, sl])
        hid = (_silu(g) * u).astype(BF16)
        part = _dot(hid, wd_ref[sl, :])
        if c == 0:
            acc_ref[...] = part
        else:
            acc_ref[...] += part
    return x + 0.5 * _rms(acc_ref[...], post_g)


def _ffn_body(x_ref, pre_ref, post_ref, wg_ref, wu_ref, wd_ref, o_ref, acc_ref, *, fc):
    o_ref[...] = _swiglu_residual(x_ref[...], pre_ref[...], post_ref[...],
                                  wg_ref.at[0], wu_ref.at[0], wd_ref.at[0], acc_ref, fc)


def _mix_ffn_body(h_ref, ya_ref, yb_ref, yc_ref, yd_ref, wo_ref, mpost_ref,
                  pre_ref, post_ref, wg_ref, wu_ref, wd_ref, o_ref, acc_ref, *, fc):
    ycat = jnp.concatenate([ya_ref[...], yb_ref[...], yc_ref[...], yd_ref[...]], axis=1)
    y = _dot(ycat.astype(BF16), wo_ref[0])
    h = h_ref[...] + _rms(y, mpost_ref[...])
    o_ref[...] = _swiglu_residual(h, pre_ref[...], post_ref[...],
                                  wg_ref.at[0], wu_ref.at[0], wd_ref.at[0], acc_ref, fc)


def _row_spec(tm, n):
    return pl.BlockSpec((tm, n), lambda i: (i, 0))


def _full_spec(shape):
    return pl.BlockSpec(shape, lambda i: (0,) * len(shape), pipeline_mode=pl.Buffered(1))


def _ffn_call(layer, x, pre_g, post_g, wg, wu, wd, tm, fc=256):
    m, d = x.shape
    return pl.pallas_call(
        functools.partial(_ffn_body, fc=fc),
        grid=(m // tm,),
        in_specs=[_row_spec(tm, d), _full_spec((1, d)), _full_spec((1, d)),
                  _layer_spec(wg.shape, layer), _layer_spec(wu.shape, layer), _layer_spec(wd.shape, layer)],
        out_specs=_row_spec(tm, d),
        out_shape=jax.ShapeDtypeStruct((m, d), F32),
        scratch_shapes=[pltpu.VMEM((tm, d), F32)],
        compiler_params=_cparams("arbitrary"),
        name="ffn",
    )(x, pre_g, post_g, wg, wu, wd)


def _mix_ffn_call(layer, h, ya, yb, yc, yd, wo, mpost, pre_g, post_g, wg, wu, wd, tm, fc=256):
    m, d = h.shape
    return pl.pallas_call(
        functools.partial(_mix_ffn_body, fc=fc),
        grid=(m // tm,),
        in_specs=[_row_spec(tm, d)] + [_row_spec(tm, GROUP_W)] * 4
                 + [_layer_spec(wo.shape, layer), _full_spec((1, d)), _full_spec((1, d)), _full_spec((1, d)),
                    _layer_spec(wg.shape, layer), _layer_spec(wu.shape, layer), _layer_spec(wd.shape, layer)],
        out_specs=_row_spec(tm, d),
        out_shape=jax.ShapeDtypeStruct((m, d), F32),
        scratch_shapes=[pltpu.VMEM((tm, d), F32)],
        compiler_params=_cparams("arbitrary"),
        name="mix_ffn",
    )(h, ya, yb, yc, yd, wo, mpost, pre_g, post_g, wg, wu, wd)


EXT = LANES
C_KEY = HEAD_DIM
C_QRY = HEAD_DIM + 3


def _inproj_prompt_body(h_ref, g_ref, wa_ref, wb_ref, wc_ref, wq_ref, wke_ref, wk_ref, wv_ref, wl_ref,
                        bfc_ref, pa_ref, pb_ref, pc_ref, kt_ref, vt_ref, lft_ref,
                        qe_ref, ke_ref, vtb_ref, carry_ref, *, per):
    tm = h_ref.shape[0]
    u = _rms(h_ref[...], g_ref[...]).astype(BF16)
    pa_ref[...] = _dot_nt(u, wa_ref[...])
    pb_ref[...] = _dot_nt(u, wb_ref[...])
    pc_ref[...] = _dot_nt(u, wc_ref[...])
    kt_ref[0] = _dot_nt(wk_ref[...], u)
    vt = _dot_nt(wv_ref[...], u)
    vt_ref[0] = vt
    vtb_ref[0] = vt.astype(BF16)
    lft = _log_sigmoid(_dot_nt(wl_ref[0:SUBLANES, :], u) + bfc_ref[:, 0:1])
    lft_ref[0] = lft

    first = pl.program_id(0) % per == 0
    carry = jnp.where(first, 0.0, carry_ref[:, 0:1])
    upto = (_iota((tm, tm), 0) <= _iota((tm, tm), 1)).astype(BF16)
    lfm = jnp.where(_iota((SUBLANES, tm), 0) < N_HEADS, lft, 0.0)
    c = _dot_sel(lfm, upto) + carry
    carry_ref[...] = jnp.broadcast_to(c[:, tm - 1:tm], carry_ref.shape)
    ccol = jnp.concatenate([c, jnp.zeros((LANES - SUBLANES, tm), F32)], axis=0).T

    hi, mid, lo = _split3(ccol)
    pieces = (hi.astype(F32) + pltpu.roll(mid.astype(F32), N_HEADS, 1)
              + pltpu.roll(lo.astype(F32), 2 * N_HEADS, 1)).astype(BF16)
    r = _iota((LANES, N_HEADS * EXT), 0)
    col = _iota((LANES, N_HEADS * EXT), 1)
    j = _imod(col, EXT) - C_KEY
    place_k = ((j >= 0) & (j < 3) & (r == N_HEADS * j + _idiv(col, EXT))).astype(BF16)
    lane = _imod(_iota((1, N_HEADS * EXT), 1), EXT)
    ones_k = ((lane >= C_QRY) & (lane < C_QRY + 3)).astype(F32)
    ke = _dot_nt(u, wke_ref[...]) + _dot(pieces, place_k) + ones_k
    for h in range(N_HEADS):
        ke_ref[0, h] = ke[:, h * EXT:(h + 1) * EXT].astype(BF16)

    hi, mid, lo = _split3(c)
    pieces_t = jnp.concatenate([hi.astype(F32), mid.astype(F32), lo.astype(F32),
                                jnp.zeros((LANES - 3 * SUBLANES, tm), F32)], axis=0).astype(BF16)
    r = _iota((VD, LANES), 0)
    col = _iota((VD, LANES), 1)
    j = _imod(r, HEAD_DIM) - (C_QRY - HEAD_DIM)
    place_q = ((j >= 0) & (j < 3) & (col == SUBLANES * j + _idiv(r, HEAD_DIM))).astype(BF16)
    rowi = _imod(_iota((VD, 1), 0), HEAD_DIM)
    neg_q = jnp.where(rowi < 3, -1.0, 0.0)
    extra = (_dot(place_q, pieces_t) + neg_q).astype(BF16)
    qt = (_dot_nt(wq_ref[...], u) * (HEAD_DIM ** -0.5)).astype(BF16)
    for h in range(N_HEADS):
        rows = slice(h * HEAD_DIM, (h + 1) * HEAD_DIM)
        qe_ref[0, h, 0:HEAD_DIM, :] = qt[rows, :]
        qe_ref[0, h, HEAD_DIM:, :] = extra[rows, :]


def _inproj_sample_body(h_ref, g_ref, wa_ref, wb_ref, wc_ref, wq_ref, wk_ref, wv_ref, wl_ref,
                        bfc_ref, bfr_ref,
                        pa_ref, pbt_ref, pc_ref, q_ref, k_ref, v_ref, lf_ref, kt_ref, vt_ref, lft_ref):
    u = _rms(h_ref[...], g_ref[...]).astype(BF16)
    pa_ref[...] = _dot_nt(u, wa_ref[...])
    pbt_ref[...] = _dot_nt(wb_ref[...], u)
    pc_ref[...] = _dot_nt(u, wc_ref[...])
    q_ref[...] = _dot_nt(u, wq_ref[...])
    k_ref[...] = _dot_nt(u, wk_ref[...])
    v_ref[...] = _dot_nt(u, wv_ref[...])
    lf_ref[...] = _log_sigmoid(_dot_nt(u, wl_ref[...]) + bfr_ref[...])
    kt_ref[...] = _dot_nt(wk_ref[...], u)
    vt_ref[...] = _dot_nt(wv_ref[...], u)
    lft_ref[...] = _log_sigmoid(_dot_nt(wl_ref[0:SUBLANES, :], u) + bfc_ref[:, 0:1])


def _inproj_weight_specs(d):
    return [_full_spec((n, d)) for n in (N_PA, N_PB, N_PC, VD, VD, VD, LANES)]


def _inproj_prompt_call(h, g, ws, bfc, bp, seq, tm):
    m, d = h.shape
    per = seq // tm
    tspec = lambda n: pl.BlockSpec((1, n, tm), lambda i: (i // per, 0, i % per))
    hx = N_HEADS * EXT
    return pl.pallas_call(
        functools.partial(_inproj_prompt_body, per=per),
        grid=(m // tm,),
        in_specs=[_row_spec(tm, d), _full_spec((1, d))]
                 + [_full_spec((n, d)) for n in (N_PA, N_PB, N_PC, VD, hx, VD, VD, LANES)]
                 + [_full_spec((SUBLANES, LANES))],
        out_specs=[_row_spec(tm, N_PA), _row_spec(tm, N_PB), _row_spec(tm, N_PC),
                   tspec(VD), tspec(VD), tspec(SUBLANES),
                   pl.BlockSpec((1, N_HEADS, EXT, tm), lambda i: (i // per, 0, 0, i % per)),
                   pl.BlockSpec((1, N_HEADS, tm, EXT), lambda i: (i // per, 0, i % per, 0)),
                   tspec(VD)],
        out_shape=[jax.ShapeDtypeStruct((m, N_PA), F32), jax.ShapeDtypeStruct((m, N_PB), F32),
                   jax.ShapeDtypeStruct((m, N_PC), F32),
                   jax.ShapeDtypeStruct((bp, VD, seq), F32), jax.ShapeDtypeStruct((bp, VD, seq), F32),
                   jax.ShapeDtypeStruct((bp, SUBLANES, seq), F32),
                   jax.ShapeDtypeStruct((bp, N_HEADS, EXT, seq), BF16),
                   jax.ShapeDtypeStruct((bp, N_HEADS, seq, EXT), BF16),
                   jax.ShapeDtypeStruct((bp, VD, seq), BF16)],
        scratch_shapes=[pltpu.VMEM((SUBLANES, LANES), F32)],
        compiler_params=_cparams("arbitrary"),
        name="inproj_prompt",
    )(h, g, *ws, bfc)


def _inproj_sample_call(h, g, ws, bfc, bfr):
    n, d = h.shape
    shapes = [(n, N_PA), (N_PB, n), (n, N_PC), (n, VD), (n, VD), (n, VD), (n, LANES),
              (VD, n), (VD, n), (SUBLANES, n)]
    return pl.pallas_call(
        _inproj_sample_body,
        grid=(1,),
        in_specs=[_full_spec((n, d)), _full_spec((1, d))] + _inproj_weight_specs(d)
                 + [_full_spec((SUBLANES, LANES)), _full_spec((1, LANES))],
        out_specs=[pl.BlockSpec(s, lambda i: (0, 0)) for s in shapes],
        out_shape=[jax.ShapeDtypeStruct(s, F32) for s in shapes],
        compiler_params=_cparams("arbitrary"),
        name="inproj_sample",
    )(h, g, *ws, bfc, bfr)


A_PAD = 32
C_PAD = 8
CONV_ROWS = 128


def _conv_prompt_body(pa_ref, pc_ref, aw_ref, ab_ref, lg_ref, lb_ref, cw_ref,
                      ya_ref, yc_ref, bufa_ref, bufc_ref, apad_ref, zpad_ref):
    seq = pa_ref.shape[1]
    c = GROUP_W
    apad_ref[0:A_PAD, :] = jnp.zeros((A_PAD, c), F32)
    apad_ref[A_PAD:, :] = pa_ref[0, :, 0:c] * _sigmoid(pa_ref[0, :, c:])
    zpad_ref[0:C_PAD, :] = jnp.zeros((C_PAD, c), F32)
    zpad_ref[C_PAD:, :] = pc_ref[0, :, c:2 * c] * pc_ref[0, :, 2 * c:]

    def step(i, carry):
        r0 = pl.multiple_of(i * CONV_ROWS, CONV_ROWS)
        win = apad_ref[pl.ds(r0, CONV_ROWS + A_PAD), :]
        acc = jnp.zeros((CONV_ROWS, c), F32) + ab_ref[...]
        for r in range(SUBLANES):
            nrow = CONV_ROWS if r == 0 else CONV_ROWS + SUBLANES
            u = None
            for a8 in range(0, A_PAD + 1, SUBLANES):
                j = a8 + r - (A_PAD - (CONV_A_W - 1))
                if 0 <= j < CONV_A_W:
                    term = aw_ref[j:j + 1, :] * win[a8:a8 + nrow, :]
                    u = term if u is None else u + term
            acc = acc + u[r:r + CONV_ROWS, :]
        mu = jnp.mean(acc, axis=-1, keepdims=True)
        var = jnp.mean(jnp.square(acc - mu), axis=-1, keepdims=True)
        yn = (acc - mu) * lax.rsqrt(var + EPS) * lg_ref[...] + lb_ref[...]
        ya_ref[0, pl.ds(r0, CONV_ROWS), :] = _silu(yn)
        zwin = zpad_ref[pl.ds(r0, CONV_ROWS + C_PAD), :]
        accc = jnp.zeros((CONV_ROWS, c), F32)
        for j in range(CONV_C_W):
            off = C_PAD - (CONV_C_W - 1) + j
            accc = accc + cw_ref[j:j + 1, :] * zwin[off:off + CONV_ROWS, :]
        yc_ref[0, pl.ds(r0, CONV_ROWS), :] = pc_ref[0, pl.ds(r0, CONV_ROWS), 0:c] * accc
        return carry

    lax.fori_loop(0, seq // CONV_ROWS, step, 0)
    na = CONV_A_W - 1
    nc = CONV_C_W - 1
    bufa_ref[0] = apad_ref[seq:A_PAD + seq, :][A_PAD - na:, :]
    bufc_ref[0] = zpad_ref[seq:C_PAD + seq, :][C_PAD - nc:, :]


def _conv_prompt_call(pa, pc, aw, ab, lg, lb, cw):
    b, seq, _ = pa.shape
    c = GROUP_W
    bspec = lambda n: pl.BlockSpec((1, seq, n), lambda i: (i, 0, 0))
    return pl.pallas_call(
        _conv_prompt_body,
        grid=(b,),
        in_specs=[bspec(N_PA), bspec(N_PC), _full_spec((CONV_A_W, c)), _full_spec((1, c)),
                  _full_spec((1, c)), _full_spec((1, c)), _full_spec((CONV_C_W, c))],
        out_specs=[bspec(c), bspec(c),
                   pl.BlockSpec((1, CONV_A_W - 1, c), lambda i: (i, 0, 0)),
                   pl.BlockSpec((1, CONV_C_W - 1, c), lambda i: (i, 0, 0))],
        out_shape=[jax.ShapeDtypeStruct((b, seq, c), F32), jax.ShapeDtypeStruct((b, seq, c), F32),
                   jax.ShapeDtypeStruct((b, CONV_A_W - 1, c), F32),
                   jax.ShapeDtypeStruct((b, CONV_C_W - 1, c), F32)],
        scratch_shapes=[pltpu.VMEM((A_PAD + seq, c), F32), pltpu.VMEM((C_PAD + seq, c), F32)],
        compiler_params=_cparams("arbitrary"),
        name="conv_prompt",
    )(pa, pc, aw, ab, lg, lb, cw)


def _gla_gate(alr, wa2_ref, ba_ref):
    return _log_sigmoid(_dot(alr.astype(BF16), wa2_ref[...]) + ba_ref[...]) * (1.0 / GLA_TAU)


def _gla_out(o, g, ng, hm):
    ms = _dot_sel(o * o, hm) * (1.0 / HEAD_DIM)
    return o * lax.rsqrt(ms + EPS) * ng * _silu(g)


GLA_SLAB = 512


def _group_row(x, g, r):
    n, c = x.shape
    x3 = x.reshape(n // g, g, c)
    return jnp.broadcast_to(x3[:, r:r + 1, :], (n // g, g, c)).reshape(n, c)


def _gla_prompt_body(pb_ref, wa2_ref, ba_ref, ng_ref, y_ref, st_ref, la_ref, s_ref):
    seq = pb_ref.shape[1]
    ck = GLA_CHUNK
    la_ref[...] = _gla_gate(pb_ref[0, :, 2 * KD + 2 * VD:], wa2_ref, ba_ref)
    s_ref[...] = jnp.zeros((KD, VD), F32)

    rs = min(GLA_SLAB, seq)
    nch = rs // ck
    tt = _iota((rs, rs), 0)
    ss = _iota((rs, rs), 1)
    tri = ((ss <= tt) & (_idiv(ss, ck) == _idiv(tt, ck))).astype(BF16)
    blk = _idiv(_iota((KD, VD), 0), DK_B) == _idiv(_iota((KD, VD), 1), HEAD_DIM)
    blk_bf = blk.astype(BF16)
    blk_f = blk.astype(F32)
    hm = _head_mean_matrix()
    scale = DK_B ** -0.5
    levels = [g for g in (2 * SUBLANES, 4 * SUBLANES, 8 * SUBLANES) if g <= ck]
    assert ck == 8 * SUBLANES
    key_head = _idiv(_iota((1, KD), 1), DK_B)
    val_head = _idiv(_iota((1, VD), 1), HEAD_DIM)
    tloc = _iota((rs, KD), 0)

    def slab(si, carry):
        r0 = pl.multiple_of(si * rs, rs)
        rows = pl.ds(r0, rs)
        q = pb_ref[0, rows, 0:KD] * scale
        k = pb_ref[0, rows, KD:2 * KD]
        v = pb_ref[0, rows, 2 * KD:2 * KD + VD]
        g = pb_ref[0, rows, 2 * KD + VD:2 * KD + 2 * VD]
        b = _sel_dot(tri, la_ref[rows, :])
        vb3 = v.astype(BF16).reshape(nch, ck, VD)

        t8 = _imod(tloc, SUBLANES)
        acc = jnp.zeros((rs, VD), F32)
        for s in range(SUBLANES):
            diff = jnp.where(t8 >= s, b - _group_row(b, SUBLANES, s), -jnp.inf)
            e = (q * _group_row(k, SUBLANES, s) * jnp.exp(diff)).astype(BF16)
            acc = acc + _dot(e, blk_bf) * _group_row(v, SUBLANES, s)

        sc = jnp.zeros((nch, N_HEADS * ck, ck), F32)
        for gsz in levels:
            r = _group_row(b, gsz, gsz // 2)
            upper = _imod(tloc, gsz) >= gsz // 2
            qf = q * jnp.exp(jnp.where(upper, b - r, -jnp.inf))
            kf = (k * jnp.exp(jnp.where(upper, -jnp.inf, r - b))).astype(BF16).reshape(nch, ck, KD)
            qstack = jnp.concatenate(
                [jnp.where(key_head == h, qf, 0.0).astype(BF16).reshape(nch, ck, KD)
                 for h in range(N_HEADS)], axis=1)
            sc_g = jnp.einsum('cqk,csk->cqs', qstack, kf, preferred_element_type=F32)
            same_group = _idiv(_imod(_iota((N_HEADS * ck, ck), 0), ck), gsz) == _idiv(_iota((N_HEADS * ck, ck), 1), gsz)
            sc = sc + jnp.where(same_group[None], sc_g, 0.0)
        res = jnp.einsum('cqs,csv->cqv', sc.astype(BF16), vb3, preferred_element_type=F32)
        for h in range(N_HEADS):
            part = res[:, h * ck:(h + 1) * ck, :].reshape(rs, VD)
            acc = acc + jnp.where(val_head == h, part, 0.0)

        b_last = _group_row(b, ck, ck - 1)
        kdec = k * jnp.exp(b_last - b)
        lastb = jnp.concatenate([b[(c + 1) * ck - 1:(c + 1) * ck, :] for c in range(nch)]
                                + [jnp.zeros((LANES - nch, KD), F32)], axis=0)
        dcols = jnp.exp(lastb.T)
        s_cur = s_ref[...]
        s_before = []
        for c in range(nch):
            s_before.append(s_cur.astype(BF16))
            crow = slice(c * ck, (c + 1) * ck)
            upd = _dot(kdec[crow, :].T.astype(BF16), vb3[c])
            s_cur = dcols[:, c:c + 1] * s_cur + upd * blk_f
        s_ref[...] = s_cur
        qb3 = (q * jnp.exp(b)).astype(BF16).reshape(nch, ck, KD)
        inter = jnp.einsum('cqk,ckv->cqv', qb3, jnp.stack(s_before), preferred_element_type=F32)
        o = inter.reshape(rs, VD) + acc
        y_ref[0, rows, :] = _gla_out(o, g, ng_ref[...], hm)
        return carry

    lax.fori_loop(0, seq // rs, slab, 0)
    for h in range(N_HEADS):
        st_ref[0, h] = s_ref[h * DK_B:(h + 1) * DK_B, h * HEAD_DIM:(h + 1) * HEAD_DIM]


def _gla_prompt_call(pb, wa2, ba, ng):
    b, seq, _ = pb.shape
    return pl.pallas_call(
        _gla_prompt_body,
        grid=(b,),
        in_specs=[pl.BlockSpec((1, seq, N_PB), lambda i: (i, 0, 0)), _full_spec((LANES, KD)),
                  _full_spec((1, KD)), _full_spec((1, VD))],
        out_specs=[pl.BlockSpec((1, seq, VD), lambda i: (i, 0, 0)),
                   pl.BlockSpec((1, N_HEADS, DK_B, HEAD_DIM), lambda i: (i, 0, 0, 0))],
        out_shape=[jax.ShapeDtypeStruct((b, seq, VD), F32),
                   jax.ShapeDtypeStruct((b, N_HEADS, DK_B, HEAD_DIM), F32)],
        scratch_shapes=[pltpu.VMEM((seq, KD), F32), pltpu.VMEM((KD, VD), F32)],
        compiler_params=_cparams("arbitrary"),
        name="gla_prompt",
    )(pb, wa2, ba, ng)


def _fox_prompt_body(qe_ref, ke_ref, vtb_ref, o_ref, *, tq):
    qi = pl.program_id(1)
    key_pos = _iota((tq, tq), 0)
    qry_pos = _iota((tq, tq), 1)

    def step(ki, carry, masked):
        ks = pl.ds(pl.multiple_of(ki * tq, tq), tq)
        scores = [_dot(ke_ref[0, h, ks, :], qe_ref[0, h]) for h in range(N_HEADS)]
        soft = []
        for h in range(N_HEADS):
            m, l, _ = carry[h]
            s = scores[h]
            if masked:
                s = jnp.where(key_pos <= qry_pos, s, NEG_INF)
            m_new = jnp.maximum(m, jnp.max(s, axis=0, keepdims=True))
            alpha = jnp.exp(m - m_new)
            p = jnp.exp(s - m_new)
            soft.append((m_new, alpha * l + jnp.sum(p, axis=0, keepdims=True), alpha, p.astype(BF16)))
        out = []
        for h in range(N_HEADS):
            m_new, l, alpha, p = soft[h]
            vh = vtb_ref[0, h * HEAD_DIM:(h + 1) * HEAD_DIM, ks]
            out.append((m_new, l, alpha * carry[h][2] + _dot(vh, p)))
        return tuple(out)

    init = tuple((jnp.full((1, tq), NEG_INF, F32), jnp.zeros((1, tq), F32),
                  jnp.zeros((HEAD_DIM, tq), F32)) for _ in range(N_HEADS))
    carry = lax.fori_loop(0, qi, functools.partial(step, masked=False), init)
    carry = step(qi, carry, True)
    o_ref[0] = jnp.concatenate([acc / l for _, l, acc in carry], axis=0).T


def _fox_prompt_call(qe, ke, vtb, tq):
    b, _, _, seq = qe.shape
    return pl.pallas_call(
        functools.partial(_fox_prompt_body, tq=tq),
        grid=(b, seq // tq),
        in_specs=[pl.BlockSpec((1, N_HEADS, EXT, tq), lambda i, j: (i, 0, 0, j)),
                  pl.BlockSpec((1, N_HEADS, seq, EXT), lambda i, j: (i, 0, 0, 0)),
                  pl.BlockSpec((1, VD, seq), lambda i, j: (i, 0, 0))],
        out_specs=pl.BlockSpec((1, tq, VD), lambda i, j: (i, j, 0)),
        out_shape=jax.ShapeDtypeStruct((b, seq, VD), F32),
        compiler_params=_cparams("arbitrary", "arbitrary"),
        name="fox_prompt",
    )(qe, ke, vtb)


def _mix_sample_body(pa_ref, pbt_ref, pc_ref, bufa_ref, sg_ref, bufc_ref,
                     aw_ref, ab_ref, lg_ref, lb_ref, wa2t_ref, bac_ref, ngc_ref, cw_ref,
                     ya_ref, yb_ref, yc_ref, bufa_o, sg_o, bufc_o, q_s, k_s, dec_s):
    c = GROUP_W
    a = pa_ref[:, 0:c] * _sigmoid(pa_ref[:, c:])
    na = CONV_A_W - 1
    acc = aw_ref[na:na + 1, :] * a + ab_ref[...]
    for j in range(na):
        acc = acc + aw_ref[j:j + 1, :] * bufa_ref[0, j]
    mu = jnp.mean(acc, axis=-1, keepdims=True)
    var = jnp.mean(jnp.square(acc - mu), axis=-1, keepdims=True)
    ya_ref[...] = _silu((acc - mu) * lax.rsqrt(var + EPS) * lg_ref[...] + lb_ref[...])
    for j in range(na - 1):
        bufa_o[j] = bufa_ref[0, j + 1]
    bufa_o[na - 1] = a
    z = pc_ref[:, c:2 * c] * pc_ref[:, 2 * c:]
    conv = (cw_ref[0:1, :] * bufc_ref[0, :, 0:c] + cw_ref[1:2, :] * bufc_ref[0, :, c:]
            + cw_ref[2:3, :] * z)
    yc_ref[...] = pc_ref[:, 0:c] * conv
    bufc_o[:, 0:c] = bufc_ref[0, :, c:]
    bufc_o[:, c:] = z
    q_s[...] = pbt_ref[0:KD, :] * (DK_B ** -0.5)
    k_s[...] = pbt_ref[KD:2 * KD, :]
    gate = _dot(wa2t_ref[...], pbt_ref[2 * KD + 2 * VD:, :].astype(BF16)) + bac_ref[...]
    dec_s[...] = jnp.exp(_log_sigmoid(gate) * (1.0 / GLA_TAU))
    n = pa_ref.shape[0]
    ys = []
    for h in range(N_HEADS):
        vrows = slice(2 * KD + h * HEAD_DIM, 2 * KD + (h + 1) * HEAD_DIM)
        grows = slice(2 * KD + VD + h * HEAD_DIM, 2 * KD + VD + (h + 1) * HEAD_DIM)
        vh = pbt_ref[vrows, :]

        def key_step(kk, o, h=h, vh=vh):
            hk = h * DK_B + kk
            one = pl.ds(hk, 1)
            s_new = dec_s[one, :] * sg_ref[0, hk] + k_s[one, :] * vh
            sg_o[hk] = s_new
            return o + q_s[one, :] * s_new

        o = lax.fori_loop(0, DK_B, key_step, jnp.zeros((HEAD_DIM, n), F32))
        ms = jnp.mean(o * o, axis=0, keepdims=True)
        ng = ngc_ref[h * HEAD_DIM:(h + 1) * HEAD_DIM, :]
        ys.append(o * lax.rsqrt(ms + EPS) * ng * _silu(pbt_ref[grows, :]))
    yb_ref[...] = jnp.concatenate(ys, axis=0).T


def _layer_spec(shape, l):
    return pl.BlockSpec((1,) + tuple(shape[1:]), lambda i: (l,) + (0,) * (len(shape) - 1),
                        pipeline_mode=pl.Buffered(1))


def _mix_sample_call(l, pa, pbt, pc, bufa, sg, bufc, aw, ab, lg, lb, wa2t, bac, ngc, cw):
    n = pa.shape[0]
    c = GROUP_W
    small = (aw, ab, lg, lb, wa2t, bac, ngc, cw)
    out_shape = [jax.ShapeDtypeStruct((n, c), F32)] * 3 + [
        jax.ShapeDtypeStruct(bufa.shape[1:], F32), jax.ShapeDtypeStruct(sg.shape[1:], F32),
        jax.ShapeDtypeStruct(bufc.shape[1:], F32)]
    return pl.pallas_call(
        _mix_sample_body,
        grid=(1,),
        in_specs=[_full_spec(pa.shape), _full_spec(pbt.shape), _full_spec(pc.shape),
                  _layer_spec(bufa.shape, l), _layer_spec(sg.shape, l), _layer_spec(bufc.shape, l)]
                 + [_full_spec(x.shape) for x in small],
        out_specs=[pl.BlockSpec(s.shape, lambda i, nd=len(s.shape): (0,) * nd) for s in out_shape],
        out_shape=out_shape,
        scratch_shapes=[pltpu.VMEM((KD, n), F32)] * 3,
        compiler_params=_cparams("arbitrary"),
        name="mix_sample",
    )(pa, pbt, pc, bufa, sg, bufc, *small)


def _logf_pages_body(x_ref, o_ref):
    n = x_ref.shape[0]
    x = x_ref[...]
    later = (_iota((PAGE_SIZE, PAGE_SIZE), 0) > _iota((PAGE_SIZE, PAGE_SIZE), 1)).astype(BF16)
    ones = jnp.ones((PAGE_SIZE, PAGE_SIZE), BF16)
    within = _dot_sel(x, later)
    total = pltpu.roll(_dot_sel(x, ones), N_HEADS, 0)
    o_ref[...] = jnp.where(_imod(_iota((n, PAGE_SIZE), 0), SUBLANES) < N_HEADS, within, total)


def _logf_pages_call(x, tm):
    m = x.shape[0]
    return pl.pallas_call(
        _logf_pages_body,
        grid=(m // tm,),
        in_specs=[_row_spec(tm, PAGE_SIZE)],
        out_specs=_row_spec(tm, PAGE_SIZE),
        out_shape=jax.ShapeDtypeStruct((m, PAGE_SIZE), F32),
        compiler_params=_cparams("arbitrary"),
        name="logf_pages",
    )(x)


def _decode_sample(b, slot, q_ref, kn_ref, vn_ref, lfn_ref, kbuf, vbuf, lfbuf, n_pages):
    past = n_pages * PAGE_SIZE
    scale = HEAD_DIM ** -0.5
    hm = (_idiv(_iota((SUBLANES, VD), 1), HEAD_DIM) == _iota((SUBLANES, VD), 0)).astype(F32)
    qbd = jnp.broadcast_to(q_ref[pl.ds(b, 1), :], (SUBLANES, VD)) * hm
    qb = qbd.astype(BF16)
    s = jnp.concatenate([_dot(qb, kbuf[slot, p].astype(BF16)) for p in range(n_pages)],
                        axis=1) * scale
    lf = lfbuf[slot]
    incl = pltpu.roll(lf, N_HEADS, 0)
    sh = PAGE_SIZE
    while sh < past:
        incl = incl + jnp.concatenate([incl[:, sh:], jnp.zeros((SUBLANES, sh), F32)], axis=1)
        sh *= 2
    later_pages = jnp.concatenate([incl[:, PAGE_SIZE:], jnp.zeros((SUBLANES, PAGE_SIZE), F32)], axis=1)
    valid = _iota((SUBLANES, past), 0) < N_HEADS
    logits = jnp.where(valid, s + lf + later_pages, 0.0)
    s_self = jnp.sum(qbd * kn_ref[pl.ds(b, 1), :], axis=1, keepdims=True) * scale
    pick = (_iota((SUBLANES, LANES), 1) == _iota((SUBLANES, LANES), 0)).astype(F32)
    c_new = jnp.sum(pick * lfn_ref[pl.ds(b, 1), :], axis=1, keepdims=True)
    self_logit = jnp.where(_iota((SUBLANES, 1), 0) < N_HEADS, s_self - c_new, 0.0)
    m = jnp.maximum(jnp.max(logits, axis=1, keepdims=True), self_logit)
    p = jnp.exp(logits - m)
    p_self = jnp.exp(self_logit - m)
    l = jnp.sum(p, axis=1, keepdims=True) + p_self
    pb = p.astype(BF16)
    pv = jnp.zeros((SUBLANES, VD), F32)
    for pg in range(n_pages):
        pv = pv + _dot_nt(pb[:, pg * PAGE_SIZE:(pg + 1) * PAGE_SIZE], vbuf[slot, pg].astype(BF16))
    o8 = (pv + p_self * vn_ref[pl.ds(b, 1), :]) / l
    return jnp.sum(o8 * hm, axis=0, keepdims=True)


def _decode_gather(pt_ref, kc_hbm, vc_hbm, lfc_hbm, kbuf, vbuf, lfbuf, sem, *, layer, n_pages, per_step, base):
    i = pl.program_id(0)
    n_steps = pl.num_programs(0)

    def copies(sample, slot):
        out = []
        for p in range(n_pages):
            page = pt_ref[sample, p]
            cols = pl.ds(p * PAGE_SIZE, PAGE_SIZE)
            out.append(pltpu.make_async_copy(kc_hbm.at[layer, page], kbuf.at[slot, p], sem.at[slot, 0]))
            out.append(pltpu.make_async_copy(vc_hbm.at[layer, page], vbuf.at[slot, p], sem.at[slot, 1]))
            out.append(pltpu.make_async_copy(lfc_hbm.at[layer, page], lfbuf.at[slot, :, cols], sem.at[slot, 2]))
        return out

    cur = (i % 2) * per_step
    nxt = per_step - cur
    first = base + i * per_step

    @pl.when(i == 0)
    def _():
        for j in range(per_step):
            for cp in copies(base + j, j):
                cp.start()

    for j in range(per_step):
        for cp in copies(first + j, cur + j):
            cp.wait()

    @pl.when(i + 1 < n_steps)
    def _():
        for j in range(per_step):
            for cp in copies(first + per_step + j, nxt + j):
                cp.start()

    return [(first + j, cur + j) for j in range(per_step)]


def _ffn_decode_body(*refs, fc, n_pages, layer, per_step, base, mix):
    pt_ref, refs = refs[0], refs[1:]
    if mix:
        h_ref, ya_ref, yb_ref, yc_ref, ydp_ref, wo_ref, mpost_ref = refs[:7]
        refs = refs[7:]
    else:
        x_ref, refs = refs[0], refs[1:]
    (pre_ref, post_ref, wg_ref, wu_ref, wd_ref, q_ref, kn_ref, vn_ref, lfn_ref, kc_hbm, vc_hbm, lfc_hbm,
     o_ref, yd_ref, acc_ref, kbuf, vbuf, lfbuf, sem) = refs
    pairs = _decode_gather(pt_ref, kc_hbm, vc_hbm, lfc_hbm, kbuf, vbuf, lfbuf, sem,
                           layer=layer, n_pages=n_pages, per_step=per_step, base=base)
    if mix:
        ycat = jnp.concatenate([ya_ref[...], yb_ref[...], yc_ref[...], ydp_ref[...]], axis=1)
        x = h_ref[...] + _rms(_dot(ycat.astype(BF16), wo_ref[0]), mpost_ref[...])
    else:
        x = x_ref[...]
    o_ref[...] = _swiglu_residual(x, pre_ref[...], post_ref[...],
                                  wg_ref.at[0], wu_ref.at[0], wd_ref.at[0], acc_ref, fc)
    for j, (sample, slot) in enumerate(pairs):
        yd_ref[j] = _decode_sample(sample, slot, q_ref, kn_ref, vn_ref, lfn_ref, kbuf, vbuf, lfbuf, n_pages)


def _ffn_decode_call(layer, x, mix_in, pre_g, post_g, wg, wu, wd, page_table, q, kn, vn, lfn, kc, vc, lfc,
                     base, n_dec, tm, fc=256):
    m, d = x.shape
    f = wg.shape[2]
    n, n_pages = page_table.shape
    steps = m // tm
    assert n_dec % steps == 0 and base + n_dec <= n
    per_step = n_dec // steps
    past = n_pages * PAGE_SIZE
    const = lambda shape: pl.BlockSpec(shape, lambda i, pt: (0,) * len(shape), pipeline_mode=pl.Buffered(1))
    lyr = lambda shape: pl.BlockSpec((1,) + shape, lambda i, pt: (layer, 0, 0), pipeline_mode=pl.Buffered(1))
    rows = lambda w: pl.BlockSpec((tm, w), lambda i, pt: (i, 0))
    anyspace = pl.BlockSpec(memory_space=pl.ANY)
    mix = mix_in is not None
    lead_specs, lead_args = [rows(d)], [x]
    if mix:
        ya, yb, yc, yd, wo, mpost = mix_in
        lead_specs += [rows(GROUP_W)] * 4 + [lyr((d, d)), const((1, d))]
        lead_args += [ya, yb, yc, yd, wo, mpost]
    grid_spec = pltpu.PrefetchScalarGridSpec(
        num_scalar_prefetch=1,
        grid=(steps,),
        in_specs=lead_specs + [const((1, d)), const((1, d)), lyr((d, f)), lyr((d, f)), lyr((f, d)),
                               const((n, VD)), const((n, VD)), const((n, VD)), const((n, LANES)),
                               anyspace, anyspace, anyspace],
        out_specs=[rows(d), pl.BlockSpec((per_step, 1, VD), lambda i, pt: (i, 0, 0))],
        scratch_shapes=[pltpu.VMEM((tm, d), F32),
                        pltpu.VMEM((2 * per_step, n_pages, VD, PAGE_SIZE), F32),
                        pltpu.VMEM((2 * per_step, n_pages, VD, PAGE_SIZE), F32),
                        pltpu.VMEM((2 * per_step, SUBLANES, past), F32),
                        pltpu.SemaphoreType.DMA((2 * per_step, 3))],
    )
    return pl.pallas_call(
        functools.partial(_ffn_decode_body, fc=fc, n_pages=n_pages, layer=layer, per_step=per_step,
                          base=base, mix=mix),
        grid_spec=grid_spec,
        out_shape=[jax.ShapeDtypeStruct((m, d), F32), jax.ShapeDtypeStruct((n_dec, 1, VD), F32)],
        compiler_params=_cparams("arbitrary"),
        name="mix_ffn_decode" if mix else "ffn_decode",
    )(page_table, *lead_args, pre_g, post_g, wg, wu, wd, q, kn, vn, lfn, kc, vc, lfc)


ATTN_TILE = 256


def _row_tile(m, cap=512):
    tm = cap
    while tm >= SUBLANES:
        if m % tm == 0:
            return tm
        tm //= 2
    raise ValueError(f"row count {m} is not a multiple of {SUBLANES}")


def kernel(x_prompt, x_sample, state_conv_a, state_gla, state_conv_c, cache_k, cache_v, cache_logf, page_table, ffn1_pre_g, ffn1_post_g, ffn1_w_gate, ffn1_w_up, ffn1_w_down, mix_pre_g, mix_post_g, w_in, w_out, a_conv_w, a_conv_b, a_ln_g, a_ln_b, b_gate_w2, b_gate_b, b_out_norm_g, c_conv_w, d_forget_b, ffn2_pre_g, ffn2_post_g, ffn2_w_gate, ffn2_w_up, ffn2_w_down):
    depth = w_in.shape[0]
    bp, seq, d = x_prompt.shape
    bd = x_sample.shape[0]
    n_pool = cache_k.shape[1]
    assert x_sample.shape[1] == 1 and d == D_MODEL
    assert seq % ATTN_TILE == 0 and seq % GLA_CHUNK == 0 and seq >= CONV_A_W - 1

    hp = x_prompt.reshape(bp * seq, d)
    hs = x_sample.reshape(bd, d)
    tmp = _row_tile(bp * seq)
    tms = _row_tile(bd)
    tq = ATTN_TILE

    off_b = N_PA
    off_c = off_b + 2 * KD + 2 * VD + GLA_RANK
    off_d = off_c + N_PC
    row = lambda v: v.reshape(1, -1)

    lanes_of = lambda v, n=LANES: jnp.broadcast_to(v[:, None], (v.shape[0], n))

    ck = cache_k.transpose(0, 1, 3, 4, 2).reshape(depth, n_pool, VD, PAGE_SIZE)
    cv = cache_v.transpose(0, 1, 3, 4, 2).reshape(depth, n_pool, VD, PAGE_SIZE)
    sca = state_conv_a.transpose(0, 2, 1, 3)
    sgl = state_gla.transpose(0, 2, 3, 4, 1).reshape(depth, KD, HEAD_DIM, bd)
    scc = state_conv_c.reshape(depth, bd, (CONV_C_W - 1) * GROUP_W)
    w_in_t = w_in.transpose(0, 2, 1)

    lf_t = jnp.pad(cache_logf.transpose(0, 1, 3, 2), ((0, 0), (0, 0), (0, SUBLANES - N_HEADS), (0, 0)))
    lf_rows = depth * n_pool * SUBLANES
    lf_pages = _logf_pages_call(lf_t.reshape(lf_rows, PAGE_SIZE), _row_tile(lf_rows, cap=4096))
    lf_pages = lf_pages.reshape(depth, n_pool, SUBLANES, PAGE_SIZE)

    ffn1_ws = tuple(w.astype(BF16) for w in (ffn1_w_gate, ffn1_w_up, ffn1_w_down))
    ffn2_ws = tuple(w.astype(BF16) for w in (ffn2_w_gate, ffn2_w_up, ffn2_w_down))
    wo = w_out.astype(BF16)

    outs = [[] for _ in range(12)]
    for l in range(depth):
        wt = w_in_t[l].astype(BF16)
        pad_rows = lambda w, n: jnp.pad(w, ((0, n - w.shape[0]), (0, 0)))
        wq, wk, wv = (wt[off_d + i * VD:off_d + (i + 1) * VD] for i in range(3))
        in_w = (wt[0:off_b], pad_rows(wt[off_b:off_c], N_PB), wt[off_c:off_d],
                wq, wk, wv, pad_rows(wt[off_d + 3 * VD:], LANES))
        slabs = lambda w: jnp.pad(w.reshape(N_HEADS, HEAD_DIM, d),
                                  ((0, 0), (0, EXT - HEAD_DIM), (0, 0))).reshape(N_HEADS * EXT, d)
        in_w_prompt = in_w[0:3] + (wq, slabs(wk), wk, wv, in_w[6])
        bf_pad = jnp.pad(d_forget_b[l], (0, LANES - N_HEADS))
        bfr = row(bf_pad)
        bfc = lanes_of(bf_pad[0:SUBLANES])
        wa2 = jnp.pad(b_gate_w2[l], ((0, LANES - GLA_RANK), (0, 0))).astype(BF16)
        mix_w = (a_conv_w[l], row(a_conv_b[l]), row(a_ln_g[l]), row(a_ln_b[l]))
        ffn1_w = (row(ffn1_pre_g[l]), row(ffn1_post_g[l])) + ffn1_ws
        ffn2_w = (row(ffn2_pre_g[l]), row(ffn2_post_g[l])) + ffn2_ws

        hs = _ffn_call(l, hs, *ffn1_w, tm=tms)
        s_pa, s_pbt, s_pc, s_q, s_k, s_v, s_lf, s_kt, s_vt, s_lft = _inproj_sample_call(
            hs, row(mix_pre_g[l]), in_w, bfc, bfr)
        s_ya, s_yb, s_yc, s_bufa, s_state, s_bufc = _mix_sample_call(
            l, s_pa, s_pbt, s_pc, sca, sgl, scc, *mix_w, wa2.T, lanes_of(b_gate_b[l], bd),
            lanes_of(b_out_norm_g[l], bd), c_conv_w[l])

        dec_args = (page_table, s_q, s_k, s_v, s_lf, ck, cv, lf_pages)
        half = bd // 2
        hp, s_yd0 = _ffn_decode_call(l, hp, None, *ffn1_w, *dec_args, base=0, n_dec=half, tm=tmp)
        pa, pb, pc, kt, vt, lft, qe, ke, vtb = _inproj_prompt_call(
            hp, row(mix_pre_g[l]), in_w_prompt, bfc, bp, seq, tm=tmp)
        r3 = lambda t: t.reshape(bp, seq, t.shape[-1])
        ya, yc, buf_a, buf_c = _conv_prompt_call(r3(pa), r3(pc), *mix_w, c_conv_w[l])
        yb, s_b = _gla_prompt_call(r3(pb), wa2, row(b_gate_b[l]), row(b_out_norm_g[l]))
        yd = _fox_prompt_call(qe, ke, vtb, tq)
        f2 = lambda t: t.reshape(bp * seq, t.shape[-1])
        hp, s_yd1 = _ffn_decode_call(l, hp, (f2(ya), f2(yb), f2(yc), f2(yd), wo, row(mix_post_g[l])),
                                     *ffn2_w, *dec_args, base=half, n_dec=bd - half, tm=tmp)
        for i, t in enumerate((buf_a, s_b, buf_c, kt, vt, lft[:, 0:N_HEADS, :])):
            outs[i].append(t)

        s_yd = jnp.concatenate([s_yd0, s_yd1], axis=0).reshape(bd, VD)
        hs = _mix_ffn_call(l, hs, s_ya, s_yb, s_yc, s_yd, wo, row(mix_post_g[l]), *ffn2_w, tm=tms)
        for i, t in enumerate((s_bufa, s_state, s_bufc, s_kt, s_vt, s_lft[0:N_HEADS, :])):
            outs[6 + i].append(t)

    p_ca, p_gla, p_cc, p_kt, p_vt, p_lft, s_ca, s_gla, s_cc, s_kt, s_vt, s_lft = (jnp.stack(o) for o in outs)
    heads = lambda t: t.reshape(t.shape[:-2] + (N_HEADS, HEAD_DIM, t.shape[-1]))
    return (hp.reshape(bp, seq, d), hs.reshape(bd, 1, d),
            p_ca, p_gla, p_cc,
            heads(p_kt).transpose(0, 1, 4, 2, 3), heads(p_vt).transpose(0, 1, 4, 2, 3),
            p_lft.transpose(0, 1, 3, 2),
            s_ca.transpose(0, 2, 1, 3),
            s_gla.reshape(depth, N_HEADS, DK_B, HEAD_DIM, bd).transpose(0, 4, 1, 2, 3),
            s_cc.reshape(depth, bd, CONV_C_W - 1, GROUP_W),
            heads(s_kt).transpose(0, 3, 1, 2)[:, :, None], heads(s_vt).transpose(0, 3, 1, 2)[:, :, None],
            s_lft.transpose(0, 2, 1)[:, :, None])
```

```python
import functools

import jax
import jax.numpy as jnp
from jax import lax
from jax.experimental import pallas as pl
from jax.experimental.pallas import tpu as pltpu

F32 = jnp.float32
BF16 = jnp.bfloat16

D_MODEL = 1024
GROUP_W = D_MODEL // 4
HEAD_DIM = 64
N_HEADS = GROUP_W // HEAD_DIM
DK_B = HEAD_DIM // 2
GLA_RANK = 16
GLA_TAU = 16.0
GLA_CHUNK = 64
CONV_A_W = 31
CONV_C_W = 3
PAGE_SIZE = 128
EPS = 1e-6
NEG_INF = -1e30

LANES = 128
SUBLANES = 8
VMEM_LIMIT_BYTES = 56 * 1024 * 1024

N_PA = 2 * GROUP_W
KD = N_HEADS * DK_B
VD = N_HEADS * HEAD_DIM
N_PB = 2 * KD + 2 * VD + LANES
N_PC = 3 * GROUP_W


def _cparams(*sem):
    return pltpu.CompilerParams(dimension_semantics=sem, vmem_limit_bytes=VMEM_LIMIT_BYTES)


def _dot(a, b):
    return jnp.dot(a, b, preferred_element_type=F32)


def _dot_nt(a, b):
    return lax.dot_general(a, b, (((1,), (1,)), ((), ())), preferred_element_type=F32)


def _split3(x):
    hi = x.astype(BF16)
    r = x - hi.astype(F32)
    mid = r.astype(BF16)
    lo = (r - mid.astype(F32)).astype(BF16)
    return hi, mid, lo


def _dot_sel(x, sel):
    hi, mid, lo = _split3(x)
    return _dot(hi, sel) + _dot(mid, sel) + _dot(lo, sel)


def _sel_dot(sel, x):
    hi, mid, lo = _split3(x)
    return _dot(sel, hi) + _dot(sel, mid) + _dot(sel, lo)


def _rms(x, g):
    return x * lax.rsqrt(jnp.mean(x * x, axis=-1, keepdims=True) + EPS) * g


def _sigmoid(x):
    return 1.0 / (1.0 + jnp.exp(-x))


def _silu(x):
    return x * _sigmoid(x)


def _log_sigmoid(x):
    return jnp.minimum(x, 0.0) - jnp.log1p(jnp.exp(-jnp.abs(x)))


def _iota(shape, dim):
    return lax.broadcasted_iota(jnp.int32, shape, dim)


def _idiv(x, n):
    assert n & (n - 1) == 0
    return lax.shift_right_logical(x, n.bit_length() - 1)


def _imod(x, n):
    assert n & (n - 1) == 0
    return x & (n - 1)


def _head_mean_matrix():
    r = _idiv(_iota((VD, VD), 0), HEAD_DIM)
    c = _idiv(_iota((VD, VD), 1), HEAD_DIM)
    return (r == c).astype(BF16)


def _swiglu_residual(x, pre_g, post_g, wg_ref, wu_ref, wd_ref, acc_ref, fc):
    xn = _rms(x, pre_g).astype(BF16)
    for c in range(wg_ref.shape[1] // fc):
        sl = slice(c * fc, (c + 1) * fc)
        g = _dot(xn, wg_ref[:, sl])
        u = _dot(xn, wu_ref[:, sl])
        hid = (_silu(g) * u).astype(BF16)
        part = _dot(hid, wd_ref[sl, :])
        if c == 0:
            acc_ref[...] = part
        else:
            acc_ref[...] += part
    return x + 0.5 * _rms(acc_ref[...], post_g)


def _ffn_body(x_ref, pre_ref, post_ref, wg_ref, wu_ref, wd_ref, o_ref, acc_ref, *, fc):
    o_ref[...] = _swiglu_residual(x_ref[...], pre_ref[...], post_ref[...],
                                  wg_ref.at[0], wu_ref.at[0], wd_ref.at[0], acc_ref, fc)


def _mix_ffn_body(h_ref, ya_ref, yb_ref, yc_ref, yd_ref, wo_ref, mpost_ref,
                  pre_ref, post_ref, wg_ref, wu_ref, wd_ref, o_ref, acc_ref, *, fc):
    ycat = jnp.concatenate([ya_ref[...], yb_ref[...], yc_ref[...], yd_ref[...]], axis=1)
    y = _dot(ycat.astype(BF16), wo_ref[0])
    h = h_ref[...] + _rms(y, mpost_ref[...])
    o_ref[...] = _swiglu_residual(h, pre_ref[...], post_ref[...],
                                  wg_ref.at[0], wu_ref.at[0], wd_ref.at[0], acc_ref, fc)


def _row_spec(tm, n):
    return pl.BlockSpec((tm, n), lambda i: (i, 0))


def _full_spec(shape):
    return pl.BlockSpec(shape, lambda i: (0,) * len(shape), pipeline_mode=pl.Buffered(1))


def _ffn_call(layer, x, pre_g, post_g, wg, wu, wd, tm, fc=256):
    m, d = x.shape
    return pl.pallas_call(
        functools.partial(_ffn_body, fc=fc),
        grid=(m // tm,),
        in_specs=[_row_spec(tm, d), _full_spec((1, d)), _full_spec((1, d)),
                  _layer_spec(wg.shape, layer), _layer_spec(wu.shape, layer), _layer_spec(wd.shape, layer)],
        out_specs=_row_spec(tm, d),
        out_shape=jax.ShapeDtypeStruct((m, d), F32),
        scratch_shapes=[pltpu.VMEM((tm, d), F32)],
        compiler_params=_cparams("arbitrary"),
        name="ffn",
    )(x, pre_g, post_g, wg, wu, wd)


def _mix_ffn_call(layer, h, ya, yb, yc, yd, wo, mpost, pre_g, post_g, wg, wu, wd, tm, fc=256):
    m, d = h.shape
    return pl.pallas_call(
        functools.partial(_mix_ffn_body, fc=fc),
        grid=(m // tm,),
        in_specs=[_row_spec(tm, d)] + [_row_spec(tm, GROUP_W)] * 4
                 + [_layer_spec(wo.shape, layer), _full_spec((1, d)), _full_spec((1, d)), _full_spec((1, d)),
                    _layer_spec(wg.shape, layer), _layer_spec(wu.shape, layer), _layer_spec(wd.shape, layer)],
        out_specs=_row_spec(tm, d),
        out_shape=jax.ShapeDtypeStruct((m, d), F32),
        scratch_shapes=[pltpu.VMEM((tm, d), F32)],
        compiler_params=_cparams("arbitrary"),
        name="mix_ffn",
    )(h, ya, yb, yc, yd, wo, mpost, pre_g, post_g, wg, wu, wd)


EXT = LANES
C_KEY = HEAD_DIM
C_QRY = HEAD_DIM + 3


def _inproj_prompt_body(h_ref, g_ref, wa_ref, wb_ref, wc_ref, wq_ref, wke_ref, wk_ref, wv_ref, wl_ref,
                        bfc_ref, pa_ref, pb_ref, pc_ref, kt_ref, vt_ref, lft_ref,
                        qe_ref, ke_ref, vtb_ref, carry_ref, *, per):
    tm = h_ref.shape[0]
    u = _rms(h_ref[...], g_ref[...]).astype(BF16)
    pa_ref[...] = _dot_nt(u, wa_ref[...])
    pb_ref[...] = _dot_nt(u, wb_ref[...])
    pc_ref[...] = _dot_nt(u, wc_ref[...])
    kt_ref[0] = _dot_nt(wk_ref[...], u)
    vt = _dot_nt(wv_ref[...], u)
    vt_ref[0] = vt
    vtb_ref[0] = vt.astype(BF16)
    lft = _log_sigmoid(_dot_nt(wl_ref[0:SUBLANES, :], u) + bfc_ref[:, 0:1])
    lft_ref[0] = lft

    first = pl.program_id(0) % per == 0
    carry = jnp.where(first, 0.0, carry_ref[:, 0:1])
    upto = (_iota((tm, tm), 0) <= _iota((tm, tm), 1)).astype(BF16)
    lfm = jnp.where(_iota((SUBLANES, tm), 0) < N_HEADS, lft, 0.0)
    c = _dot_sel(lfm, upto) + carry
    carry_ref[...] = jnp.broadcast_to(c[:, tm - 1:tm], carry_ref.shape)
    ccol = jnp.concatenate([c, jnp.zeros((LANES - SUBLANES, tm), F32)], axis=0).T

    hi, mid, lo = _split3(ccol)
    pieces = (hi.astype(F32) + pltpu.roll(mid.astype(F32), N_HEADS, 1)
              + pltpu.roll(lo.astype(F32), 2 * N_HEADS, 1)).astype(BF16)
    r = _iota((LANES, N_HEADS * EXT), 0)
    col = _iota((LANES, N_HEADS * EXT), 1)
    j = _imod(col, EXT) - C_KEY
    place_k = ((j >= 0) & (j < 3) & (r == N_HEADS * j + _idiv(col, EXT))).astype(BF16)
    lane = _imod(_iota((1, N_HEADS * EXT), 1), EXT)
    ones_k = ((lane >= C_QRY) & (lane < C_QRY + 3)).astype(F32)
    ke = _dot_nt(u, wke_ref[...]) + _dot(pieces, place_k) + ones_k
    for h in range(N_HEADS):
        ke_ref[0, h] = ke[:, h * EXT:(h + 1) * EXT].astype(BF16)

    hi, mid, lo = _split3(c)
    pieces_t = jnp.concatenate([hi.astype(F32), mid.astype(F32), lo.astype(F32),
                                jnp.zeros((LANES - 3 * SUBLANES, tm), F32)], axis=0).astype(BF16)
    r = _iota((VD, LANES), 0)
    col = _iota((VD, LANES), 1)
    j = _imod(r, HEAD_DIM) - (C_QRY - HEAD_DIM)
    place_q = ((j >= 0) & (j < 3) & (col == SUBLANES * j + _idiv(r, HEAD_DIM))).astype(BF16)
    rowi = _imod(_iota((VD, 1), 0), HEAD_DIM)
    neg_q = jnp.where(rowi < 3, -1.0, 0.0)
    extra = (_dot(place_q, pieces_t) + neg_q).astype(BF16)
    qt = (_dot_nt(wq_ref[...], u) * (HEAD_DIM ** -0.5)).astype(BF16)
    for h in range(N_HEADS):
        rows = slice(h * HEAD_DIM, (h + 1) * HEAD_DIM)
        qe_ref[0, h, 0:HEAD_DIM, :] = qt[rows, :]
        qe_ref[0, h, HEAD_DIM:, :] = extra[rows, :]


def _inproj_sample_body(h_ref, g_ref, wa_ref, wb_ref, wc_ref, wq_ref, wk_ref, wv_ref, wl_ref,
                        bfc_ref, bfr_ref,
                        pa_ref, pbt_ref, pc_ref, q_ref, k_ref, v_ref, lf_ref, kt_ref, vt_ref, lft_ref):
    u = _rms(h_ref[...], g_ref[...]).astype(BF16)
    pa_ref[...] = _dot_nt(u, wa_ref[...])
    pbt_ref[...] = _dot_nt(wb_ref[...], u)
    pc_ref[...] = _dot_nt(u, wc_ref[...])
    q_ref[...] = _dot_nt(u, wq_ref[...])
    k_ref[...] = _dot_nt(u, wk_ref[...])
    v_ref[...] = _dot_nt(u, wv_ref[...])
    lf_ref[...] = _log_sigmoid(_dot_nt(u, wl_ref[...]) + bfr_ref[...])
    kt_ref[...] = _dot_nt(wk_ref[...], u)
    vt_ref[...] = _dot_nt(wv_ref[...], u)
    lft_ref[...] = _log_sigmoid(_dot_nt(wl_ref[0:SUBLANES, :], u) + bfc_ref[:, 0:1])


def _inproj_weight_specs(d):
    return [_full_spec((n, d)) for n in (N_PA, N_PB, N_PC, VD, VD, VD, LANES)]


def _inproj_prompt_call(h, g, ws, bfc, bp, seq, tm):
    m, d = h.shape
    per = seq // tm
    tspec = lambda n: pl.BlockSpec((1, n, tm), lambda i: (i // per, 0, i % per))
    hx = N_HEADS * EXT
    return pl.pallas_call(
        functools.partial(_inproj_prompt_body, per=per),
        grid=(m // tm,),
        in_specs=[_row_spec(tm, d), _full_spec((1, d))]
                 + [_full_spec((n, d)) for n in (N_PA, N_PB, N_PC, VD, hx, VD, VD, LANES)]
                 + [_full_spec((SUBLANES, LANES))],
        out_specs=[_row_spec(tm, N_PA), _row_spec(tm, N_PB), _row_spec(tm, N_PC),
                   tspec(VD), tspec(VD), tspec(SUBLANES),
                   pl.BlockSpec((1, N_HEADS, EXT, tm), lambda i: (i // per, 0, 0, i % per)),
                   pl.BlockSpec((1, N_HEADS, tm, EXT), lambda i: (i // per, 0, i % per, 0)),
                   tspec(VD)],
        out_shape=[jax.ShapeDtypeStruct((m, N_PA), F32), jax.ShapeDtypeStruct((m, N_PB), F32),
                   jax.ShapeDtypeStruct((m, N_PC), F32),
                   jax.ShapeDtypeStruct((bp, VD, seq), F32), jax.ShapeDtypeStruct((bp, VD, seq), F32),
                   jax.ShapeDtypeStruct((bp, SUBLANES, seq), F32),
                   jax.ShapeDtypeStruct((bp, N_HEADS, EXT, seq), BF16),
                   jax.ShapeDtypeStruct((bp, N_HEADS, seq, EXT), BF16),
                   jax.ShapeDtypeStruct((bp, VD, seq), BF16)],
        scratch_shapes=[pltpu.VMEM((SUBLANES, LANES), F32)],
        compiler_params=_cparams("arbitrary"),
        name="inproj_prompt",
    )(h, g, *ws, bfc)


def _inproj_sample_call(h, g, ws, bfc, bfr):
    n, d = h.shape
    shapes = [(n, N_PA), (N_PB, n), (n, N_PC), (n, VD), (n, VD), (n, VD), (n, LANES),
              (VD, n), (VD, n), (SUBLANES, n)]
    return pl.pallas_call(
        _inproj_sample_body,
        grid=(1,),
        in_specs=[_full_spec((n, d)), _full_spec((1, d))] + _inproj_weight_specs(d)
                 + [_full_spec((SUBLANES, LANES)), _full_spec((1, LANES))],
        out_specs=[pl.BlockSpec(s, lambda i: (0, 0)) for s in shapes],
        out_shape=[jax.ShapeDtypeStruct(s, F32) for s in shapes],
        compiler_params=_cparams("arbitrary"),
        name="inproj_sample",
    )(h, g, *ws, bfc, bfr)


A_PAD = 32
C_PAD = 8
CONV_ROWS = 128


def _conv_prompt_body(pa_ref, pc_ref, aw_ref, ab_ref, lg_ref, lb_ref, cw_ref,
                      ya_ref, yc_ref, bufa_ref, bufc_ref, apad_ref, zpad_ref):
    seq = pa_ref.shape[1]
    c = GROUP_W
    apad_ref[0:A_PAD, :] = jnp.zeros((A_PAD, c), F32)
    apad_ref[A_PAD:, :] = pa_ref[0, :, 0:c] * _sigmoid(pa_ref[0, :, c:])
    zpad_ref[0:C_PAD, :] = jnp.zeros((C_PAD, c), F32)
    zpad_ref[C_PAD:, :] = pc_ref[0, :, c:2 * c] * pc_ref[0, :, 2 * c:]

    def step(i, carry):
        r0 = pl.multiple_of(i * CONV_ROWS, CONV_ROWS)
        win = apad_ref[pl.ds(r0, CONV_ROWS + A_PAD), :]
        acc = jnp.zeros((CONV_ROWS, c), F32) + ab_ref[...]
        for r in range(SUBLANES):
            nrow = CONV_ROWS if r == 0 else CONV_ROWS + SUBLANES
            u = None
            for a8 in range(0, A_PAD + 1, SUBLANES):
                j = a8 + r - (A_PAD - (CONV_A_W - 1))
                if 0 <= j < CONV_A_W:
                    term = aw_ref[j:j + 1, :] * win[a8:a8 + nrow, :]
                    u = term if u is None else u + term
            acc = acc + u[r:r + CONV_ROWS, :]
        mu = jnp.mean(acc, axis=-1, keepdims=True)
        var = jnp.mean(jnp.square(acc - mu), axis=-1, keepdims=True)
        yn = (acc - mu) * lax.rsqrt(var + EPS) * lg_ref[...] + lb_ref[...]
        ya_ref[0, pl.ds(r0, CONV_ROWS), :] = _silu(yn)
        zwin = zpad_ref[pl.ds(r0, CONV_ROWS + C_PAD), :]
        accc = jnp.zeros((CONV_ROWS, c), F32)
        for j in range(CONV_C_W):
            off = C_PAD - (CONV_C_W - 1) + j
            accc = accc + cw_ref[j:j + 1, :] * zwin[off:off + CONV_ROWS, :]
        yc_ref[0, pl.ds(r0, CONV_ROWS), :] = pc_ref[0, pl.ds(r0, CONV_ROWS), 0:c] * accc
        return carry

    lax.fori_loop(0, seq // CONV_ROWS, step, 0)
    na = CONV_A_W - 1
    nc = CONV_C_W - 1
    bufa_ref[0] = apad_ref[seq:A_PAD + seq, :][A_PAD - na:, :]
    bufc_ref[0] = zpad_ref[seq:C_PAD + seq, :][C_PAD - nc:, :]


def _conv_prompt_call(pa, pc, aw, ab, lg, lb, cw):
    b, seq, _ = pa.shape
    c = GROUP_W
    bspec = lambda n: pl.BlockSpec((1, seq, n), lambda i: (i, 0, 0))
    return pl.pallas_call(
        _conv_prompt_body,
        grid=(b,),
        in_specs=[bspec(N_PA), bspec(N_PC), _full_spec((CONV_A_W, c)), _full_spec((1, c)),
                  _full_spec((1, c)), _full_spec((1, c)), _full_spec((CONV_C_W, c))],
        out_specs=[bspec(c), bspec(c),
                   pl.BlockSpec((1, CONV_A_W - 1, c), lambda i: (i, 0, 0)),
                   pl.BlockSpec((1, CONV_C_W - 1, c), lambda i: (i, 0, 0))],
        out_shape=[jax.ShapeDtypeStruct((b, seq, c), F32), jax.ShapeDtypeStruct((b, seq, c), F32),
                   jax.ShapeDtypeStruct((b, CONV_A_W - 1, c), F32),
                   jax.ShapeDtypeStruct((b, CONV_C_W - 1, c), F32)],
        scratch_shapes=[pltpu.VMEM((A_PAD + seq, c), F32), pltpu.VMEM((C_PAD + seq, c), F32)],
        compiler_params=_cparams("arbitrary"),
        name="conv_prompt",
    )(pa, pc, aw, ab, lg, lb, cw)


def _gla_gate(alr, wa2_ref, ba_ref):
    return _log_sigmoid(_dot(alr.astype(BF16), wa2_ref[...]) + ba_ref[...]) * (1.0 / GLA_TAU)


def _gla_out(o, g, ng, hm):
    ms = _dot_sel(o * o, hm) * (1.0 / HEAD_DIM)
    return o * lax.rsqrt(ms + EPS) * ng * _silu(g)


GLA_SLAB = 512


def _group_row(x, g, r):
    n, c = x.shape
    x3 = x.reshape(n // g, g, c)
    return jnp.broadcast_to(x3[:, r:r + 1, :], (n // g, g, c)).reshape(n, c)


def _gla_prompt_body(pb_ref, wa2_ref, ba_ref, ng_ref, y_ref, st_ref, la_ref, s_ref):
    seq = pb_ref.shape[1]
    ck = GLA_CHUNK
    la_ref[...] = _gla_gate(pb_ref[0, :, 2 * KD + 2 * VD:], wa2_ref, ba_ref)
    s_ref[...] = jnp.zeros((KD, VD), F32)

    rs = min(GLA_SLAB, seq)
    nch = rs // ck
    tt = _iota((rs, rs), 0)
    ss = _iota((rs, rs), 1)
    tri = ((ss <= tt) & (_idiv(ss, ck) == _idiv(tt, ck))).astype(BF16)
    blk = _idiv(_iota((KD, VD), 0), DK_B) == _idiv(_iota((KD, VD), 1), HEAD_DIM)
    blk_bf = blk.astype(BF16)
    blk_f = blk.astype(F32)
    hm = _head_mean_matrix()
    scale = DK_B ** -0.5
    levels = [g for g in (2 * SUBLANES, 4 * SUBLANES, 8 * SUBLANES) if g <= ck]
    assert ck == 8 * SUBLANES
    key_head = _idiv(_iota((1, KD), 1), DK_B)
    val_head = _idiv(_iota((1, VD), 1), HEAD_DIM)
    tloc = _iota((rs, KD), 0)

    def slab(si, carry):
        r0 = pl.multiple_of(si * rs, rs)
        rows = pl.ds(r0, rs)
        q = pb_ref[0, rows, 0:KD] * scale
        k = pb_ref[0, rows, KD:2 * KD]
        v = pb_ref[0, rows, 2 * KD:2 * KD + VD]
        g = pb_ref[0, rows, 2 * KD + VD:2 * KD + 2 * VD]
        b = _sel_dot(tri, la_ref[rows, :])
        vb3 = v.astype(BF16).reshape(nch, ck, VD)

        t8 = _imod(tloc, SUBLANES)
        acc = jnp.zeros((rs, VD), F32)
        for s in range(SUBLANES):
            diff = jnp.where(t8 >= s, b - _group_row(b, SUBLANES, s), -jnp.inf)
            e = (q * _group_row(k, SUBLANES, s) * jnp.exp(diff)).astype(BF16)
            acc = acc + _dot(e, blk_bf) * _group_row(v, SUBLANES, s)

        sc = jnp.zeros((nch, N_HEADS * ck, ck), F32)
        for gsz in levels:
            r = _group_row(b, gsz, gsz // 2)
            upper = _imod(tloc, gsz) >= gsz // 2
            qf = q * jnp.exp(jnp.where(upper, b - r, -jnp.inf))
            kf = (k * jnp.exp(jnp.where(upper, -jnp.inf, r - b))).astype(BF16).reshape(nch, ck, KD)
            qstack = jnp.concatenate(
                [jnp.where(key_head == h, qf, 0.0).astype(BF16).reshape(nch, ck, KD)
                 for h in range(N_HEADS)], axis=1)
            sc_g = jnp.einsum('cqk,csk->cqs', qstack, kf, preferred_element_type=F32)
            same_group = _idiv(_imod(_iota((N_HEADS * ck, ck), 0), ck), gsz) == _idiv(_iota((N_HEADS * ck, ck), 1), gsz)
            sc = sc + jnp.where(same_group[None], sc_g, 0.0)
        res = jnp.einsum('cqs,csv->cqv', sc.astype(BF16), vb3, preferred_element_type=F32)
        for h in range(N_HEADS):
            part = res[:, h * ck:(h + 1) * ck, :].reshape(rs, VD)
            acc = acc + jnp.where(val_head == h, part, 0.0)

        b_last = _group_row(b, ck, ck - 1)
        kdec = k * jnp.exp(b_last - b)
        lastb = jnp.concatenate([b[(c + 1) * ck - 1:(c + 1) * ck, :] for c in range(nch)]
                                + [jnp.zeros((LANES - nch, KD), F32)], axis=0)
        dcols = jnp.exp(lastb.T)
        s_cur = s_ref[...]
        s_before = []
        for c in range(nch):
            s_before.append(s_cur.astype(BF16))
            crow = slice(c * ck, (c + 1) * ck)
            upd = _dot(kdec[crow, :].T.astype(BF16), vb3[c])
            s_cur = dcols[:, c:c + 1] * s_cur + upd * blk_f
        s_ref[...] = s_cur
        qb3 = (q * jnp.exp(b)).astype(BF16).reshape(nch, ck, KD)
        inter = jnp.einsum('cqk,ckv->cqv', qb3, jnp.stack(s_before), preferred_element_type=F32)
        o = inter.reshape(rs, VD) + acc
        y_ref[0, rows, :] = _gla_out(o, g, ng_ref[...], hm)
        return carry

    lax.fori_loop(0, seq // rs, slab, 0)
    for h in range(N_HEADS):
        st_ref[0, h] = s_ref[h * DK_B:(h + 1) * DK_B, h * HEAD_DIM:(h + 1) * HEAD_DIM]


def _gla_prompt_call(pb, wa2, ba, ng):
    b, seq, _ = pb.shape
    return pl.pallas_call(
        _gla_prompt_body,
        grid=(b,),
        in_specs=[pl.BlockSpec((1, seq, N_PB), lambda i: (i, 0, 0)), _full_spec((LANES, KD)),
                  _full_spec((1, KD)), _full_spec((1, VD))],
        out_specs=[pl.BlockSpec((1, seq, VD), lambda i: (i, 0, 0)),
                   pl.BlockSpec((1, N_HEADS, DK_B, HEAD_DIM), lambda i: (i, 0, 0, 0))],
        out_shape=[jax.ShapeDtypeStruct((b, seq, VD), F32),
                   jax.ShapeDtypeStruct((b, N_HEADS, DK_B, HEAD_DIM), F32)],
        scratch_shapes=[pltpu.VMEM((seq, KD), F32), pltpu.VMEM((KD, VD), F32)],
        compiler_params=_cparams("arbitrary"),
        name="gla_prompt",
    )(pb, wa2, ba, ng)


def _fox_prompt_body(qe_ref, ke_ref, vtb_ref, o_ref, *, tq):
    qi = pl.program_id(1)
    key_pos = _iota((tq, tq), 0)
    qry_pos = _iota((tq, tq), 1)

    def tile(ki):
        return pl.ds(pl.multiple_of(ki * tq, tq), tq)

    def consume(tiles, stats):
        scores = [[_dot(ke_ref[0, h, tile(ki), :], qe_ref[0, h]) for h in range(N_HEADS)]
                  for ki, _ in tiles]
        for (ki, masked), sc in zip(tiles, scores):
            soft = []
            for h in range(N_HEADS):
                m, l, _ = stats[h]
                s = sc[h]
                if masked:
                    s = jnp.where(key_pos <= qry_pos, s, NEG_INF)
                m_new = jnp.maximum(m, jnp.max(s, axis=0, keepdims=True))
                alpha = jnp.exp(m - m_new)
                p = jnp.exp(s - m_new)
                soft.append((m_new, alpha * l + jnp.sum(p, axis=0, keepdims=True), alpha, p.astype(BF16)))
            out = []
            for h in range(N_HEADS):
                m_new, l, alpha, p = soft[h]
                vh = vtb_ref[0, h * HEAD_DIM:(h + 1) * HEAD_DIM, tile(ki)]
                out.append((m_new, l, alpha * stats[h][2] + _dot(vh, p)))
            stats = tuple(out)
        return stats

    init = tuple((jnp.full((1, tq), NEG_INF, F32), jnp.zeros((1, tq), F32),
                  jnp.zeros((HEAD_DIM, tq), F32)) for _ in range(N_HEADS))
    stats = lax.fori_loop(0, qi // 2, lambda k, st: consume([(2 * k, False), (2 * k + 1, False)], st), init)
    stats = lax.cond(qi % 2 == 1,
                     lambda st: consume([(qi - 1, False), (qi, True)], st),
                     lambda st: consume([(qi, True)], st), stats)
    o_ref[0] = jnp.concatenate([acc / l for _, l, acc in stats], axis=0).T


def _fox_prompt_call(qe, ke, vtb, tq):
    b, _, _, seq = qe.shape
    return pl.pallas_call(
        functools.partial(_fox_prompt_body, tq=tq),
        grid=(b, seq // tq),
        in_specs=[pl.BlockSpec((1, N_HEADS, EXT, tq), lambda i, j: (i, 0, 0, j)),
                  pl.BlockSpec((1, N_HEADS, seq, EXT), lambda i, j: (i, 0, 0, 0)),
                  pl.BlockSpec((1, VD, seq), lambda i, j: (i, 0, 0))],
        out_specs=pl.BlockSpec((1, tq, VD), lambda i, j: (i, j, 0)),
        out_shape=jax.ShapeDtypeStruct((b, seq, VD), F32),
        compiler_params=_cparams("arbitrary", "arbitrary"),
        name="fox_prompt",
    )(qe, ke, vtb)


def _mix_sample_body(pa_ref, pbt_ref, pc_ref, bufa_ref, sg_ref, bufc_ref,
                     aw_ref, ab_ref, lg_ref, lb_ref, wa2t_ref, bac_ref, ngc_ref, cw_ref,
                     ya_ref, yb_ref, yc_ref, bufa_o, sg_o, bufc_o, q_s, k_s, dec_s):
    c = GROUP_W
    a = pa_ref[:, 0:c] * _sigmoid(pa_ref[:, c:])
    na = CONV_A_W - 1
    acc = aw_ref[na:na + 1, :] * a + ab_ref[...]
    for j in range(na):
        acc = acc + aw_ref[j:j + 1, :] * bufa_ref[0, j]
    mu = jnp.mean(acc, axis=-1, keepdims=True)
    var = jnp.mean(jnp.square(acc - mu), axis=-1, keepdims=True)
    ya_ref[...] = _silu((acc - mu) * lax.rsqrt(var + EPS) * lg_ref[...] + lb_ref[...])
    for j in range(na - 1):
        bufa_o[j] = bufa_ref[0, j + 1]
    bufa_o[na - 1] = a
    z = pc_ref[:, c:2 * c] * pc_ref[:, 2 * c:]
    conv = (cw_ref[0:1, :] * bufc_ref[0, :, 0:c] + cw_ref[1:2, :] * bufc_ref[0, :, c:]
            + cw_ref[2:3, :] * z)
    yc_ref[...] = pc_ref[:, 0:c] * conv
    bufc_o[:, 0:c] = bufc_ref[0, :, c:]
    bufc_o[:, c:] = z
    q_s[...] = pbt_ref[0:KD, :] * (DK_B ** -0.5)
    k_s[...] = pbt_ref[KD:2 * KD, :]
    gate = _dot(wa2t_ref[...], pbt_ref[2 * KD + 2 * VD:, :].astype(BF16)) + bac_ref[...]
    dec_s[...] = jnp.exp(_log_sigmoid(gate) * (1.0 / GLA_TAU))
    n = pa_ref.shape[0]
    ys = []
    for h in range(N_HEADS):
        vrows = slice(2 * KD + h * HEAD_DIM, 2 * KD + (h + 1) * HEAD_DIM)
        grows = slice(2 * KD + VD + h * HEAD_DIM, 2 * KD + VD + (h + 1) * HEAD_DIM)
        vh = pbt_ref[vrows, :]

        def key_step(kk, o, h=h, vh=vh):
            hk = h * DK_B + kk
            one = pl.ds(hk, 1)
            s_new = dec_s[one, :] * sg_ref[0, hk] + k_s[one, :] * vh
            sg_o[hk] = s_new
            return o + q_s[one, :] * s_new

        o = lax.fori_loop(0, DK_B, key_step, jnp.zeros((HEAD_DIM, n), F32))
        ms = jnp.mean(o * o, axis=0, keepdims=True)
        ng = ngc_ref[h * HEAD_DIM:(h + 1) * HEAD_DIM, :]
        ys.append(o * lax.rsqrt(ms + EPS) * ng * _silu(pbt_ref[grows, :]))
    yb_ref[...] = jnp.concatenate(ys, axis=0).T


def _layer_spec(shape, l):
    return pl.BlockSpec((1,) + tuple(shape[1:]), lambda i: (l,) + (0,) * (len(shape) - 1),
                        pipeline_mode=pl.Buffered(1))


def _mix_sample_call(l, pa, pbt, pc, bufa, sg, bufc, aw, ab, lg, lb, wa2t, bac, ngc, cw):
    n = pa.shape[0]
    c = GROUP_W
    small = (aw, ab, lg, lb, wa2t, bac, ngc, cw)
    out_shape = [jax.ShapeDtypeStruct((n, c), F32)] * 3 + [
        jax.ShapeDtypeStruct(bufa.shape[1:], F32), jax.ShapeDtypeStruct(sg.shape[1:], F32),
        jax.ShapeDtypeStruct(bufc.shape[1:], F32)]
    return pl.pallas_call(
        _mix_sample_body,
        grid=(1,),
        in_specs=[_full_spec(pa.shape), _full_spec(pbt.shape), _full_spec(pc.shape),
                  _layer_spec(bufa.shape, l), _layer_spec(sg.shape, l), _layer_spec(bufc.shape, l)]
                 + [_full_spec(x.shape) for x in small],
        out_specs=[pl.BlockSpec(s.shape, lambda i, nd=len(s.shape): (0,) * nd) for s in out_shape],
        out_shape=out_shape,
        scratch_shapes=[pltpu.VMEM((KD, n), F32)] * 3,
        compiler_params=_cparams("arbitrary"),
        name="mix_sample",
    )(pa, pbt, pc, bufa, sg, bufc, *small)


def _logf_pages_body(x_ref, o_ref):
    n = x_ref.shape[0]
    x = x_ref[...]
    later = (_iota((PAGE_SIZE, PAGE_SIZE), 0) > _iota((PAGE_SIZE, PAGE_SIZE), 1)).astype(BF16)
    ones = jnp.ones((PAGE_SIZE, PAGE_SIZE), BF16)
    within = _dot_sel(x, later)
    total = pltpu.roll(_dot_sel(x, ones), N_HEADS, 0)
    o_ref[...] = jnp.where(_imod(_iota((n, PAGE_SIZE), 0), SUBLANES) < N_HEADS, within, total)


def _logf_pages_call(x, tm):
    m = x.shape[0]
    return pl.pallas_call(
        _logf_pages_body,
        grid=(m // tm,),
        in_specs=[_row_spec(tm, PAGE_SIZE)],
        out_specs=_row_spec(tm, PAGE_SIZE),
        out_shape=jax.ShapeDtypeStruct((m, PAGE_SIZE), F32),
        compiler_params=_cparams("arbitrary"),
        name="logf_pages",
    )(x)


def _decode_sample(b, slot, q_ref, kn_ref, vn_ref, lfn_ref, kbuf, vbuf, lfbuf, n_pages):
    past = n_pages * PAGE_SIZE
    scale = HEAD_DIM ** -0.5
    hm = (_idiv(_iota((SUBLANES, VD), 1), HEAD_DIM) == _iota((SUBLANES, VD), 0)).astype(F32)
    qbd = jnp.broadcast_to(q_ref[pl.ds(b, 1), :], (SUBLANES, VD)) * hm
    qb = qbd.astype(BF16)
    s = jnp.concatenate([_dot(qb, kbuf[slot, p].astype(BF16)) for p in range(n_pages)],
                        axis=1) * scale
    lf = lfbuf[slot]
    incl = pltpu.roll(lf, N_HEADS, 0)
    sh = PAGE_SIZE
    while sh < past:
        incl = incl + jnp.concatenate([incl[:, sh:], jnp.zeros((SUBLANES, sh), F32)], axis=1)
        sh *= 2
    later_pages = jnp.concatenate([incl[:, PAGE_SIZE:], jnp.zeros((SUBLANES, PAGE_SIZE), F32)], axis=1)
    valid = _iota((SUBLANES, past), 0) < N_HEADS
    logits = jnp.where(valid, s + lf + later_pages, 0.0)
    s_self = jnp.sum(qbd * kn_ref[pl.ds(b, 1), :], axis=1, keepdims=True) * scale
    pick = (_iota((SUBLANES, LANES), 1) == _iota((SUBLANES, LANES), 0)).astype(F32)
    c_new = jnp.sum(pick * lfn_ref[pl.ds(b, 1), :], axis=1, keepdims=True)
    self_logit = jnp.where(_iota((SUBLANES, 1), 0) < N_HEADS, s_self - c_new, 0.0)
    m = jnp.maximum(jnp.max(logits, axis=1, keepdims=True), self_logit)
    p = jnp.exp(logits - m)
    p_self = jnp.exp(self_logit - m)
    l = jnp.sum(p, axis=1, keepdims=True) + p_self
    pb = p.astype(BF16)
    pv = jnp.zeros((SUBLANES, VD), F32)
    for pg in range(n_pages):
        pv = pv + _dot_nt(pb[:, pg * PAGE_SIZE:(pg + 1) * PAGE_SIZE], vbuf[slot, pg].astype(BF16))
    o8 = (pv + p_self * vn_ref[pl.ds(b, 1), :]) / l
    return jnp.sum(o8 * hm, axis=0, keepdims=True)


def _decode_gather(pt_ref, kc_hbm, vc_hbm, lfc_hbm, kbuf, vbuf, lfbuf, sem, *, layer, n_pages, per_step, base):
    i = pl.program_id(0)
    n_steps = pl.num_programs(0)

    def copies(sample, slot):
        out = []
        for p in range(n_pages):
            page = pt_ref[sample, p]
            cols = pl.ds(p * PAGE_SIZE, PAGE_SIZE)
            out.append(pltpu.make_async_copy(kc_hbm.at[layer, page], kbuf.at[slot, p], sem.at[slot, 0]))
            out.append(pltpu.make_async_copy(vc_hbm.at[layer, page], vbuf.at[slot, p], sem.at[slot, 1]))
            out.append(pltpu.make_async_copy(lfc_hbm.at[layer, page], lfbuf.at[slot, :, cols], sem.at[slot, 2]))
        return out

    cur = (i % 2) * per_step
    nxt = per_step - cur
    first = base + i * per_step

    @pl.when(i == 0)
    def _():
        for j in range(per_step):
            for cp in copies(base + j, j):
                cp.start()

    for j in range(per_step):
        for cp in copies(first + j, cur + j):
            cp.wait()

    @pl.when(i + 1 < n_steps)
    def _():
        for j in range(per_step):
            for cp in copies(first + per_step + j, nxt + j):
                cp.start()

    return [(first + j, cur + j) for j in range(per_step)]


def _ffn_decode_body(*refs, fc, n_pages, layer, per_step, base, mix):
    pt_ref, refs = refs[0], refs[1:]
    if mix:
        h_ref, ya_ref, yb_ref, yc_ref, ydp_ref, wo_ref, mpost_ref = refs[:7]
        refs = refs[7:]
    else:
        x_ref, refs = refs[0], refs[1:]
    (pre_ref, post_ref, wg_ref, wu_ref, wd_ref, q_ref, kn_ref, vn_ref, lfn_ref, kc_hbm, vc_hbm, lfc_hbm,
     o_ref, yd_ref, acc_ref, kbuf, vbuf, lfbuf, sem) = refs
    pairs = _decode_gather(pt_ref, kc_hbm, vc_hbm, lfc_hbm, kbuf, vbuf, lfbuf, sem,
                           layer=layer, n_pages=n_pages, per_step=per_step, base=base)
    if mix:
        ycat = jnp.concatenate([ya_ref[...], yb_ref[...], yc_ref[...], ydp_ref[...]], axis=1)
        x = h_ref[...] + _rms(_dot(ycat.astype(BF16), wo_ref[0]), mpost_ref[...])
    else:
        x = x_ref[...]
    o_ref[...] = _swiglu_residual(x, pre_ref[...], post_ref[...],
                                  wg_ref.at[0], wu_ref.at[0], wd_ref.at[0], acc_ref, fc)
    for j, (sample, slot) in enumerate(pairs):
        yd_ref[j] = _decode_sample(sample, slot, q_ref, kn_ref, vn_ref, lfn_ref, kbuf, vbuf, lfbuf, n_pages)


def _ffn_decode_call(layer, x, mix_in, pre_g, post_g, wg, wu, wd, page_table, q, kn, vn, lfn, kc, vc, lfc,
                     base, n_dec, tm, fc=256):
    m, d = x.shape
    f = wg.shape[2]
    n, n_pages = page_table.shape
    steps = m // tm
    assert n_dec % steps == 0 and base + n_dec <= n
    per_step = n_dec // steps
    past = n_pages * PAGE_SIZE
    const = lambda shape: pl.BlockSpec(shape, lambda i, pt: (0,) * len(shape), pipeline_mode=pl.Buffered(1))
    lyr = lambda shape: pl.BlockSpec((1,) + shape, lambda i, pt: (layer, 0, 0), pipeline_mode=pl.Buffered(1))
    rows = lambda w: pl.BlockSpec((tm, w), lambda i, pt: (i, 0))
    anyspace = pl.BlockSpec(memory_space=pl.ANY)
    mix = mix_in is not None
    lead_specs, lead_args = [rows(d)], [x]
    if mix:
        ya, yb, yc, yd, wo, mpost = mix_in
        lead_specs += [rows(GROUP_W)] * 4 + [lyr((d, d)), const((1, d))]
        lead_args += [ya, yb, yc, yd, wo, mpost]
    grid_spec = pltpu.PrefetchScalarGridSpec(
        num_scalar_prefetch=1,
        grid=(steps,),
        in_specs=lead_specs + [const((1, d)), const((1, d)), lyr((d, f)), lyr((d, f)), lyr((f, d)),
                               const((n, VD)), const((n, VD)), const((n, VD)), const((n, LANES)),
                               anyspace, anyspace, anyspace],
        out_specs=[rows(d), pl.BlockSpec((per_step, 1, VD), lambda i, pt: (i, 0, 0))],
        scratch_shapes=[pltpu.VMEM((tm, d), F32),
                        pltpu.VMEM((2 * per_step, n_pages, VD, PAGE_SIZE), F32),
                        pltpu.VMEM((2 * per_step, n_pages, VD, PAGE_SIZE), F32),
                        pltpu.VMEM((2 * per_step, SUBLANES, past), F32),
                        pltpu.SemaphoreType.DMA((2 * per_step, 3))],
    )
    return pl.pallas_call(
        functools.partial(_ffn_decode_body, fc=fc, n_pages=n_pages, layer=layer, per_step=per_step,
                          base=base, mix=mix),
        grid_spec=grid_spec,
        out_shape=[jax.ShapeDtypeStruct((m, d), F32), jax.ShapeDtypeStruct((n_dec, 1, VD), F32)],
        compiler_params=_cparams("arbitrary"),
        name="mix_ffn_decode" if mix else "ffn_decode",
    )(page_table, *lead_args, pre_g, post_g, wg, wu, wd, q, kn, vn, lfn, kc, vc, lfc)


ATTN_TILE = 256


def _row_tile(m, cap=512):
    tm = cap
    while tm >= SUBLANES:
        if m % tm == 0:
            return tm
        tm //= 2
    raise ValueError(f"row count {m} is not a multiple of {SUBLANES}")


def kernel(x_prompt, x_sample, state_conv_a, state_gla, state_conv_c, cache_k, cache_v, cache_logf, page_table, ffn1_pre_g, ffn1_post_g, ffn1_w_gate, ffn1_w_up, ffn1_w_down, mix_pre_g, mix_post_g, w_in, w_out, a_conv_w, a_conv_b, a_ln_g, a_ln_b, b_gate_w2, b_gate_b, b_out_norm_g, c_conv_w, d_forget_b, ffn2_pre_g, ffn2_post_g, ffn2_w_gate, ffn2_w_up, ffn2_w_down):
    depth = w_in.shape[0]
    bp, seq, d = x_prompt.shape
    bd = x_sample.shape[0]
    n_pool = cache_k.shape[1]
    assert x_sample.shape[1] == 1 and d == D_MODEL
    assert seq % ATTN_TILE == 0 and seq % GLA_CHUNK == 0 and seq >= CONV_A_W - 1

    hp = x_prompt.reshape(bp * seq, d)
    hs = x_sample.reshape(bd, d)
    tmp = _row_tile(bp * seq)
    tms = _row_tile(bd)
    tq = ATTN_TILE

    off_b = N_PA
    off_c = off_b + 2 * KD + 2 * VD + GLA_RANK
    off_d = off_c + N_PC
    row = lambda v: v.reshape(1, -1)

    lanes_of = lambda v, n=LANES: jnp.broadcast_to(v[:, None], (v.shape[0], n))

    ck = cache_k.transpose(0, 1, 3, 4, 2).reshape(depth, n_pool, VD, PAGE_SIZE)
    cv = cache_v.transpose(0, 1, 3, 4, 2).reshape(depth, n_pool, VD, PAGE_SIZE)
    sca = state_conv_a.transpose(0, 2, 1, 3)
    sgl = state_gla.transpose(0, 2, 3, 4, 1).reshape(depth, KD, HEAD_DIM, bd)
    scc = state_conv_c.reshape(depth, bd, (CONV_C_W - 1) * GROUP_W)
    w_in_t = w_in.transpose(0, 2, 1)

    lf_t = jnp.pad(cache_logf.transpose(0, 1, 3, 2), ((0, 0), (0, 0), (0, SUBLANES - N_HEADS), (0, 0)))
    lf_rows = depth * n_pool * SUBLANES
    lf_pages = _logf_pages_call(lf_t.reshape(lf_rows, PAGE_SIZE), _row_tile(lf_rows, cap=4096))
    lf_pages = lf_pages.reshape(depth, n_pool, SUBLANES, PAGE_SIZE)

    ffn1_ws = tuple(w.astype(BF16) for w in (ffn1_w_gate, ffn1_w_up, ffn1_w_down))
    ffn2_ws = tuple(w.astype(BF16) for w in (ffn2_w_gate, ffn2_w_up, ffn2_w_down))
    wo = w_out.astype(BF16)

    outs = [[] for _ in range(12)]
    for l in range(depth):
        wt = w_in_t[l].astype(BF16)
        pad_rows = lambda w, n: jnp.pad(w, ((0, n - w.shape[0]), (0, 0)))
        wq, wk, wv = (wt[off_d + i * VD:off_d + (i + 1) * VD] for i in range(3))
        in_w = (wt[0:off_b], pad_rows(wt[off_b:off_c], N_PB), wt[off_c:off_d],
                wq, wk, wv, pad_rows(wt[off_d + 3 * VD:], LANES))
        slabs = lambda w: jnp.pad(w.reshape(N_HEADS, HEAD_DIM, d),
                                  ((0, 0), (0, EXT - HEAD_DIM), (0, 0))).reshape(N_HEADS * EXT, d)
        in_w_prompt = in_w[0:3] + (wq, slabs(wk), wk, wv, in_w[6])
        bf_pad = jnp.pad(d_forget_b[l], (0, LANES - N_HEADS))
        bfr = row(bf_pad)
        bfc = lanes_of(bf_pad[0:SUBLANES])
        wa2 = jnp.pad(b_gate_w2[l], ((0, LANES - GLA_RANK), (0, 0))).astype(BF16)
        mix_w = (a_conv_w[l], row(a_conv_b[l]), row(a_ln_g[l]), row(a_ln_b[l]))
        ffn1_w = (row(ffn1_pre_g[l]), row(ffn1_post_g[l])) + ffn1_ws
        ffn2_w = (row(ffn2_pre_g[l]), row(ffn2_post_g[l])) + ffn2_ws

        hs = _ffn_call(l, hs, *ffn1_w, tm=tms)
        s_pa, s_pbt, s_pc, s_q, s_k, s_v, s_lf, s_kt, s_vt, s_lft = _inproj_sample_call(
            hs, row(mix_pre_g[l]), in_w, bfc, bfr)
        s_ya, s_yb, s_yc, s_bufa, s_state, s_bufc = _mix_sample_call(
            l, s_pa, s_pbt, s_pc, sca, sgl, scc, *mix_w, wa2.T, lanes_of(b_gate_b[l], bd),
            lanes_of(b_out_norm_g[l], bd), c_conv_w[l])

        dec_args = (page_table, s_q, s_k, s_v, s_lf, ck, cv, lf_pages)
        half = bd // 2
        hp, s_yd0 = _ffn_decode_call(l, hp, None, *ffn1_w, *dec_args, base=0, n_dec=half, tm=tmp)
        pa, pb, pc, kt, vt, lft, qe, ke, vtb = _inproj_prompt_call(
            hp, row(mix_pre_g[l]), in_w_prompt, bfc, bp, seq, tm=tmp)
        r3 = lambda t: t.reshape(bp, seq, t.shape[-1])
        ya, yc, buf_a, buf_c = _conv_prompt_call(r3(pa), r3(pc), *mix_w, c_conv_w[l])
        yb, s_b = _gla_prompt_call(r3(pb), wa2, row(b_gate_b[l]), row(b_out_norm_g[l]))
        yd = _fox_prompt_call(qe, ke, vtb, tq)
        f2 = lambda t: t.reshape(bp * seq, t.shape[-1])
        hp, s_yd1 = _ffn_decode_call(l, hp, (f2(ya), f2(yb), f2(yc), f2(yd), wo, row(mix_post_g[l])),
                                     *ffn2_w, *dec_args, base=half, n_dec=bd - half, tm=tmp)
        for i, t in enumerate((buf_a, s_b, buf_c, kt, vt, lft[:, 0:N_HEADS, :])):
            outs[i].append(t)

        s_yd = jnp.concatenate([s_yd0, s_yd1], axis=0).reshape(bd, VD)
        hs = _mix_ffn_call(l, hs, s_ya, s_yb, s_yc, s_yd, wo, row(mix_post_g[l]), *ffn2_w, tm=tms)
        for i, t in enumerate((s_bufa, s_state, s_bufc, s_kt, s_vt, s_lft[0:N_HEADS, :])):
            outs[6 + i].append(t)

    p_ca, p_gla, p_cc, p_kt, p_vt, p_lft, s_ca, s_gla, s_cc, s_kt, s_vt, s_lft = (jnp.stack(o) for o in outs)
    heads = lambda t: t.reshape(t.shape[:-2] + (N_HEADS, HEAD_DIM, t.shape[-1]))
    return (hp.reshape(bp, seq, d), hs.reshape(bd, 1, d),
            p_ca, p_gla, p_cc,
            heads(p_kt).transpose(0, 1, 4, 2, 3), heads(p_vt).transpose(0, 1, 4, 2, 3),
            p_lft.transpose(0, 1, 3, 2),
            s_ca.transpose(0, 2, 1, 3),
            s_gla.reshape(depth, N_HEADS, DK_B, HEAD_DIM, bd).transpose(0, 4, 1, 2, 3),
            s_cc.reshape(depth, bd, CONV_C_W - 1, GROUP_W),
            heads(s_kt).transpose(0, 3, 1, 2)[:, :, None], heads(s_vt).transpose(0, 3, 1, 2)[:, :, None],
            s_lft.transpose(0, 2, 1)[:, :, None])
```

```python
import functools

import jax
import jax.numpy as jnp
from jax import lax
from jax.experimental import pallas as pl
from jax.experimental.pallas import tpu as pltpu

F32 = jnp.float32
BF16 = jnp.bfloat16

D_MODEL = 1024
GROUP_W = D_MODEL // 4
HEAD_DIM = 64
N_HEADS = GROUP_W // HEAD_DIM
DK_B = HEAD_DIM // 2
GLA_RANK = 16
GLA_TAU = 16.0
GLA_CHUNK = 64
CONV_A_W = 31
CONV_C_W = 3
PAGE_SIZE = 128
EPS = 1e-6
NEG_INF = -1e30

LANES = 128
SUBLANES = 8
VMEM_LIMIT_BYTES = 56 * 1024 * 1024

N_PA = 2 * GROUP_W
KD = N_HEADS * DK_B
VD = N_HEADS * HEAD_DIM
N_PB = 2 * KD + 2 * VD + LANES
N_PC = 3 * GROUP_W


def _cparams(*sem):
    return pltpu.CompilerParams(dimension_semantics=sem, vmem_limit_bytes=VMEM_LIMIT_BYTES)


def _dot(a, b):
    return jnp.dot(a, b, preferred_element_type=F32)


def _dot_nt(a, b):
    return lax.dot_general(a, b, (((1,), (1,)), ((), ())), preferred_element_type=F32)


def _split3(x):
    hi = x.astype(BF16)
    r = x - hi.astype(F32)
    mid = r.astype(BF16)
    lo = (r - mid.astype(F32)).astype(BF16)
    return hi, mid, lo


def _dot_sel(x, sel):
    hi, mid, lo = _split3(x)
    return _dot(hi, sel) + _dot(mid, sel) + _dot(lo, sel)


def _sel_dot(sel, x):
    hi, mid, lo = _split3(x)
    return _dot(sel, hi) + _dot(sel, mid) + _dot(sel, lo)


def _rms(x, g):
    return x * lax.rsqrt(jnp.mean(x * x, axis=-1, keepdims=True) + EPS) * g


def _sigmoid(x):
    return 1.0 / (1.0 + jnp.exp(-x))


def _silu(x):
    return x * _sigmoid(x)


def _log_sigmoid(x):
    return jnp.minimum(x, 0.0) - jnp.log1p(jnp.exp(-jnp.abs(x)))


def _iota(shape, dim):
    return lax.broadcasted_iota(jnp.int32, shape, dim)


def _idiv(x, n):
    assert n & (n - 1) == 0
    return lax.shift_right_logical(x, n.bit_length() - 1)


def _imod(x, n):
    assert n & (n - 1) == 0
    return x & (n - 1)


def _head_mean_matrix():
    r = _idiv(_iota((VD, VD), 0), HEAD_DIM)
    c = _idiv(_iota((VD, VD), 1), HEAD_DIM)
    return (r == c).astype(BF16)


def _swiglu_residual(x, pre_g, post_g, wg_ref, wu_ref, wd_ref, acc_ref, fc):
    xn = _rms(x, pre_g).astype(BF16)
    for c in range(wg_ref.shape[1] // fc):
        sl = slice(c * fc, (c + 1) * fc)
        g = _dot(xn, wg_ref[:, sl])
        u = _dot(xn, wu_ref[:, sl])
        hid = (_silu(g) * u).astype(BF16)
        part = _dot(hid, wd_ref[sl, :])
        if c == 0:
            acc_ref[...] = part
        else:
            acc_ref[...] += part
    return x + 0.5 * _rms(acc_ref[...], post_g)


def _ffn_body(x_ref, pre_ref, post_ref, wg_ref, wu_ref, wd_ref, o_ref, acc_ref, *, fc):
    o_ref[...] = _swiglu_residual(x_ref[...], pre_ref[...], post_ref[...],
                                  wg_ref.at[0], wu_ref.at[0], wd_ref.at[0], acc_ref, fc)


def _mix_ffn_body(h_ref, ya_ref, yb_ref, yc_ref, yd_ref, wo_ref, mpost_ref,
                  pre_ref, post_ref, wg_ref, wu_ref, wd_ref, o_ref, acc_ref, *, fc):
    ycat = jnp.concatenate([ya_ref[...], yb_ref[...], yc_ref[...], yd_ref[...]], axis=1)
    y = _dot(ycat.astype(BF16), wo_ref[0])
    h = h_ref[...] + _rms(y, mpost_ref[...])
    o_ref[...] = _swiglu_residual(h, pre_ref[...], post_ref[...],
                                  wg_ref.at[0], wu_ref.at[0], wd_ref.at[0], acc_ref, fc)


def _row_spec(tm, n):
    return pl.BlockSpec((tm, n), lambda i: (i, 0))


def _full_spec(shape):
    return pl.BlockSpec(shape, lambda i: (0,) * len(shape), pipeline_mode=pl.Buffered(1))


def _ffn_call(layer, x, pre_g, post_g, wg, wu, wd, tm, fc=256):
    m, d = x.shape
    return pl.pallas_call(
        functools.partial(_ffn_body, fc=fc),
        grid=(m // tm,),
        in_specs=[_row_spec(tm, d), _full_spec((1, d)), _full_spec((1, d)),
                  _layer_spec(wg.shape, layer), _layer_spec(wu.shape, layer), _layer_spec(wd.shape, layer)],
        out_specs=_row_spec(tm, d),
        out_shape=jax.ShapeDtypeStruct((m, d), F32),
        scratch_shapes=[pltpu.VMEM((tm, d), F32)],
        compiler_params=_cparams("arbitrary"),
        name="ffn",
    )(x, pre_g, post_g, wg, wu, wd)


def _mix_ffn_call(layer, h, ya, yb, yc, yd, wo, mpost, pre_g, post_g, wg, wu, wd, tm, fc=256):
    m, d = h.shape
    return pl.pallas_call(
        functools.partial(_mix_ffn_body, fc=fc),
        grid=(m // tm,),
        in_specs=[_row_spec(tm, d)] + [_row_spec(tm, GROUP_W)] * 4
                 + [_layer_spec(wo.shape, layer), _full_spec((1, d)), _full_spec((1, d)), _full_spec((1, d)),
                    _layer_spec(wg.shape, layer), _layer_spec(wu.shape, layer), _layer_spec(wd.shape, layer)],
        out_specs=_row_spec(tm, d),
        out_shape=jax.ShapeDtypeStruct((m, d), F32),
        scratch_shapes=[pltpu.VMEM((tm, d), F32)],
        compiler_params=_cparams("arbitrary"),
        name="mix_ffn",
    )(h, ya, yb, yc, yd, wo, mpost, pre_g, post_g, wg, wu, wd)


EXT = LANES
C_KEY = HEAD_DIM
C_QRY = HEAD_DIM + 3


def _inproj_prompt_body(h_ref, g_ref, wa_ref, wb_ref, wc_ref, wq_ref, wke_ref, wk_ref, wv_ref, wl_ref,
                        bfc_ref, pa_ref, pb_ref, pc_ref, kt_ref, vt_ref, lft_ref,
                        qe_ref, ke_ref, vtb_ref, carry_ref, *, per):
    tm = h_ref.shape[0]
    u = _rms(h_ref[...], g_ref[...]).astype(BF16)
    pa_ref[...] = _dot_nt(u, wa_ref[...])
    pb_ref[...] = _dot_nt(u, wb_ref[...])
    pc_ref[...] = _dot_nt(u, wc_ref[...])
    kt_ref[0] = _dot_nt(wk_ref[...], u)
    vt = _dot_nt(wv_ref[...], u)
    vt_ref[0] = vt
    vtb_ref[0] = vt.astype(BF16)
    lft = _log_sigmoid(_dot_nt(wl_ref[0:SUBLANES, :], u) + bfc_ref[:, 0:1])
    lft_ref[0] = lft

    first = pl.program_id(0) % per == 0
    carry = jnp.where(first, 0.0, carry_ref[:, 0:1])
    upto = (_iota((tm, tm), 0) <= _iota((tm, tm), 1)).astype(BF16)
    lfm = jnp.where(_iota((SUBLANES, tm), 0) < N_HEADS, lft, 0.0)
    c = _dot_sel(lfm, upto) + carry
    carry_ref[...] = jnp.broadcast_to(c[:, tm - 1:tm], carry_ref.shape)
    ccol = jnp.concatenate([c, jnp.zeros((LANES - SUBLANES, tm), F32)], axis=0).T

    hi, mid, lo = _split3(ccol)
    pieces = (hi.astype(F32) + pltpu.roll(mid.astype(F32), N_HEADS, 1)
              + pltpu.roll(lo.astype(F32), 2 * N_HEADS, 1)).astype(BF16)
    r = _iota((LANES, N_HEADS * EXT), 0)
    col = _iota((LANES, N_HEADS * EXT), 1)
    j = _imod(col, EXT) - C_KEY
    place_k = ((j >= 0) & (j < 3) & (r == N_HEADS * j + _idiv(col, EXT))).astype(BF16)
    lane = _imod(_iota((1, N_HEADS * EXT), 1), EXT)
    ones_k = ((lane >= C_QRY) & (lane < C_QRY + 3)).astype(F32)
    ke = _dot_nt(u, wke_ref[...]) + _dot(pieces, place_k) + ones_k
    for h in range(N_HEADS):
        ke_ref[0, h] = ke[:, h * EXT:(h + 1) * EXT].astype(BF16)

    hi, mid, lo = _split3(c)
    pieces_t = jnp.concatenate([hi.astype(F32), mid.astype(F32), lo.astype(F32),
                                jnp.zeros((LANES - 3 * SUBLANES, tm), F32)], axis=0).astype(BF16)
    r = _iota((VD, LANES), 0)
    col = _iota((VD, LANES), 1)
    j = _imod(r, HEAD_DIM) - (C_QRY - HEAD_DIM)
    place_q = ((j >= 0) & (j < 3) & (col == SUBLANES * j + _idiv(r, HEAD_DIM))).astype(BF16)
    rowi = _imod(_iota((VD, 1), 0), HEAD_DIM)
    neg_q = jnp.where(rowi < 3, -1.0, 0.0)
    extra = (_dot(place_q, pieces_t) + neg_q).astype(BF16)
    qt = (_dot_nt(wq_ref[...], u) * (HEAD_DIM ** -0.5)).astype(BF16)
    for h in range(N_HEADS):
        rows = slice(h * HEAD_DIM, (h + 1) * HEAD_DIM)
        qe_ref[0, h, 0:HEAD_DIM, :] = qt[rows, :]
        qe_ref[0, h, HEAD_DIM:, :] = extra[rows, :]


def _inproj_sample_body(h_ref, g_ref, wa_ref, wb_ref, wc_ref, wq_ref, wk_ref, wv_ref, wl_ref,
                        bfc_ref, bfr_ref,
                        pa_ref, pbt_ref, pc_ref, q_ref, k_ref, v_ref, lf_ref, kt_ref, vt_ref, lft_ref, qt_ref):
    u = _rms(h_ref[...], g_ref[...]).astype(BF16)
    pa_ref[...] = _dot_nt(u, wa_ref[...])
    pbt_ref[...] = _dot_nt(wb_ref[...], u)
    pc_ref[...] = _dot_nt(u, wc_ref[...])
    q_ref[...] = _dot_nt(u, wq_ref[...])
    qt_ref[...] = _dot_nt(wq_ref[...], u)
    k_ref[...] = _dot_nt(u, wk_ref[...])
    v_ref[...] = _dot_nt(u, wv_ref[...])
    lf_ref[...] = _log_sigmoid(_dot_nt(u, wl_ref[...]) + bfr_ref[...])
    kt_ref[...] = _dot_nt(wk_ref[...], u)
    vt_ref[...] = _dot_nt(wv_ref[...], u)
    lft_ref[...] = _log_sigmoid(_dot_nt(wl_ref[0:SUBLANES, :], u) + bfc_ref[:, 0:1])


def _inproj_weight_specs(d):
    return [_full_spec((n, d)) for n in (N_PA, N_PB, N_PC, VD, VD, VD, LANES)]


def _inproj_prompt_call(h, g, ws, bfc, bp, seq, tm):
    m, d = h.shape
    per = seq // tm
    tspec = lambda n: pl.BlockSpec((1, n, tm), lambda i: (i // per, 0, i % per))
    hx = N_HEADS * EXT
    return pl.pallas_call(
        functools.partial(_inproj_prompt_body, per=per),
        grid=(m // tm,),
        in_specs=[_row_spec(tm, d), _full_spec((1, d))]
                 + [_full_spec((n, d)) for n in (N_PA, N_PB, N_PC, VD, hx, VD, VD, LANES)]
                 + [_full_spec((SUBLANES, LANES))],
        out_specs=[_row_spec(tm, N_PA), _row_spec(tm, N_PB), _row_spec(tm, N_PC),
                   tspec(VD), tspec(VD), tspec(SUBLANES),
                   pl.BlockSpec((1, N_HEADS, EXT, tm), lambda i: (i // per, 0, 0, i % per)),
                   pl.BlockSpec((1, N_HEADS, tm, EXT), lambda i: (i // per, 0, i % per, 0)),
                   tspec(VD)],
        out_shape=[jax.ShapeDtypeStruct((m, N_PA), F32), jax.ShapeDtypeStruct((m, N_PB), F32),
                   jax.ShapeDtypeStruct((m, N_PC), F32),
                   jax.ShapeDtypeStruct((bp, VD, seq), F32), jax.ShapeDtypeStruct((bp, VD, seq), F32),
                   jax.ShapeDtypeStruct((bp, SUBLANES, seq), F32),
                   jax.ShapeDtypeStruct((bp, N_HEADS, EXT, seq), BF16),
                   jax.ShapeDtypeStruct((bp, N_HEADS, seq, EXT), BF16),
                   jax.ShapeDtypeStruct((bp, VD, seq), BF16)],
        scratch_shapes=[pltpu.VMEM((SUBLANES, LANES), F32)],
        compiler_params=_cparams("arbitrary"),
        name="inproj_prompt",
    )(h, g, *ws, bfc)


def _inproj_sample_call(h, g, ws, bfc, bfr):
    n, d = h.shape
    shapes = [(n, N_PA), (N_PB, n), (n, N_PC), (n, VD), (n, VD), (n, VD), (n, LANES),
              (VD, n), (VD, n), (SUBLANES, n), (VD, n)]
    return pl.pallas_call(
        _inproj_sample_body,
        grid=(1,),
        in_specs=[_full_spec((n, d)), _full_spec((1, d))] + _inproj_weight_specs(d)
                 + [_full_spec((SUBLANES, LANES)), _full_spec((1, LANES))],
        out_specs=[pl.BlockSpec(s, lambda i: (0, 0)) for s in shapes],
        out_shape=[jax.ShapeDtypeStruct(s, F32) for s in shapes],
        compiler_params=_cparams("arbitrary"),
        name="inproj_sample",
    )(h, g, *ws, bfc, bfr)


A_PAD = 32
C_PAD = 8
CONV_ROWS = 128


def _conv_prompt_body(pa_ref, pc_ref, aw_ref, ab_ref, lg_ref, lb_ref, cw_ref,
                      ya_ref, yc_ref, bufa_ref, bufc_ref, apad_ref, zpad_ref):
    seq = pa_ref.shape[1]
    c = GROUP_W
    apad_ref[0:A_PAD, :] = jnp.zeros((A_PAD, c), F32)
    apad_ref[A_PAD:, :] = pa_ref[0, :, 0:c] * _sigmoid(pa_ref[0, :, c:])
    zpad_ref[0:C_PAD, :] = jnp.zeros((C_PAD, c), F32)
    zpad_ref[C_PAD:, :] = pc_ref[0, :, c:2 * c] * pc_ref[0, :, 2 * c:]

    def step(i, carry):
        r0 = pl.multiple_of(i * CONV_ROWS, CONV_ROWS)
        win = apad_ref[pl.ds(r0, CONV_ROWS + A_PAD), :]
        acc = jnp.zeros((CONV_ROWS, c), F32) + ab_ref[...]
        for r in range(SUBLANES):
            nrow = CONV_ROWS if r == 0 else CONV_ROWS + SUBLANES
            u = None
            for a8 in range(0, A_PAD + 1, SUBLANES):
                j = a8 + r - (A_PAD - (CONV_A_W - 1))
                if 0 <= j < CONV_A_W:
                    term = aw_ref[j:j + 1, :] * win[a8:a8 + nrow, :]
                    u = term if u is None else u + term
            acc = acc + u[r:r + CONV_ROWS, :]
        mu = jnp.mean(acc, axis=-1, keepdims=True)
        var = jnp.mean(jnp.square(acc - mu), axis=-1, keepdims=True)
        yn = (acc - mu) * lax.rsqrt(var + EPS) * lg_ref[...] + lb_ref[...]
        ya_ref[0, pl.ds(r0, CONV_ROWS), :] = _silu(yn)
        zwin = zpad_ref[pl.ds(r0, CONV_ROWS + C_PAD), :]
        accc = jnp.zeros((CONV_ROWS, c), F32)
        for j in range(CONV_C_W):
            off = C_PAD - (CONV_C_W - 1) + j
            accc = accc + cw_ref[j:j + 1, :] * zwin[off:off + CONV_ROWS, :]
        yc_ref[0, pl.ds(r0, CONV_ROWS), :] = pc_ref[0, pl.ds(r0, CONV_ROWS), 0:c] * accc
        return carry

    lax.fori_loop(0, seq // CONV_ROWS, step, 0)
    na = CONV_A_W - 1
    nc = CONV_C_W - 1
    bufa_ref[0] = apad_ref[seq:A_PAD + seq, :][A_PAD - na:, :]
    bufc_ref[0] = zpad_ref[seq:C_PAD + seq, :][C_PAD - nc:, :]


def _conv_prompt_call(pa, pc, aw, ab, lg, lb, cw):
    b, seq, _ = pa.shape
    c = GROUP_W
    bspec = lambda n: pl.BlockSpec((1, seq, n), lambda i: (i, 0, 0))
    return pl.pallas_call(
        _conv_prompt_body,
        grid=(b,),
        in_specs=[bspec(N_PA), bspec(N_PC), _full_spec((CONV_A_W, c)), _full_spec((1, c)),
                  _full_spec((1, c)), _full_spec((1, c)), _full_spec((CONV_C_W, c))],
        out_specs=[bspec(c), bspec(c),
                   pl.BlockSpec((1, CONV_A_W - 1, c), lambda i: (i, 0, 0)),
                   pl.BlockSpec((1, CONV_C_W - 1, c), lambda i: (i, 0, 0))],
        out_shape=[jax.ShapeDtypeStruct((b, seq, c), F32), jax.ShapeDtypeStruct((b, seq, c), F32),
                   jax.ShapeDtypeStruct((b, CONV_A_W - 1, c), F32),
                   jax.ShapeDtypeStruct((b, CONV_C_W - 1, c), F32)],
        scratch_shapes=[pltpu.VMEM((A_PAD + seq, c), F32), pltpu.VMEM((C_PAD + seq, c), F32)],
        compiler_params=_cparams("arbitrary"),
        name="conv_prompt",
    )(pa, pc, aw, ab, lg, lb, cw)


def _gla_gate(alr, wa2_ref, ba_ref):
    return _log_sigmoid(_dot(alr.astype(BF16), wa2_ref[...]) + ba_ref[...]) * (1.0 / GLA_TAU)


def _gla_out(o, g, ng, hm):
    ms = _dot_sel(o * o, hm) * (1.0 / HEAD_DIM)
    return o * lax.rsqrt(ms + EPS) * ng * _silu(g)


GLA_SLAB = 512


def _group_row(x, g, r):
    n, c = x.shape
    x3 = x.reshape(n // g, g, c)
    return jnp.broadcast_to(x3[:, r:r + 1, :], (n // g, g, c)).reshape(n, c)


def _gla_prompt_body(pb_ref, wa2_ref, ba_ref, ng_ref, y_ref, st_ref, la_ref, s_ref):
    seq = pb_ref.shape[1]
    ck = GLA_CHUNK
    la_ref[...] = _gla_gate(pb_ref[0, :, 2 * KD + 2 * VD:], wa2_ref, ba_ref)
    s_ref[...] = jnp.zeros((KD, VD), F32)

    rs = min(GLA_SLAB, seq)
    nch = rs // ck
    tt = _iota((rs, rs), 0)
    ss = _iota((rs, rs), 1)
    tri = ((ss <= tt) & (_idiv(ss, ck) == _idiv(tt, ck))).astype(BF16)
    blk = _idiv(_iota((KD, VD), 0), DK_B) == _idiv(_iota((KD, VD), 1), HEAD_DIM)
    blk_bf = blk.astype(BF16)
    blk_f = blk.astype(F32)
    hm = _head_mean_matrix()
    scale = DK_B ** -0.5
    levels = [g for g in (2 * SUBLANES, 4 * SUBLANES, 8 * SUBLANES) if g <= ck]
    assert ck == 8 * SUBLANES
    key_head = _idiv(_iota((1, KD), 1), DK_B)
    val_head = _idiv(_iota((1, VD), 1), HEAD_DIM)
    tloc = _iota((rs, KD), 0)

    def slab(si, carry):
        r0 = pl.multiple_of(si * rs, rs)
        rows = pl.ds(r0, rs)
        q = pb_ref[0, rows, 0:KD] * scale
        k = pb_ref[0, rows, KD:2 * KD]
        v = pb_ref[0, rows, 2 * KD:2 * KD + VD]
        g = pb_ref[0, rows, 2 * KD + VD:2 * KD + 2 * VD]
        b = _sel_dot(tri, la_ref[rows, :])
        vb3 = v.astype(BF16).reshape(nch, ck, VD)

        t8 = _imod(tloc, SUBLANES)
        acc = jnp.zeros((rs, VD), F32)
        for s in range(SUBLANES):
            diff = jnp.where(t8 >= s, b - _group_row(b, SUBLANES, s), -jnp.inf)
            e = (q * _group_row(k, SUBLANES, s) * jnp.exp(diff)).astype(BF16)
            acc = acc + _dot(e, blk_bf) * _group_row(v, SUBLANES, s)

        sc = jnp.zeros((nch, N_HEADS * ck, ck), F32)
        for gsz in levels:
            r = _group_row(b, gsz, gsz // 2)
            upper = _imod(tloc, gsz) >= gsz // 2
            qf = q * jnp.exp(jnp.where(upper, b - r, -jnp.inf))
            kf = (k * jnp.exp(jnp.where(upper, -jnp.inf, r - b))).astype(BF16).reshape(nch, ck, KD)
            qstack = jnp.concatenate(
                [jnp.where(key_head == h, qf, 0.0).astype(BF16).reshape(nch, ck, KD)
                 for h in range(N_HEADS)], axis=1)
            sc_g = jnp.einsum('cqk,csk->cqs', qstack, kf, preferred_element_type=F32)
            same_group = _idiv(_imod(_iota((N_HEADS * ck, ck), 0), ck), gsz) == _idiv(_iota((N_HEADS * ck, ck), 1), gsz)
            sc = sc + jnp.where(same_group[None], sc_g, 0.0)
        res = jnp.einsum('cqs,csv->cqv', sc.astype(BF16), vb3, preferred_element_type=F32)
        for h in range(N_HEADS):
            part = res[:, h * ck:(h + 1) * ck, :].reshape(rs, VD)
            acc = acc + jnp.where(val_head == h, part, 0.0)

        b_last = _group_row(b, ck, ck - 1)
        kdec = k * jnp.exp(b_last - b)
        lastb = jnp.concatenate([b[(c + 1) * ck - 1:(c + 1) * ck, :] for c in range(nch)]
                                + [jnp.zeros((LANES - nch, KD), F32)], axis=0)
        dcols = jnp.exp(lastb.T)
        s_cur = s_ref[...]
        s_before = []
        for c in range(nch):
            s_before.append(s_cur.astype(BF16))
            crow = slice(c * ck, (c + 1) * ck)
            upd = _dot(kdec[crow, :].T.astype(BF16), vb3[c])
            s_cur = dcols[:, c:c + 1] * s_cur + upd * blk_f
        s_ref[...] = s_cur
        qb3 = (q * jnp.exp(b)).astype(BF16).reshape(nch, ck, KD)
        inter = jnp.einsum('cqk,ckv->cqv', qb3, jnp.stack(s_before), preferred_element_type=F32)
        o = inter.reshape(rs, VD) + acc
        y_ref[0, rows, :] = _gla_out(o, g, ng_ref[...], hm)
        return carry

    lax.fori_loop(0, seq // rs, slab, 0)
    for h in range(N_HEADS):
        st_ref[0, h] = s_ref[h * DK_B:(h + 1) * DK_B, h * HEAD_DIM:(h + 1) * HEAD_DIM]


def _gla_prompt_call(pb, wa2, ba, ng):
    b, seq, _ = pb.shape
    return pl.pallas_call(
        _gla_prompt_body,
        grid=(b,),
        in_specs=[pl.BlockSpec((1, seq, N_PB), lambda i: (i, 0, 0)), _full_spec((LANES, KD)),
                  _full_spec((1, KD)), _full_spec((1, VD))],
        out_specs=[pl.BlockSpec((1, seq, VD), lambda i: (i, 0, 0)),
                   pl.BlockSpec((1, N_HEADS, DK_B, HEAD_DIM), lambda i: (i, 0, 0, 0))],
        out_shape=[jax.ShapeDtypeStruct((b, seq, VD), F32),
                   jax.ShapeDtypeStruct((b, N_HEADS, DK_B, HEAD_DIM), F32)],
        scratch_shapes=[pltpu.VMEM((seq, KD), F32), pltpu.VMEM((KD, VD), F32)],
        compiler_params=_cparams("arbitrary"),
        name="gla_prompt",
    )(pb, wa2, ba, ng)


def _fox_prompt_body(qe_ref, ke_ref, vtb_ref, o_ref, *, tq):
    qi = pl.program_id(1)
    key_pos = _iota((tq, tq), 0)
    qry_pos = _iota((tq, tq), 1)

    def tile(ki):
        return pl.ds(pl.multiple_of(ki * tq, tq), tq)

    def consume(tiles, stats):
        scores = [[_dot(ke_ref[0, h, tile(ki), :], qe_ref[0, h]) for h in range(N_HEADS)]
                  for ki, _ in tiles]
        for (ki, masked), sc in zip(tiles, scores):
            soft = []
            for h in range(N_HEADS):
                m, l, _ = stats[h]
                s = sc[h]
                if masked:
                    s = jnp.where(key_pos <= qry_pos, s, NEG_INF)
                m_new = jnp.maximum(m, jnp.max(s, axis=0, keepdims=True))
                alpha = jnp.exp(m - m_new)
                p = jnp.exp(s - m_new)
                soft.append((m_new, alpha * l + jnp.sum(p, axis=0, keepdims=True), alpha, p.astype(BF16)))
            out = []
            for h in range(N_HEADS):
                m_new, l, alpha, p = soft[h]
                vh = vtb_ref[0, h * HEAD_DIM:(h + 1) * HEAD_DIM, tile(ki)]
                out.append((m_new, l, alpha * stats[h][2] + _dot(vh, p)))
            stats = tuple(out)
        return stats

    init = tuple((jnp.full((1, tq), NEG_INF, F32), jnp.zeros((1, tq), F32),
                  jnp.zeros((HEAD_DIM, tq), F32)) for _ in range(N_HEADS))
    stats = lax.fori_loop(0, qi // 2, lambda k, st: consume([(2 * k, False), (2 * k + 1, False)], st), init)
    stats = lax.cond(qi % 2 == 1,
                     lambda st: consume([(qi - 1, False), (qi, True)], st),
                     lambda st: consume([(qi, True)], st), stats)
    o_ref[0] = jnp.concatenate([acc / l for _, l, acc in stats], axis=0).T


def _fox_prompt_call(qe, ke, vtb, tq):
    b, _, _, seq = qe.shape
    return pl.pallas_call(
        functools.partial(_fox_prompt_body, tq=tq),
        grid=(b, seq // tq),
        in_specs=[pl.BlockSpec((1, N_HEADS, EXT, tq), lambda i, j: (i, 0, 0, j)),
                  pl.BlockSpec((1, N_HEADS, seq, EXT), lambda i, j: (i, 0, 0, 0)),
                  pl.BlockSpec((1, VD, seq), lambda i, j: (i, 0, 0))],
        out_specs=pl.BlockSpec((1, tq, VD), lambda i, j: (i, j, 0)),
        out_shape=jax.ShapeDtypeStruct((b, seq, VD), F32),
        compiler_params=_cparams("arbitrary", "arbitrary"),
        name="fox_prompt",
    )(qe, ke, vtb)


def _mix_sample_body(pa_ref, pbt_ref, pc_ref, bufa_ref, sg_ref, bufc_ref,
                     aw_ref, ab_ref, lg_ref, lb_ref, wa2t_ref, bac_ref, ngc_ref, cw_ref,
                     ya_ref, yb_ref, yc_ref, bufa_o, sg_o, bufc_o, q_s, k_s, dec_s):
    c = GROUP_W
    a = pa_ref[:, 0:c] * _sigmoid(pa_ref[:, c:])
    na = CONV_A_W - 1
    acc = aw_ref[na:na + 1, :] * a + ab_ref[...]
    for j in range(na):
        acc = acc + aw_ref[j:j + 1, :] * bufa_ref[0, j]
    mu = jnp.mean(acc, axis=-1, keepdims=True)
    var = jnp.mean(jnp.square(acc - mu), axis=-1, keepdims=True)
    ya_ref[...] = _silu((acc - mu) * lax.rsqrt(var + EPS) * lg_ref[...] + lb_ref[...])
    for j in range(na - 1):
        bufa_o[j] = bufa_ref[0, j + 1]
    bufa_o[na - 1] = a
    z = pc_ref[:, c:2 * c] * pc_ref[:, 2 * c:]
    conv = (cw_ref[0:1, :] * bufc_ref[0, :, 0:c] + cw_ref[1:2, :] * bufc_ref[0, :, c:]
            + cw_ref[2:3, :] * z)
    yc_ref[...] = pc_ref[:, 0:c] * conv
    bufc_o[:, 0:c] = bufc_ref[0, :, c:]
    bufc_o[:, c:] = z
    q_s[...] = pbt_ref[0:KD, :] * (DK_B ** -0.5)
    k_s[...] = pbt_ref[KD:2 * KD, :]
    gate = _dot(wa2t_ref[...], pbt_ref[2 * KD + 2 * VD:, :].astype(BF16)) + bac_ref[...]
    dec_s[...] = jnp.exp(_log_sigmoid(gate) * (1.0 / GLA_TAU))
    n = pa_ref.shape[0]
    ys = []
    for h in range(N_HEADS):
        vrows = slice(2 * KD + h * HEAD_DIM, 2 * KD + (h + 1) * HEAD_DIM)
        grows = slice(2 * KD + VD + h * HEAD_DIM, 2 * KD + VD + (h + 1) * HEAD_DIM)
        vh = pbt_ref[vrows, :]

        def key_step(kk, o, h=h, vh=vh):
            hk = h * DK_B + kk
            one = pl.ds(hk, 1)
            s_new = dec_s[one, :] * sg_ref[0, hk] + k_s[one, :] * vh
            sg_o[hk] = s_new
            return o + q_s[one, :] * s_new

        o = lax.fori_loop(0, DK_B, key_step, jnp.zeros((HEAD_DIM, n), F32))
        ms = jnp.mean(o * o, axis=0, keepdims=True)
        ng = ngc_ref[h * HEAD_DIM:(h + 1) * HEAD_DIM, :]
        ys.append(o * lax.rsqrt(ms + EPS) * ng * _silu(pbt_ref[grows, :]))
    yb_ref[...] = jnp.concatenate(ys, axis=0).T


def _layer_spec(shape, l):
    return pl.BlockSpec((1,) + tuple(shape[1:]), lambda i: (l,) + (0,) * (len(shape) - 1),
                        pipeline_mode=pl.Buffered(1))


def _mix_sample_call(l, pa, pbt, pc, bufa, sg, bufc, aw, ab, lg, lb, wa2t, bac, ngc, cw):
    n = pa.shape[0]
    c = GROUP_W
    small = (aw, ab, lg, lb, wa2t, bac, ngc, cw)
    out_shape = [jax.ShapeDtypeStruct((n, c), F32)] * 3 + [
        jax.ShapeDtypeStruct(bufa.shape[1:], F32), jax.ShapeDtypeStruct(sg.shape[1:], F32),
        jax.ShapeDtypeStruct(bufc.shape[1:], F32)]
    return pl.pallas_call(
        _mix_sample_body,
        grid=(1,),
        in_specs=[_full_spec(pa.shape), _full_spec(pbt.shape), _full_spec(pc.shape),
                  _layer_spec(bufa.shape, l), _layer_spec(sg.shape, l), _layer_spec(bufc.shape, l)]
                 + [_full_spec(x.shape) for x in small],
        out_specs=[pl.BlockSpec(s.shape, lambda i, nd=len(s.shape): (0,) * nd) for s in out_shape],
        out_shape=out_shape,
        scratch_shapes=[pltpu.VMEM((KD, n), F32)] * 3,
        compiler_params=_cparams("arbitrary"),
        name="mix_sample",
    )(pa, pbt, pc, bufa, sg, bufc, *small)


def _logf_pages_body(x_ref, o_ref):
    n = x_ref.shape[0]
    x = x_ref[...]
    later = (_iota((PAGE_SIZE, PAGE_SIZE), 0) > _iota((PAGE_SIZE, PAGE_SIZE), 1)).astype(BF16)
    ones = jnp.ones((PAGE_SIZE, PAGE_SIZE), BF16)
    within = _dot_sel(x, later)
    total = pltpu.roll(_dot_sel(x, ones), N_HEADS, 0)
    o_ref[...] = jnp.where(_imod(_iota((n, PAGE_SIZE), 0), SUBLANES) < N_HEADS, within, total)


def _logf_pages_call(x, tm):
    m = x.shape[0]
    return pl.pallas_call(
        _logf_pages_body,
        grid=(m // tm,),
        in_specs=[_row_spec(tm, PAGE_SIZE)],
        out_specs=_row_spec(tm, PAGE_SIZE),
        out_shape=jax.ShapeDtypeStruct((m, PAGE_SIZE), F32),
        compiler_params=_cparams("arbitrary"),
        name="logf_pages",
    )(x)


def _decode_sample(b, slot, q_ref, qt_ref, kn_ref, vn_ref, lfn_ref, kbuf, vbuf, lfbuf, n_pages):
    past = n_pages * PAGE_SIZE
    scale = HEAD_DIM ** -0.5
    heads = [slice(h * HEAD_DIM, (h + 1) * HEAD_DIM) for h in range(N_HEADS)]
    hm = (_idiv(_iota((SUBLANES, VD), 1), HEAD_DIM) == _iota((SUBLANES, VD), 0)).astype(F32)
    qbd = jnp.broadcast_to(q_ref[pl.ds(b, 1), :], (SUBLANES, VD)) * hm
    onehot = (_iota((1, qt_ref.shape[1]), 1) == b).astype(F32)
    qcol = jnp.sum(qt_ref[...] * onehot, axis=1, keepdims=True) * scale
    qcb = jnp.broadcast_to(qcol, (VD, PAGE_SIZE))
    parts = [[] for _ in heads]
    for p in range(n_pages):
        prod = kbuf[slot, p] * qcb
        for h, rows in enumerate(heads):
            parts[h].append(prod[rows, :].reshape(HEAD_DIM // SUBLANES, SUBLANES, PAGE_SIZE).sum(axis=0))
    rowid = _iota((SUBLANES, past), 0)
    s = jnp.zeros((SUBLANES, past), F32)
    for h in range(N_HEADS):
        t = jnp.concatenate(parts[h], axis=1)
        for sh in (4, 2, 1):
            t = t + pltpu.roll(t, sh, 0)
        s = jnp.where(rowid == h, t, s)
    lf = lfbuf[slot]
    incl = pltpu.roll(lf, N_HEADS, 0)
    sh = PAGE_SIZE
    while sh < past:
        incl = incl + jnp.concatenate([incl[:, sh:], jnp.zeros((SUBLANES, sh), F32)], axis=1)
        sh *= 2
    later_pages = jnp.concatenate([incl[:, PAGE_SIZE:], jnp.zeros((SUBLANES, PAGE_SIZE), F32)], axis=1)
    valid = _iota((SUBLANES, past), 0) < N_HEADS
    logits = jnp.where(valid, s + lf + later_pages, 0.0)
    s_self = jnp.sum(qbd * kn_ref[pl.ds(b, 1), :], axis=1, keepdims=True) * scale
    pick = (_iota((SUBLANES, LANES), 1) == _iota((SUBLANES, LANES), 0)).astype(F32)
    c_new = jnp.sum(pick * lfn_ref[pl.ds(b, 1), :], axis=1, keepdims=True)
    self_logit = jnp.where(_iota((SUBLANES, 1), 0) < N_HEADS, s_self - c_new, 0.0)
    m = jnp.maximum(jnp.max(logits, axis=1, keepdims=True), self_logit)
    p = jnp.exp(logits - m)
    p_self = jnp.exp(self_logit - m)
    l = jnp.sum(p, axis=1, keepdims=True) + p_self
    acc = [jnp.zeros((HEAD_DIM, PAGE_SIZE), F32) for _ in heads]
    for pg in range(n_pages):
        vt = vbuf[slot, pg]
        cols = slice(pg * PAGE_SIZE, (pg + 1) * PAGE_SIZE)
        for h, rows in enumerate(heads):
            acc[h] = acc[h] + vt[rows, :] * jnp.broadcast_to(p[h:h + 1, cols], (HEAD_DIM, PAGE_SIZE))
    ocol = jnp.concatenate([jnp.sum(a, axis=1, keepdims=True) for a in acc], axis=0)
    eye = (_iota((VD, VD), 0) == _iota((VD, VD), 1)).astype(F32)
    orow = jnp.sum(eye * ocol, axis=0, keepdims=True)
    p_self_row = jnp.sum(p_self * hm, axis=0, keepdims=True)
    l_row = jnp.sum(l * hm, axis=0, keepdims=True)
    return (orow + p_self_row * vn_ref[pl.ds(b, 1), :]) / l_row


def _decode_gather(pt_ref, kc_hbm, vc_hbm, lfc_hbm, kbuf, vbuf, lfbuf, sem, *, layer, n_pages, per_step, base):
    i = pl.program_id(0)
    n_steps = pl.num_programs(0)

    def copies(sample, slot):
        out = []
        for p in range(n_pages):
            page = pt_ref[sample, p]
            cols = pl.ds(p * PAGE_SIZE, PAGE_SIZE)
            out.append(pltpu.make_async_copy(kc_hbm.at[layer, page], kbuf.at[slot, p], sem.at[slot, 0]))
            out.append(pltpu.make_async_copy(vc_hbm.at[layer, page], vbuf.at[slot, p], sem.at[slot, 1]))
            out.append(pltpu.make_async_copy(lfc_hbm.at[layer, page], lfbuf.at[slot, :, cols], sem.at[slot, 2]))
        return out

    cur = (i % 2) * per_step
    nxt = per_step - cur
    first = base + i * per_step

    @pl.when(i == 0)
    def _():
        for j in range(per_step):
            for cp in copies(base + j, j):
                cp.start()

    for j in range(per_step):
        for cp in copies(first + j, cur + j):
            cp.wait()

    @pl.when(i + 1 < n_steps)
    def _():
        for j in range(per_step):
            for cp in copies(first + per_step + j, nxt + j):
                cp.start()

    return [(first + j, cur + j) for j in range(per_step)]


def _ffn_decode_body(*refs, fc, n_pages, layer, per_step, base, mix):
    pt_ref, refs = refs[0], refs[1:]
    if mix:
        h_ref, ya_ref, yb_ref, yc_ref, ydp_ref, wo_ref, mpost_ref = refs[:7]
        refs = refs[7:]
    else:
        x_ref, refs = refs[0], refs[1:]
    (pre_ref, post_ref, wg_ref, wu_ref, wd_ref, q_ref, qt_ref, kn_ref, vn_ref, lfn_ref, kc_hbm, vc_hbm, lfc_hbm,
     o_ref, yd_ref, acc_ref, kbuf, vbuf, lfbuf, sem) = refs
    pairs = _decode_gather(pt_ref, kc_hbm, vc_hbm, lfc_hbm, kbuf, vbuf, lfbuf, sem,
                           layer=layer, n_pages=n_pages, per_step=per_step, base=base)
    if mix:
        ycat = jnp.concatenate([ya_ref[...], yb_ref[...], yc_ref[...], ydp_ref[...]], axis=1)
        x = h_ref[...] + _rms(_dot(ycat.astype(BF16), wo_ref[0]), mpost_ref[...])
    else:
        x = x_ref[...]
    for j, (sample, slot) in enumerate(pairs):
        yd_ref[j] = _decode_sample(sample, slot, q_ref, qt_ref, kn_ref, vn_ref, lfn_ref,
                                   kbuf, vbuf, lfbuf, n_pages)
    o_ref[...] = _swiglu_residual(x, pre_ref[...], post_ref[...],
                                  wg_ref.at[0], wu_ref.at[0], wd_ref.at[0], acc_ref, fc)


def _ffn_decode_call(layer, x, mix_in, pre_g, post_g, wg, wu, wd, page_table, q, qt, kn, vn, lfn, kc, vc, lfc,
                     base, n_dec, tm, fc=256):
    m, d = x.shape
    f = wg.shape[2]
    n, n_pages = page_table.shape
    steps = m // tm
    assert n_dec % steps == 0 and base + n_dec <= n
    per_step = n_dec // steps
    past = n_pages * PAGE_SIZE
    const = lambda shape: pl.BlockSpec(shape, lambda i, pt: (0,) * len(shape), pipeline_mode=pl.Buffered(1))
    lyr = lambda shape: pl.BlockSpec((1,) + shape, lambda i, pt: (layer, 0, 0), pipeline_mode=pl.Buffered(1))
    rows = lambda w: pl.BlockSpec((tm, w), lambda i, pt: (i, 0))
    anyspace = pl.BlockSpec(memory_space=pl.ANY)
    mix = mix_in is not None
    lead_specs, lead_args = [rows(d)], [x]
    if mix:
        ya, yb, yc, yd, wo, mpost = mix_in
        lead_specs += [rows(GROUP_W)] * 4 + [lyr((d, d)), const((1, d))]
        lead_args += [ya, yb, yc, yd, wo, mpost]
    grid_spec = pltpu.PrefetchScalarGridSpec(
        num_scalar_prefetch=1,
        grid=(steps,),
        in_specs=lead_specs + [const((1, d)), const((1, d)), lyr((d, f)), lyr((d, f)), lyr((f, d)),
                               const((n, VD)), const((VD, n)), const((n, VD)), const((n, VD)), const((n, LANES)),
                               anyspace, anyspace, anyspace],
        out_specs=[rows(d), pl.BlockSpec((per_step, 1, VD), lambda i, pt: (i, 0, 0))],
        scratch_shapes=[pltpu.VMEM((tm, d), F32),
                        pltpu.VMEM((2 * per_step, n_pages, VD, PAGE_SIZE), F32),
                        pltpu.VMEM((2 * per_step, n_pages, VD, PAGE_SIZE), F32),
                        pltpu.VMEM((2 * per_step, SUBLANES, past), F32),
                        pltpu.SemaphoreType.DMA((2 * per_step, 3))],
    )
    return pl.pallas_call(
        functools.partial(_ffn_decode_body, fc=fc, n_pages=n_pages, layer=layer, per_step=per_step,
                          base=base, mix=mix),
        grid_spec=grid_spec,
        out_shape=[jax.ShapeDtypeStruct((m, d), F32), jax.ShapeDtypeStruct((n_dec, 1, VD), F32)],
        compiler_params=_cparams("arbitrary"),
        name="mix_ffn_decode" if mix else "ffn_decode",
    )(page_table, *lead_args, pre_g, post_g, wg, wu, wd, q, qt, kn, vn, lfn, kc, vc, lfc)


ATTN_TILE = 256


def _row_tile(m, cap=512):
    tm = cap
    while tm >= SUBLANES:
        if m % tm == 0:
            return tm
        tm //= 2
    raise ValueError(f"row count {m} is not a multiple of {SUBLANES}")


def kernel(x_prompt, x_sample, state_conv_a, state_gla, state_conv_c, cache_k, cache_v, cache_logf, page_table, ffn1_pre_g, ffn1_post_g, ffn1_w_gate, ffn1_w_up, ffn1_w_down, mix_pre_g, mix_post_g, w_in, w_out, a_conv_w, a_conv_b, a_ln_g, a_ln_b, b_gate_w2, b_gate_b, b_out_norm_g, c_conv_w, d_forget_b, ffn2_pre_g, ffn2_post_g, ffn2_w_gate, ffn2_w_up, ffn2_w_down):
    depth = w_in.shape[0]
    bp, seq, d = x_prompt.shape
    bd = x_sample.shape[0]
    n_pool = cache_k.shape[1]
    assert x_sample.shape[1] == 1 and d == D_MODEL
    assert seq % ATTN_TILE == 0 and seq % GLA_CHUNK == 0 and seq >= CONV_A_W - 1

    hp = x_prompt.reshape(bp * seq, d)
    hs = x_sample.reshape(bd, d)
    tmp = _row_tile(bp * seq)
    tms = _row_tile(bd)
    tq = ATTN_TILE

    off_b = N_PA
    off_c = off_b + 2 * KD + 2 * VD + GLA_RANK
    off_d = off_c + N_PC
    row = lambda v: v.reshape(1, -1)

    lanes_of = lambda v, n=LANES: jnp.broadcast_to(v[:, None], (v.shape[0], n))

    ck = cache_k.transpose(0, 1, 3, 4, 2).reshape(depth, n_pool, VD, PAGE_SIZE)
    cv = cache_v.transpose(0, 1, 3, 4, 2).reshape(depth, n_pool, VD, PAGE_SIZE)
    sca = state_conv_a.transpose(0, 2, 1, 3)
    sgl = state_gla.transpose(0, 2, 3, 4, 1).reshape(depth, KD, HEAD_DIM, bd)
    scc = state_conv_c.reshape(depth, bd, (CONV_C_W - 1) * GROUP_W)
    w_in_t = w_in.transpose(0, 2, 1)

    lf_t = jnp.pad(cache_logf.transpose(0, 1, 3, 2), ((0, 0), (0, 0), (0, SUBLANES - N_HEADS), (0, 0)))
    lf_rows = depth * n_pool * SUBLANES
    lf_pages = _logf_pages_call(lf_t.reshape(lf_rows, PAGE_SIZE), _row_tile(lf_rows, cap=4096))
    lf_pages = lf_pages.reshape(depth, n_pool, SUBLANES, PAGE_SIZE)

    ffn1_ws = tuple(w.astype(BF16) for w in (ffn1_w_gate, ffn1_w_up, ffn1_w_down))
    ffn2_ws = tuple(w.astype(BF16) for w in (ffn2_w_gate, ffn2_w_up, ffn2_w_down))
    wo = w_out.astype(BF16)

    outs = [[] for _ in range(12)]
    for l in range(depth):
        wt = w_in_t[l].astype(BF16)
        pad_rows = lambda w, n: jnp.pad(w, ((0, n - w.shape[0]), (0, 0)))
        wq, wk, wv = (wt[off_d + i * VD:off_d + (i + 1) * VD] for i in range(3))
        in_w = (wt[0:off_b], pad_rows(wt[off_b:off_c], N_PB), wt[off_c:off_d],
                wq, wk, wv, pad_rows(wt[off_d + 3 * VD:], LANES))
        slabs = lambda w: jnp.pad(w.reshape(N_HEADS, HEAD_DIM, d),
                                  ((0, 0), (0, EXT - HEAD_DIM), (0, 0))).reshape(N_HEADS * EXT, d)
        in_w_prompt = in_w[0:3] + (wq, slabs(wk), wk, wv, in_w[6])
        bf_pad = jnp.pad(d_forget_b[l], (0, LANES - N_HEADS))
        bfr = row(bf_pad)
        bfc = lanes_of(bf_pad[0:SUBLANES])
        wa2 = jnp.pad(b_gate_w2[l], ((0, LANES - GLA_RANK), (0, 0))).astype(BF16)
        mix_w = (a_conv_w[l], row(a_conv_b[l]), row(a_ln_g[l]), row(a_ln_b[l]))
        ffn1_w = (row(ffn1_pre_g[l]), row(ffn1_post_g[l])) + ffn1_ws
        ffn2_w = (row(ffn2_pre_g[l]), row(ffn2_post_g[l])) + ffn2_ws

        hs = _ffn_call(l, hs, *ffn1_w, tm=tms)
        s_pa, s_pbt, s_pc, s_q, s_k, s_v, s_lf, s_kt, s_vt, s_lft, s_qt = _inproj_sample_call(
            hs, row(mix_pre_g[l]), in_w, bfc, bfr)
        s_ya, s_yb, s_yc, s_bufa, s_state, s_bufc = _mix_sample_call(
            l, s_pa, s_pbt, s_pc, sca, sgl, scc, *mix_w, wa2.T, lanes_of(b_gate_b[l], bd),
            lanes_of(b_out_norm_g[l], bd), c_conv_w[l])

        dec_args = (page_table, s_q, s_qt, s_k, s_v, s_lf, ck, cv, lf_pages)
        half = bd // 2
        hp, s_yd0 = _ffn_decode_call(l, hp, None, *ffn1_w, *dec_args, base=0, n_dec=half, tm=tmp)
        pa, pb, pc, kt, vt, lft, qe, ke, vtb = _inproj_prompt_call(
            hp, row(mix_pre_g[l]), in_w_prompt, bfc, bp, seq, tm=tmp)
        r3 = lambda t: t.reshape(bp, seq, t.shape[-1])
        ya, yc, buf_a, buf_c = _conv_prompt_call(r3(pa), r3(pc), *mix_w, c_conv_w[l])
        yb, s_b = _gla_prompt_call(r3(pb), wa2, row(b_gate_b[l]), row(b_out_norm_g[l]))
        yd = _fox_prompt_call(qe, ke, vtb, tq)
        f2 = lambda t: t.reshape(bp * seq, t.shape[-1])
        hp, s_yd1 = _ffn_decode_call(l, hp, (f2(ya), f2(yb), f2(yc), f2(yd), wo, row(mix_post_g[l])),
                                     *ffn2_w, *dec_args, base=half, n_dec=bd - half, tm=tmp)
        for i, t in enumerate((buf_a, s_b, buf_c, kt, vt, lft[:, 0:N_HEADS, :])):
            outs[i].append(t)

        s_yd = jnp.concatenate([s_yd0, s_yd1], axis=0).reshape(bd, VD)
        hs = _mix_ffn_call(l, hs, s_ya, s_yb, s_yc, s_yd, wo, row(mix_post_g[l]), *ffn2_w, tm=tms)
        for i, t in enumerate((s_bufa, s_state, s_bufc, s_kt, s_vt, s_lft[0:N_HEADS, :])):
            outs[6 + i].append(t)

    p_ca, p_gla, p_cc, p_kt, p_vt, p_lft, s_ca, s_gla, s_cc, s_kt, s_vt, s_lft = (jnp.stack(o) for o in outs)
    heads = lambda t: t.reshape(t.shape[:-2] + (N_HEADS, HEAD_DIM, t.shape[-1]))
    return (hp.reshape(bp, seq, d), hs.reshape(bd, 1, d),
            p_ca, p_gla, p_cc,
            heads(p_kt).transpose(0, 1, 4, 2, 3), heads(p_vt).transpose(0, 1, 4, 2, 3),
            p_lft.transpose(0, 1, 3, 2),
            s_ca.transpose(0, 2, 1, 3),
            s_gla.reshape(depth, N_HEADS, DK_B, HEAD_DIM, bd).transpose(0, 4, 1, 2, 3),
            s_cc.reshape(depth, bd, CONV_C_W - 1, GROUP_W),
            heads(s_kt).transpose(0, 3, 1, 2)[:, :, None], heads(s_vt).transpose(0, 3, 1, 2)[:, :, None],
            s_lft.transpose(0, 2, 1)[:, :, None])
```

```python
import functools

import jax
import jax.numpy as jnp
from jax import lax
from jax.experimental import pallas as pl
from jax.experimental.pallas import tpu as pltpu

F32 = jnp.float32
BF16 = jnp.bfloat16

D_MODEL = 1024
GROUP_W = D_MODEL // 4
HEAD_DIM = 64
N_HEADS = GROUP_W // HEAD_DIM
DK_B = HEAD_DIM // 2
GLA_RANK = 16
GLA_TAU = 16.0
GLA_CHUNK = 64
CONV_A_W = 31
CONV_C_W = 3
PAGE_SIZE = 128
EPS = 1e-6
NEG_INF = -1e30

LANES = 128
SUBLANES = 8
VMEM_LIMIT_BYTES = 56 * 1024 * 1024

N_PA = 2 * GROUP_W
KD = N_HEADS * DK_B
VD = N_HEADS * HEAD_DIM
N_PB = 2 * KD + 2 * VD + LANES
N_PC = 3 * GROUP_W


def _cparams(*sem):
    return pltpu.CompilerParams(dimension_semantics=sem, vmem_limit_bytes=VMEM_LIMIT_BYTES)


def _dot(a, b):
    return jnp.dot(a, b, preferred_element_type=F32)


def _dot_nt(a, b):
    return lax.dot_general(a, b, (((1,), (1,)), ((), ())), preferred_element_type=F32)


def _split3(x):
    hi = x.astype(BF16)
    r = x - hi.astype(F32)
    mid = r.astype(BF16)
    lo = (r - mid.astype(F32)).astype(BF16)
    return hi, mid, lo


def _dot_sel(x, sel):
    hi, mid, lo = _split3(x)
    return _dot(hi, sel) + _dot(mid, sel) + _dot(lo, sel)


def _sel_dot(sel, x):
    hi, mid, lo = _split3(x)
    return _dot(sel, hi) + _dot(sel, mid) + _dot(sel, lo)


def _rms(x, g):
    return x * lax.rsqrt(jnp.mean(x * x, axis=-1, keepdims=True) + EPS) * g


def _sigmoid(x):
    return 1.0 / (1.0 + jnp.exp(-x))


def _silu(x):
    return x * _sigmoid(x)


def _log_sigmoid(x):
    return jnp.minimum(x, 0.0) - jnp.log1p(jnp.exp(-jnp.abs(x)))


def _iota(shape, dim):
    return lax.broadcasted_iota(jnp.int32, shape, dim)


def _idiv(x, n):
    assert n & (n - 1) == 0
    return lax.shift_right_logical(x, n.bit_length() - 1)


def _imod(x, n):
    assert n & (n - 1) == 0
    return x & (n - 1)


def _head_mean_matrix():
    r = _idiv(_iota((VD, VD), 0), HEAD_DIM)
    c = _idiv(_iota((VD, VD), 1), HEAD_DIM)
    return (r == c).astype(BF16)


def _swiglu_residual(x, pre_g, post_g, wg_ref, wu_ref, wd_ref, acc_ref, fc):
    xn = _rms(x, pre_g).astype(BF16)
    for c in range(wg_ref.shape[1] // fc):
        sl = slice(c * fc, (c + 1) * fc)
        g = _dot(xn, wg_ref[:, sl])
        u = _dot(xn, wu_ref[:, sl])
        hid = (_silu(g) * u).astype(BF16)
        part = _dot(hid, wd_ref[sl, :])
        if c == 0:
            acc_ref[...] = part
        else:
            acc_ref[...] += part
    return x + 0.5 * _rms(acc_ref[...], post_g)


def _ffn_body(x_ref, pre_ref, post_ref, wg_ref, wu_ref, wd_ref, o_ref, acc_ref, *, fc):
    o_ref[...] = _swiglu_residual(x_ref[...], pre_ref[...], post_ref[...],
                                  wg_ref.at[0], wu_ref.at[0], wd_ref.at[0], acc_ref, fc)


def _mix_ffn_body(h_ref, ya_ref, yb_ref, yc_ref, yd_ref, wo_ref, mpost_ref,
                  pre_ref, post_ref, wg_ref, wu_ref, wd_ref, o_ref, acc_ref, *, fc):
    ycat = jnp.concatenate([ya_ref[...], yb_ref[...], yc_ref[...], yd_ref[...]], axis=1)
    y = _dot(ycat.astype(BF16), wo_ref[0])
    h = h_ref[...] + _rms(y, mpost_ref[...])
    o_ref[...] = _swiglu_residual(h, pre_ref[...], post_ref[...],
                                  wg_ref.at[0], wu_ref.at[0], wd_ref.at[0], acc_ref, fc)


def _row_spec(tm, n):
    return pl.BlockSpec((tm, n), lambda i: (i, 0))


def _full_spec(shape):
    return pl.BlockSpec(shape, lambda i: (0,) * len(shape), pipeline_mode=pl.Buffered(1))


def _ffn_call(layer, x, pre_g, post_g, wg, wu, wd, tm, fc=256):
    m, d = x.shape
    return pl.pallas_call(
        functools.partial(_ffn_body, fc=fc),
        grid=(m // tm,),
        in_specs=[_row_spec(tm, d), _full_spec((1, d)), _full_spec((1, d)),
                  _layer_spec(wg.shape, layer), _layer_spec(wu.shape, layer), _layer_spec(wd.shape, layer)],
        out_specs=_row_spec(tm, d),
        out_shape=jax.ShapeDtypeStruct((m, d), F32),
        scratch_shapes=[pltpu.VMEM((tm, d), F32)],
        compiler_params=_cparams("arbitrary"),
        name="ffn",
    )(x, pre_g, post_g, wg, wu, wd)


def _mix_ffn_call(layer, h, ya, yb, yc, yd, wo, mpost, pre_g, post_g, wg, wu, wd, tm, fc=256):
    m, d = h.shape
    return pl.pallas_call(
        functools.partial(_mix_ffn_body, fc=fc),
        grid=(m // tm,),
        in_specs=[_row_spec(tm, d)] + [_row_spec(tm, GROUP_W)] * 4
                 + [_layer_spec(wo.shape, layer), _full_spec((1, d)), _full_spec((1, d)), _full_spec((1, d)),
                    _layer_spec(wg.shape, layer), _layer_spec(wu.shape, layer), _layer_spec(wd.shape, layer)],
        out_specs=_row_spec(tm, d),
        out_shape=jax.ShapeDtypeStruct((m, d), F32),
        scratch_shapes=[pltpu.VMEM((tm, d), F32)],
        compiler_params=_cparams("arbitrary"),
        name="mix_ffn",
    )(h, ya, yb, yc, yd, wo, mpost, pre_g, post_g, wg, wu, wd)


EXT = LANES
C_KEY = HEAD_DIM
C_QRY = HEAD_DIM + 3


def _inproj_prompt_body(h_ref, g_ref, wa_ref, wb_ref, wc_ref, wq_ref, wke_ref, wk_ref, wv_ref, wl_ref,
                        bfc_ref, pa_ref, pb_ref, pc_ref, kt_ref, vt_ref, lft_ref,
                        qe_ref, ke_ref, vtb_ref, carry_ref, *, per):
    tm = h_ref.shape[0]
    u = _rms(h_ref[...], g_ref[...]).astype(BF16)
    pa_ref[...] = _dot_nt(u, wa_ref[...])
    pb_ref[...] = _dot_nt(u, wb_ref[...])
    pc_ref[...] = _dot_nt(u, wc_ref[...])
    kt_ref[0] = _dot_nt(wk_ref[...], u)
    vt = _dot_nt(wv_ref[...], u)
    vt_ref[0] = vt
    vtb_ref[0] = vt.astype(BF16)
    lft = _log_sigmoid(_dot_nt(wl_ref[0:SUBLANES, :], u) + bfc_ref[:, 0:1])
    lft_ref[0] = lft

    first = pl.program_id(0) % per == 0
    carry = jnp.where(first, 0.0, carry_ref[:, 0:1])
    upto = (_iota((tm, tm), 0) <= _iota((tm, tm), 1)).astype(BF16)
    lfm = jnp.where(_iota((SUBLANES, tm), 0) < N_HEADS, lft, 0.0)
    c = _dot_sel(lfm, upto) + carry
    carry_ref[...] = jnp.broadcast_to(c[:, tm - 1:tm], carry_ref.shape)
    ccol = jnp.concatenate([c, jnp.zeros((LANES - SUBLANES, tm), F32)], axis=0).T

    hi, mid, lo = _split3(ccol)
    pieces = (hi.astype(F32) + pltpu.roll(mid.astype(F32), N_HEADS, 1)
              + pltpu.roll(lo.astype(F32), 2 * N_HEADS, 1)).astype(BF16)
    r = _iota((LANES, N_HEADS * EXT), 0)
    col = _iota((LANES, N_HEADS * EXT), 1)
    j = _imod(col, EXT) - C_KEY
    place_k = ((j >= 0) & (j < 3) & (r == N_HEADS * j + _idiv(col, EXT))).astype(BF16)
    lane = _imod(_iota((1, N_HEADS * EXT), 1), EXT)
    ones_k = ((lane >= C_QRY) & (lane < C_QRY + 3)).astype(F32)
    ke = _dot_nt(u, wke_ref[...]) + _dot(pieces, place_k) + ones_k
    for h in range(N_HEADS):
        ke_ref[0, h] = ke[:, h * EXT:(h + 1) * EXT].astype(BF16)

    hi, mid, lo = _split3(c)
    pieces_t = jnp.concatenate([hi.astype(F32), mid.astype(F32), lo.astype(F32),
                                jnp.zeros((LANES - 3 * SUBLANES, tm), F32)], axis=0).astype(BF16)
    r = _iota((VD, LANES), 0)
    col = _iota((VD, LANES), 1)
    j = _imod(r, HEAD_DIM) - (C_QRY - HEAD_DIM)
    place_q = ((j >= 0) & (j < 3) & (col == SUBLANES * j + _idiv(r, HEAD_DIM))).astype(BF16)
    rowi = _imod(_iota((VD, 1), 0), HEAD_DIM)
    neg_q = jnp.where(rowi < 3, -1.0, 0.0)
    extra = (_dot(place_q, pieces_t) + neg_q).astype(BF16)
    qt = (_dot_nt(wq_ref[...], u) * (HEAD_DIM ** -0.5)).astype(BF16)
    for h in range(N_HEADS):
        rows = slice(h * HEAD_DIM, (h + 1) * HEAD_DIM)
        qe_ref[0, h, 0:HEAD_DIM, :] = qt[rows, :]
        qe_ref[0, h, HEAD_DIM:, :] = extra[rows, :]


def _inproj_sample_body(h_ref, g_ref, wa_ref, wb_ref, wc_ref, wq_ref, wk_ref, wv_ref, wl_ref,
                        bfc_ref, bfr_ref,
                        pa_ref, pbt_ref, pc_ref, q_ref, k_ref, v_ref, lf_ref, kt_ref, vt_ref, lft_ref, qt_ref):
    u = _rms(h_ref[...], g_ref[...]).astype(BF16)
    pa_ref[...] = _dot_nt(u, wa_ref[...])
    pbt_ref[...] = _dot_nt(wb_ref[...], u)
    pc_ref[...] = _dot_nt(u, wc_ref[...])
    q_ref[...] = _dot_nt(u, wq_ref[...])
    qt_ref[...] = _dot_nt(wq_ref[...], u)
    k_ref[...] = _dot_nt(u, wk_ref[...])
    v_ref[...] = _dot_nt(u, wv_ref[...])
    lf_ref[...] = _log_sigmoid(_dot_nt(u, wl_ref[...]) + bfr_ref[...])
    kt_ref[...] = _dot_nt(wk_ref[...], u)
    vt_ref[...] = _dot_nt(wv_ref[...], u)
    lft_ref[...] = _log_sigmoid(_dot_nt(wl_ref[0:SUBLANES, :], u) + bfc_ref[:, 0:1])


def _inproj_weight_specs(d):
    return [_full_spec((n, d)) for n in (N_PA, N_PB, N_PC, VD, VD, VD, LANES)]


def _inproj_prompt_call(h, g, ws, bfc, bp, seq, tm):
    m, d = h.shape
    per = seq // tm
    tspec = lambda n: pl.BlockSpec((1, n, tm), lambda i: (i // per, 0, i % per))
    hx = N_HEADS * EXT
    return pl.pallas_call(
        functools.partial(_inproj_prompt_body, per=per),
        grid=(m // tm,),
        in_specs=[_row_spec(tm, d), _full_spec((1, d))]
                 + [_full_spec((n, d)) for n in (N_PA, N_PB, N_PC, VD, hx, VD, VD, LANES)]
                 + [_full_spec((SUBLANES, LANES))],
        out_specs=[_row_spec(tm, N_PA), _row_spec(tm, N_PB), _row_spec(tm, N_PC),
                   tspec(VD), tspec(VD), tspec(SUBLANES),
                   pl.BlockSpec((1, N_HEADS, EXT, tm), lambda i: (i // per, 0, 0, i % per)),
                   pl.BlockSpec((1, N_HEADS, tm, EXT), lambda i: (i // per, 0, i % per, 0)),
                   tspec(VD)],
        out_shape=[jax.ShapeDtypeStruct((m, N_PA), F32), jax.ShapeDtypeStruct((m, N_PB), F32),
                   jax.ShapeDtypeStruct((m, N_PC), F32),
                   jax.ShapeDtypeStruct((bp, VD, seq), F32), jax.ShapeDtypeStruct((bp, VD, seq), F32),
                   jax.ShapeDtypeStruct((bp, SUBLANES, seq), F32),
                   jax.ShapeDtypeStruct((bp, N_HEADS, EXT, seq), BF16),
                   jax.ShapeDtypeStruct((bp, N_HEADS, seq, EXT), BF16),
                   jax.ShapeDtypeStruct((bp, VD, seq), BF16)],
        scratch_shapes=[pltpu.VMEM((SUBLANES, LANES), F32)],
        compiler_params=_cparams("arbitrary"),
        name="inproj_prompt",
    )(h, g, *ws, bfc)


def _inproj_sample_call(h, g, ws, bfc, bfr):
    n, d = h.shape
    shapes = [(n, N_PA), (N_PB, n), (n, N_PC), (n, VD), (n, VD), (n, VD), (n, LANES),
              (VD, n), (VD, n), (SUBLANES, n), (VD, n)]
    return pl.pallas_call(
        _inproj_sample_body,
        grid=(1,),
        in_specs=[_full_spec((n, d)), _full_spec((1, d))] + _inproj_weight_specs(d)
                 + [_full_spec((SUBLANES, LANES)), _full_spec((1, LANES))],
        out_specs=[pl.BlockSpec(s, lambda i: (0, 0)) for s in shapes],
        out_shape=[jax.ShapeDtypeStruct(s, F32) for s in shapes],
        compiler_params=_cparams("arbitrary"),
        name="inproj_sample",
    )(h, g, *ws, bfc, bfr)


A_PAD = 32
C_PAD = 8
CONV_ROWS = 256


def _conv_prompt_body(pa_ref, pc_ref, aw_ref, ab_ref, lg_ref, lb_ref, cw_ref,
                      ya_ref, yc_ref, bufa_ref, bufc_ref, apad_ref, zpad_ref):
    seq = pa_ref.shape[1]
    c = GROUP_W
    apad_ref[0:A_PAD, :] = jnp.zeros((A_PAD, c), F32)
    apad_ref[A_PAD:, :] = pa_ref[0, :, 0:c] * _sigmoid(pa_ref[0, :, c:])
    zpad_ref[0:C_PAD, :] = jnp.zeros((C_PAD, c), F32)
    zpad_ref[C_PAD:, :] = pc_ref[0, :, c:2 * c] * pc_ref[0, :, 2 * c:]

    def step(i, carry):
        r0 = pl.multiple_of(i * CONV_ROWS, CONV_ROWS)
        win = apad_ref[pl.ds(r0, CONV_ROWS + A_PAD), :]
        acc = jnp.zeros((CONV_ROWS, c), F32) + ab_ref[...]
        for r in range(SUBLANES):
            nrow = CONV_ROWS if r == 0 else CONV_ROWS + SUBLANES
            u = None
            for a8 in range(0, A_PAD + 1, SUBLANES):
                j = a8 + r - (A_PAD - (CONV_A_W - 1))
                if 0 <= j < CONV_A_W:
                    term = aw_ref[j:j + 1, :] * win[a8:a8 + nrow, :]
                    u = term if u is None else u + term
            acc = acc + u[r:r + CONV_ROWS, :]
        mu = jnp.mean(acc, axis=-1, keepdims=True)
        var = jnp.mean(jnp.square(acc - mu), axis=-1, keepdims=True)
        yn = (acc - mu) * lax.rsqrt(var + EPS) * lg_ref[...] + lb_ref[...]
        ya_ref[0, pl.ds(r0, CONV_ROWS), :] = _silu(yn)
        zwin = zpad_ref[pl.ds(r0, CONV_ROWS + C_PAD), :]
        accc = jnp.zeros((CONV_ROWS, c), F32)
        for j in range(CONV_C_W):
            off = C_PAD - (CONV_C_W - 1) + j
            accc = accc + cw_ref[j:j + 1, :] * zwin[off:off + CONV_ROWS, :]
        yc_ref[0, pl.ds(r0, CONV_ROWS), :] = pc_ref[0, pl.ds(r0, CONV_ROWS), 0:c] * accc
        return carry

    lax.fori_loop(0, seq // CONV_ROWS, step, 0)
    na = CONV_A_W - 1
    nc = CONV_C_W - 1
    bufa_ref[0] = apad_ref[seq:A_PAD + seq, :][A_PAD - na:, :]
    bufc_ref[0] = zpad_ref[seq:C_PAD + seq, :][C_PAD - nc:, :]


def _conv_prompt_call(pa, pc, aw, ab, lg, lb, cw):
    b, seq, _ = pa.shape
    c = GROUP_W
    bspec = lambda n: pl.BlockSpec((1, seq, n), lambda i: (i, 0, 0))
    return pl.pallas_call(
        _conv_prompt_body,
        grid=(b,),
        in_specs=[bspec(N_PA), bspec(N_PC), _full_spec((CONV_A_W, c)), _full_spec((1, c)),
                  _full_spec((1, c)), _full_spec((1, c)), _full_spec((CONV_C_W, c))],
        out_specs=[bspec(c), bspec(c),
                   pl.BlockSpec((1, CONV_A_W - 1, c), lambda i: (i, 0, 0)),
                   pl.BlockSpec((1, CONV_C_W - 1, c), lambda i: (i, 0, 0))],
        out_shape=[jax.ShapeDtypeStruct((b, seq, c), F32), jax.ShapeDtypeStruct((b, seq, c), F32),
                   jax.ShapeDtypeStruct((b, CONV_A_W - 1, c), F32),
                   jax.ShapeDtypeStruct((b, CONV_C_W - 1, c), F32)],
        scratch_shapes=[pltpu.VMEM((A_PAD + seq, c), F32), pltpu.VMEM((C_PAD + seq, c), F32)],
        compiler_params=_cparams("arbitrary"),
        name="conv_prompt",
    )(pa, pc, aw, ab, lg, lb, cw)


def _gla_gate(alr, wa2_ref, ba_ref):
    return _log_sigmoid(_dot(alr.astype(BF16), wa2_ref[...]) + ba_ref[...]) * (1.0 / GLA_TAU)


def _gla_out(o, g, ng, hm):
    ms = _dot_sel(o * o, hm) * (1.0 / HEAD_DIM)
    return o * lax.rsqrt(ms + EPS) * ng * _silu(g)


GLA_SLAB = 256


def _group_row(x, g, r):
    n, c = x.shape
    x3 = x.reshape(n // g, g, c)
    return jnp.broadcast_to(x3[:, r:r + 1, :], (n // g, g, c)).reshape(n, c)


def _gla_prompt_body(pb_ref, wa2_ref, ba_ref, ng_ref, y_ref, st_ref, la_ref, s_ref):
    seq = pb_ref.shape[1]
    ck = GLA_CHUNK
    la_ref[...] = _gla_gate(pb_ref[0, :, 2 * KD + 2 * VD:], wa2_ref, ba_ref)
    s_ref[...] = jnp.zeros((KD, VD), F32)

    rs = min(GLA_SLAB, seq)
    nch = rs // ck
    tt = _iota((rs, rs), 0)
    ss = _iota((rs, rs), 1)
    tri = ((ss <= tt) & (_idiv(ss, ck) == _idiv(tt, ck))).astype(BF16)
    blk = _idiv(_iota((KD, VD), 0), DK_B) == _idiv(_iota((KD, VD), 1), HEAD_DIM)
    blk_bf = blk.astype(BF16)
    blk_f = blk.astype(F32)
    hm = _head_mean_matrix()
    scale = DK_B ** -0.5
    levels = [g for g in (2 * SUBLANES, 4 * SUBLANES, 8 * SUBLANES) if g <= ck]
    assert ck == 8 * SUBLANES
    key_head = _idiv(_iota((1, KD), 1), DK_B)
    val_head = _idiv(_iota((1, VD), 1), HEAD_DIM)
    tloc = _iota((rs, KD), 0)

    def slab(si, carry):
        r0 = pl.multiple_of(si * rs, rs)
        rows = pl.ds(r0, rs)
        q = pb_ref[0, rows, 0:KD] * scale
        k = pb_ref[0, rows, KD:2 * KD]
        v = pb_ref[0, rows, 2 * KD:2 * KD + VD]
        g = pb_ref[0, rows, 2 * KD + VD:2 * KD + 2 * VD]
        b = _sel_dot(tri, la_ref[rows, :])
        vb3 = v.astype(BF16).reshape(nch, ck, VD)

        t8 = _imod(tloc, SUBLANES)
        acc = jnp.zeros((rs, VD), F32)
        for s in range(SUBLANES):
            diff = jnp.where(t8 >= s, b - _group_row(b, SUBLANES, s), -jnp.inf)
            e = (q * _group_row(k, SUBLANES, s) * jnp.exp(diff)).astype(BF16)
            acc = acc + _dot(e, blk_bf) * _group_row(v, SUBLANES, s)

        sc = jnp.zeros((nch, N_HEADS * ck, ck), F32)
        for gsz in levels:
            r = _group_row(b, gsz, gsz // 2)
            upper = _imod(tloc, gsz) >= gsz // 2
            qf = q * jnp.exp(jnp.where(upper, b - r, -jnp.inf))
            kf = (k * jnp.exp(jnp.where(upper, -jnp.inf, r - b))).astype(BF16).reshape(nch, ck, KD)
            qstack = jnp.concatenate(
                [jnp.where(key_head == h, qf, 0.0).astype(BF16).reshape(nch, ck, KD)
                 for h in range(N_HEADS)], axis=1)
            sc_g = jnp.einsum('cqk,csk->cqs', qstack, kf, preferred_element_type=F32)
            same_group = _idiv(_imod(_iota((N_HEADS * ck, ck), 0), ck), gsz) == _idiv(_iota((N_HEADS * ck, ck), 1), gsz)
            sc = sc + jnp.where(same_group[None], sc_g, 0.0)
        res = jnp.einsum('cqs,csv->cqv', sc.astype(BF16), vb3, preferred_element_type=F32)
        for h in range(N_HEADS):
            part = res[:, h * ck:(h + 1) * ck, :].reshape(rs, VD)
            acc = acc + jnp.where(val_head == h, part, 0.0)

        b_last = _group_row(b, ck, ck - 1)
        kdec = k * jnp.exp(b_last - b)
        lastb = jnp.concatenate([b[(c + 1) * ck - 1:(c + 1) * ck, :] for c in range(nch)]
                                + [jnp.zeros((LANES - nch, KD), F32)], axis=0)
        dcols = jnp.exp(lastb.T)
        s_cur = s_ref[...]
        s_before = []
        for c in range(nch):
            s_before.append(s_cur.astype(BF16))
            crow = slice(c * ck, (c + 1) * ck)
            upd = _dot(kdec[crow, :].T.astype(BF16), vb3[c])
            s_cur = dcols[:, c:c + 1] * s_cur + upd * blk_f
        s_ref[...] = s_cur
        qb3 = (q * jnp.exp(b)).astype(BF16).reshape(nch, ck, KD)
        inter = jnp.einsum('cqk,ckv->cqv', qb3, jnp.stack(s_before), preferred_element_type=F32)
        o = inter.reshape(rs, VD) + acc
        y_ref[0, rows, :] = _gla_out(o, g, ng_ref[...], hm)
        return carry

    lax.fori_loop(0, seq // rs, slab, 0)
    for h in range(N_HEADS):
        st_ref[0, h] = s_ref[h * DK_B:(h + 1) * DK_B, h * HEAD_DIM:(h + 1) * HEAD_DIM]


def _gla_prompt_call(pb, wa2, ba, ng):
    b, seq, _ = pb.shape
    return pl.pallas_call(
        _gla_prompt_body,
        grid=(b,),
        in_specs=[pl.BlockSpec((1, seq, N_PB), lambda i: (i, 0, 0)), _full_spec((LANES, KD)),
                  _full_spec((1, KD)), _full_spec((1, VD))],
        out_specs=[pl.BlockSpec((1, seq, VD), lambda i: (i, 0, 0)),
                   pl.BlockSpec((1, N_HEADS, DK_B, HEAD_DIM), lambda i: (i, 0, 0, 0))],
        out_shape=[jax.ShapeDtypeStruct((b, seq, VD), F32),
                   jax.ShapeDtypeStruct((b, N_HEADS, DK_B, HEAD_DIM), F32)],
        scratch_shapes=[pltpu.VMEM((seq, KD), F32), pltpu.VMEM((KD, VD), F32)],
        compiler_params=_cparams("arbitrary"),
        name="gla_prompt",
    )(pb, wa2, ba, ng)


FOX_GROUP = 4


def _fox_prompt_body(qe_ref, ke_ref, vtb_ref, o_ref, *, tq):
    qi = pl.program_id(1)
    key_pos = _iota((tq, tq), 0)
    qry_pos = _iota((tq, tq), 1)

    def tile(ki):
        return pl.ds(pl.multiple_of(ki * tq, tq), tq)

    def consume(tiles, stats):
        scores = [[_dot(ke_ref[0, h, tile(ki), :], qe_ref[0, h]) for h in range(N_HEADS)]
                  for ki, _ in tiles]
        for (ki, masked), sc in zip(tiles, scores):
            soft = []
            for h in range(N_HEADS):
                m, l, _ = stats[h]
                s = sc[h]
                if masked:
                    s = jnp.where(key_pos <= qry_pos, s, NEG_INF)
                m_new = jnp.maximum(m, jnp.max(s, axis=0, keepdims=True))
                alpha = jnp.exp(m - m_new)
                p = jnp.exp(s - m_new)
                soft.append((m_new, alpha * l + jnp.sum(p, axis=0, keepdims=True), alpha, p.astype(BF16)))
            out = []
            for h in range(N_HEADS):
                m_new, l, alpha, p = soft[h]
                vh = vtb_ref[0, h * HEAD_DIM:(h + 1) * HEAD_DIM, tile(ki)]
                out.append((m_new, l, alpha * stats[h][2] + _dot(vh, p)))
            stats = tuple(out)
        return stats

    init = tuple((jnp.full((1, tq), NEG_INF, F32), jnp.zeros((1, tq), F32),
                  jnp.zeros((HEAD_DIM, tq), F32)) for _ in range(N_HEADS))
    g = FOX_GROUP
    stats = lax.fori_loop(
        0, qi // g, lambda k, st: consume([(g * k + t, False) for t in range(g)], st), init)
    left = qi % g
    tails = [functools.partial(consume, [(qi - r + t, False) for t in range(r)] + [(qi, True)])
             for r in range(g)]
    stats = lax.switch(left, tails, stats)
    o_ref[0] = jnp.concatenate([acc / l for _, l, acc in stats], axis=0).T


def _fox_prompt_call(qe, ke, vtb, tq):
    b, _, _, seq = qe.shape
    return pl.pallas_call(
        functools.partial(_fox_prompt_body, tq=tq),
        grid=(b, seq // tq),
        in_specs=[pl.BlockSpec((1, N_HEADS, EXT, tq), lambda i, j: (i, 0, 0, j)),
                  pl.BlockSpec((1, N_HEADS, seq, EXT), lambda i, j: (i, 0, 0, 0)),
                  pl.BlockSpec((1, VD, seq), lambda i, j: (i, 0, 0))],
        out_specs=pl.BlockSpec((1, tq, VD), lambda i, j: (i, j, 0)),
        out_shape=jax.ShapeDtypeStruct((b, seq, VD), F32),
        compiler_params=_cparams("arbitrary", "arbitrary"),
        name="fox_prompt",
    )(qe, ke, vtb)


def _mix_sample_body(pa_ref, pbt_ref, pc_ref, bufa_ref, sg_ref, bufc_ref,
                     aw_ref, ab_ref, lg_ref, lb_ref, wa2t_ref, bac_ref, ngc_ref, cw_ref,
                     ya_ref, yb_ref, yc_ref, bufa_o, sg_o, bufc_o, q_s, k_s, dec_s):
    c = GROUP_W
    a = pa_ref[:, 0:c] * _sigmoid(pa_ref[:, c:])
    na = CONV_A_W - 1
    acc = aw_ref[na:na + 1, :] * a + ab_ref[...]
    for j in range(na):
        acc = acc + aw_ref[j:j + 1, :] * bufa_ref[0, j]
    mu = jnp.mean(acc, axis=-1, keepdims=True)
    var = jnp.mean(jnp.square(acc - mu), axis=-1, keepdims=True)
    ya_ref[...] = _silu((acc - mu) * lax.rsqrt(var + EPS) * lg_ref[...] + lb_ref[...])
    for j in range(na - 1):
        bufa_o[j] = bufa_ref[0, j + 1]
    bufa_o[na - 1] = a
    z = pc_ref[:, c:2 * c] * pc_ref[:, 2 * c:]
    conv = (cw_ref[0:1, :] * bufc_ref[0, :, 0:c] + cw_ref[1:2, :] * bufc_ref[0, :, c:]
            + cw_ref[2:3, :] * z)
    yc_ref[...] = pc_ref[:, 0:c] * conv
    bufc_o[:, 0:c] = bufc_ref[0, :, c:]
    bufc_o[:, c:] = z
    q_s[...] = pbt_ref[0:KD, :] * (DK_B ** -0.5)
    k_s[...] = pbt_ref[KD:2 * KD, :]
    gate = _dot(wa2t_ref[...], pbt_ref[2 * KD + 2 * VD:, :].astype(BF16)) + bac_ref[...]
    dec_s[...] = jnp.exp(_log_sigmoid(gate) * (1.0 / GLA_TAU))
    n = pa_ref.shape[0]
    ys = []
    for h in range(N_HEADS):
        vrows = slice(2 * KD + h * HEAD_DIM, 2 * KD + (h + 1) * HEAD_DIM)
        grows = slice(2 * KD + VD + h * HEAD_DIM, 2 * KD + VD + (h + 1) * HEAD_DIM)
        vh = pbt_ref[vrows, :]

        def key_step(kk, o, h=h, vh=vh):
            hk = h * DK_B + kk
            one = pl.ds(hk, 1)
            s_new = dec_s[one, :] * sg_ref[0, hk] + k_s[one, :] * vh
            sg_o[hk] = s_new
            return o + q_s[one, :] * s_new

        o = lax.fori_loop(0, DK_B, key_step, jnp.zeros((HEAD_DIM, n), F32))
        ms = jnp.mean(o * o, axis=0, keepdims=True)
        ng = ngc_ref[h * HEAD_DIM:(h + 1) * HEAD_DIM, :]
        ys.append(o * lax.rsqrt(ms + EPS) * ng * _silu(pbt_ref[grows, :]))
    yb_ref[...] = jnp.concatenate(ys, axis=0).T


def _layer_spec(shape, l):
    return pl.BlockSpec((1,) + tuple(shape[1:]), lambda i: (l,) + (0,) * (len(shape) - 1),
                        pipeline_mode=pl.Buffered(1))


def _mix_sample_call(l, pa, pbt, pc, bufa, sg, bufc, aw, ab, lg, lb, wa2t, bac, ngc, cw):
    n = pa.shape[0]
    c = GROUP_W
    small = (aw, ab, lg, lb, wa2t, bac, ngc, cw)
    out_shape = [jax.ShapeDtypeStruct((n, c), F32)] * 3 + [
        jax.ShapeDtypeStruct(bufa.shape[1:], F32), jax.ShapeDtypeStruct(sg.shape[1:], F32),
        jax.ShapeDtypeStruct(bufc.shape[1:], F32)]
    return pl.pallas_call(
        _mix_sample_body,
        grid=(1,),
        in_specs=[_full_spec(pa.shape), _full_spec(pbt.shape), _full_spec(pc.shape),
                  _layer_spec(bufa.shape, l), _layer_spec(sg.shape, l), _layer_spec(bufc.shape, l)]
                 + [_full_spec(x.shape) for x in small],
        out_specs=[pl.BlockSpec(s.shape, lambda i, nd=len(s.shape): (0,) * nd) for s in out_shape],
        out_shape=out_shape,
        scratch_shapes=[pltpu.VMEM((KD, n), F32)] * 3,
        compiler_params=_cparams("arbitrary"),
        name="mix_sample",
    )(pa, pbt, pc, bufa, sg, bufc, *small)


def _logf_pages_body(x_ref, o_ref):
    n = x_ref.shape[0]
    x = x_ref[...]
    later = (_iota((PAGE_SIZE, PAGE_SIZE), 0) > _iota((PAGE_SIZE, PAGE_SIZE), 1)).astype(BF16)
    ones = jnp.ones((PAGE_SIZE, PAGE_SIZE), BF16)
    within = _dot_sel(x, later)
    total = pltpu.roll(_dot_sel(x, ones), N_HEADS, 0)
    o_ref[...] = jnp.where(_imod(_iota((n, PAGE_SIZE), 0), SUBLANES) < N_HEADS, within, total)


def _logf_pages_call(x, tm):
    m = x.shape[0]
    return pl.pallas_call(
        _logf_pages_body,
        grid=(m // tm,),
        in_specs=[_row_spec(tm, PAGE_SIZE)],
        out_specs=_row_spec(tm, PAGE_SIZE),
        out_shape=jax.ShapeDtypeStruct((m, PAGE_SIZE), F32),
        compiler_params=_cparams("arbitrary"),
        name="logf_pages",
    )(x)


def _decode_sample(b, slot, q_ref, qt_ref, kn_ref, vn_ref, lfn_ref, kbuf, vbuf, lfbuf, n_pages):
    past = n_pages * PAGE_SIZE
    scale = HEAD_DIM ** -0.5
    heads = [slice(h * HEAD_DIM, (h + 1) * HEAD_DIM) for h in range(N_HEADS)]
    hm = (_idiv(_iota((SUBLANES, VD), 1), HEAD_DIM) == _iota((SUBLANES, VD), 0)).astype(F32)
    qbd = jnp.broadcast_to(q_ref[pl.ds(b, 1), :], (SUBLANES, VD)) * hm
    onehot = (_iota((1, qt_ref.shape[1]), 1) == b).astype(F32)
    qcol = jnp.sum(qt_ref[...] * onehot, axis=1, keepdims=True) * scale
    qcb = jnp.broadcast_to(qcol, (VD, PAGE_SIZE))
    parts = [[] for _ in heads]
    for p in range(n_pages):
        prod = kbuf[slot, p] * qcb
        for h, rows in enumerate(heads):
            parts[h].append(prod[rows, :].reshape(HEAD_DIM // SUBLANES, SUBLANES, PAGE_SIZE).sum(axis=0))
    rowid = _iota((SUBLANES, past), 0)
    s = jnp.zeros((SUBLANES, past), F32)
    for h in range(N_HEADS):
        t = jnp.concatenate(parts[h], axis=1)
        for sh in (4, 2, 1):
            t = t + pltpu.roll(t, sh, 0)
        s = jnp.where(rowid == h, t, s)
    lf = lfbuf[slot]
    incl = pltpu.roll(lf, N_HEADS, 0)
    sh = PAGE_SIZE
    while sh < past:
        incl = incl + jnp.concatenate([incl[:, sh:], jnp.zeros((SUBLANES, sh), F32)], axis=1)
        sh *= 2
    later_pages = jnp.concatenate([incl[:, PAGE_SIZE:], jnp.zeros((SUBLANES, PAGE_SIZE), F32)], axis=1)
    valid = _iota((SUBLANES, past), 0) < N_HEADS
    logits = jnp.where(valid, s + lf + later_pages, 0.0)
    s_self = jnp.sum(qbd * kn_ref[pl.ds(b, 1), :], axis=1, keepdims=True) * scale
    pick = (_iota((SUBLANES, LANES), 1) == _iota((SUBLANES, LANES), 0)).astype(F32)
    c_new = jnp.sum(pick * lfn_ref[pl.ds(b, 1), :], axis=1, keepdims=True)
    self_logit = jnp.where(_iota((SUBLANES, 1), 0) < N_HEADS, s_self - c_new, 0.0)
    m = jnp.maximum(jnp.max(logits, axis=1, keepdims=True), self_logit)
    p = jnp.exp(logits - m)
    p_self = jnp.exp(self_logit - m)
    l = jnp.sum(p, axis=1, keepdims=True) + p_self
    acc = [jnp.zeros((HEAD_DIM, PAGE_SIZE), F32) for _ in heads]
    for pg in range(n_pages):
        vt = vbuf[slot, pg]
        cols = slice(pg * PAGE_SIZE, (pg + 1) * PAGE_SIZE)
        for h, rows in enumerate(heads):
            acc[h] = acc[h] + vt[rows, :] * jnp.broadcast_to(p[h:h + 1, cols], (HEAD_DIM, PAGE_SIZE))
    ocol = jnp.concatenate([jnp.sum(a, axis=1, keepdims=True) for a in acc], axis=0)
    eye = (_iota((VD, VD), 0) == _iota((VD, VD), 1)).astype(F32)
    orow = jnp.sum(eye * ocol, axis=0, keepdims=True)
    p_self_row = jnp.sum(p_self * hm, axis=0, keepdims=True)
    l_row = jnp.sum(l * hm, axis=0, keepdims=True)
    return (orow + p_self_row * vn_ref[pl.ds(b, 1), :]) / l_row


def _decode_gather(pt_ref, kc_hbm, vc_hbm, lfc_hbm, kbuf, vbuf, lfbuf, sem, *, layer, n_pages, per_step, base):
    i = pl.program_id(0)
    n_steps = pl.num_programs(0)

    def copies(sample, slot):
        out = []
        for p in range(n_pages):
            page = pt_ref[sample, p]
            cols = pl.ds(p * PAGE_SIZE, PAGE_SIZE)
            out.append(pltpu.make_async_copy(kc_hbm.at[layer, page], kbuf.at[slot, p], sem.at[slot, 0]))
            out.append(pltpu.make_async_copy(vc_hbm.at[layer, page], vbuf.at[slot, p], sem.at[slot, 1]))
            out.append(pltpu.make_async_copy(lfc_hbm.at[layer, page], lfbuf.at[slot, :, cols], sem.at[slot, 2]))
        return out

    cur = (i % 2) * per_step
    nxt = per_step - cur
    first = base + i * per_step

    @pl.when(i == 0)
    def _():
        for j in range(per_step):
            for cp in copies(base + j, j):
                cp.start()

    for j in range(per_step):
        for cp in copies(first + j, cur + j):
            cp.wait()

    @pl.when(i + 1 < n_steps)
    def _():
        for j in range(per_step):
            for cp in copies(first + per_step + j, nxt + j):
                cp.start()

    return [(first + j, cur + j) for j in range(per_step)]


def _ffn_decode_body(*refs, fc, n_pages, layer, per_step, base, mix):
    pt_ref, refs = refs[0], refs[1:]
    if mix:
        h_ref, ya_ref, yb_ref, yc_ref, ydp_ref, wo_ref, mpost_ref = refs[:7]
        refs = refs[7:]
    else:
        x_ref, refs = refs[0], refs[1:]
    (pre_ref, post_ref, wg_ref, wu_ref, wd_ref, q_ref, qt_ref, kn_ref, vn_ref, lfn_ref, kc_hbm, vc_hbm, lfc_hbm,
     o_ref, yd_ref, acc_ref, kbuf, vbuf, lfbuf, sem) = refs
    pairs = _decode_gather(pt_ref, kc_hbm, vc_hbm, lfc_hbm, kbuf, vbuf, lfbuf, sem,
                           layer=layer, n_pages=n_pages, per_step=per_step, base=base)
    if mix:
        ycat = jnp.concatenate([ya_ref[...], yb_ref[...], yc_ref[...], ydp_ref[...]], axis=1)
        x = h_ref[...] + _rms(_dot(ycat.astype(BF16), wo_ref[0]), mpost_ref[...])
    else:
        x = x_ref[...]
    for j, (sample, slot) in enumerate(pairs):
        yd_ref[j] = _decode_sample(sample, slot, q_ref, qt_ref, kn_ref, vn_ref, lfn_ref,
                                   kbuf, vbuf, lfbuf, n_pages)
    o_ref[...] = _swiglu_residual(x, pre_ref[...], post_ref[...],
                                  wg_ref.at[0], wu_ref.at[0], wd_ref.at[0], acc_ref, fc)


def _ffn_decode_call(layer, x, mix_in, pre_g, post_g, wg, wu, wd, page_table, q, qt, kn, vn, lfn, kc, vc, lfc,
                     base, n_dec, tm, fc=256):
    m, d = x.shape
    f = wg.shape[2]
    n, n_pages = page_table.shape
    steps = m // tm
    assert n_dec % steps == 0 and base + n_dec <= n
    per_step = n_dec // steps
    past = n_pages * PAGE_SIZE
    const = lambda shape: pl.BlockSpec(shape, lambda i, pt: (0,) * len(shape), pipeline_mode=pl.Buffered(1))
    lyr = lambda shape: pl.BlockSpec((1,) + shape, lambda i, pt: (layer, 0, 0), pipeline_mode=pl.Buffered(1))
    rows = lambda w: pl.BlockSpec((tm, w), lambda i, pt: (i, 0))
    anyspace = pl.BlockSpec(memory_space=pl.ANY)
    mix = mix_in is not None
    lead_specs, lead_args = [rows(d)], [x]
    if mix:
        ya, yb, yc, yd, wo, mpost = mix_in
        lead_specs += [rows(GROUP_W)] * 4 + [lyr((d, d)), const((1, d))]
        lead_args += [ya, yb, yc, yd, wo, mpost]
    grid_spec = pltpu.PrefetchScalarGridSpec(
        num_scalar_prefetch=1,
        grid=(steps,),
        in_specs=lead_specs + [const((1, d)), const((1, d)), lyr((d, f)), lyr((d, f)), lyr((f, d)),
                               const((n, VD)), const((VD, n)), const((n, VD)), const((n, VD)), const((n, LANES)),
                               anyspace, anyspace, anyspace],
        out_specs=[rows(d), pl.BlockSpec((per_step, 1, VD), lambda i, pt: (i, 0, 0))],
        scratch_shapes=[pltpu.VMEM((tm, d), F32),
                        pltpu.VMEM((2 * per_step, n_pages, VD, PAGE_SIZE), F32),
                        pltpu.VMEM((2 * per_step, n_pages, VD, PAGE_SIZE), F32),
                        pltpu.VMEM((2 * per_step, SUBLANES, past), F32),
                        pltpu.SemaphoreType.DMA((2 * per_step, 3))],
    )
    return pl.pallas_call(
        functools.partial(_ffn_decode_body, fc=fc, n_pages=n_pages, layer=layer, per_step=per_step,
                          base=base, mix=mix),
        grid_spec=grid_spec,
        out_shape=[jax.ShapeDtypeStruct((m, d), F32), jax.ShapeDtypeStruct((n_dec, 1, VD), F32)],
        compiler_params=_cparams("arbitrary"),
        name="mix_ffn_decode" if mix else "ffn_decode",
    )(page_table, *lead_args, pre_g, post_g, wg, wu, wd, q, qt, kn, vn, lfn, kc, vc, lfc)


ATTN_TILE = 256


def _row_tile(m, cap=512):
    tm = cap
    while tm >= SUBLANES:
        if m % tm == 0:
            return tm
        tm //= 2
    raise ValueError(f"row count {m} is not a multiple of {SUBLANES}")


def kernel(x_prompt, x_sample, state_conv_a, state_gla, state_conv_c, cache_k, cache_v, cache_logf, page_table, ffn1_pre_g, ffn1_post_g, ffn1_w_gate, ffn1_w_up, ffn1_w_down, mix_pre_g, mix_post_g, w_in, w_out, a_conv_w, a_conv_b, a_ln_g, a_ln_b, b_gate_w2, b_gate_b, b_out_norm_g, c_conv_w, d_forget_b, ffn2_pre_g, ffn2_post_g, ffn2_w_gate, ffn2_w_up, ffn2_w_down):
    depth = w_in.shape[0]
    bp, seq, d = x_prompt.shape
    bd = x_sample.shape[0]
    n_pool = cache_k.shape[1]
    assert x_sample.shape[1] == 1 and d == D_MODEL
    assert seq % ATTN_TILE == 0 and seq % GLA_CHUNK == 0 and seq >= CONV_A_W - 1

    hp = x_prompt.reshape(bp * seq, d)
    hs = x_sample.reshape(bd, d)
    tmp = _row_tile(bp * seq)
    tms = _row_tile(bd)
    tq = ATTN_TILE

    off_b = N_PA
    off_c = off_b + 2 * KD + 2 * VD + GLA_RANK
    off_d = off_c + N_PC
    row = lambda v: v.reshape(1, -1)

    lanes_of = lambda v, n=LANES: jnp.broadcast_to(v[:, None], (v.shape[0], n))

    ck = cache_k.transpose(0, 1, 3, 4, 2).reshape(depth, n_pool, VD, PAGE_SIZE)
    cv = cache_v.transpose(0, 1, 3, 4, 2).reshape(depth, n_pool, VD, PAGE_SIZE)
    sca = state_conv_a.transpose(0, 2, 1, 3)
    sgl = state_gla.transpose(0, 2, 3, 4, 1).reshape(depth, KD, HEAD_DIM, bd)
    scc = state_conv_c.reshape(depth, bd, (CONV_C_W - 1) * GROUP_W)
    w_in_t = w_in.transpose(0, 2, 1)

    lf_t = jnp.pad(cache_logf.transpose(0, 1, 3, 2), ((0, 0), (0, 0), (0, SUBLANES - N_HEADS), (0, 0)))
    lf_rows = depth * n_pool * SUBLANES
    lf_pages = _logf_pages_call(lf_t.reshape(lf_rows, PAGE_SIZE), _row_tile(lf_rows, cap=4096))
    lf_pages = lf_pages.reshape(depth, n_pool, SUBLANES, PAGE_SIZE)

    ffn1_ws = tuple(w.astype(BF16) for w in (ffn1_w_gate, ffn1_w_up, ffn1_w_down))
    ffn2_ws = tuple(w.astype(BF16) for w in (ffn2_w_gate, ffn2_w_up, ffn2_w_down))
    wo = w_out.astype(BF16)

    outs = [[] for _ in range(12)]
    for l in range(depth):
        wt = w_in_t[l].astype(BF16)
        pad_rows = lambda w, n: jnp.pad(w, ((0, n - w.shape[0]), (0, 0)))
        wq, wk, wv = (wt[off_d + i * VD:off_d + (i + 1) * VD] for i in range(3))
        in_w = (wt[0:off_b], pad_rows(wt[off_b:off_c], N_PB), wt[off_c:off_d],
                wq, wk, wv, pad_rows(wt[off_d + 3 * VD:], LANES))
        slabs = lambda w: jnp.pad(w.reshape(N_HEADS, HEAD_DIM, d),
                                  ((0, 0), (0, EXT - HEAD_DIM), (0, 0))).reshape(N_HEADS * EXT, d)
        in_w_prompt = in_w[0:3] + (wq, slabs(wk), wk, wv, in_w[6])
        bf_pad = jnp.pad(d_forget_b[l], (0, LANES - N_HEADS))
        bfr = row(bf_pad)
        bfc = lanes_of(bf_pad[0:SUBLANES])
        wa2 = jnp.pad(b_gate_w2[l], ((0, LANES - GLA_RANK), (0, 0))).astype(BF16)
        mix_w = (a_conv_w[l], row(a_conv_b[l]), row(a_ln_g[l]), row(a_ln_b[l]))
        ffn1_w = (row(ffn1_pre_g[l]), row(ffn1_post_g[l])) + ffn1_ws
        ffn2_w = (row(ffn2_pre_g[l]), row(ffn2_post_g[l])) + ffn2_ws

        hs = _ffn_call(l, hs, *ffn1_w, tm=tms)
        s_pa, s_pbt, s_pc, s_q, s_k, s_v, s_lf, s_kt, s_vt, s_lft, s_qt = _inproj_sample_call(
            hs, row(mix_pre_g[l]), in_w, bfc, bfr)
        s_ya, s_yb, s_yc, s_bufa, s_state, s_bufc = _mix_sample_call(
            l, s_pa, s_pbt, s_pc, sca, sgl, scc, *mix_w, wa2.T, lanes_of(b_gate_b[l], bd),
            lanes_of(b_out_norm_g[l], bd), c_conv_w[l])

        dec_args = (page_table, s_q, s_qt, s_k, s_v, s_lf, ck, cv, lf_pages)
        half = bd // 2
        hp, s_yd0 = _ffn_decode_call(l, hp, None, *ffn1_w, *dec_args, base=0, n_dec=half, tm=tmp)
        pa, pb, pc, kt, vt, lft, qe, ke, vtb = _inproj_prompt_call(
            hp, row(mix_pre_g[l]), in_w_prompt, bfc, bp, seq, tm=tmp)
        r3 = lambda t: t.reshape(bp, seq, t.shape[-1])
        ya, yc, buf_a, buf_c = _conv_prompt_call(r3(pa), r3(pc), *mix_w, c_conv_w[l])
        yb, s_b = _gla_prompt_call(r3(pb), wa2, row(b_gate_b[l]), row(b_out_norm_g[l]))
        yd = _fox_prompt_call(qe, ke, vtb, tq)
        f2 = lambda t: t.reshape(bp * seq, t.shape[-1])
        hp, s_yd1 = _ffn_decode_call(l, hp, (f2(ya), f2(yb), f2(yc), f2(yd), wo, row(mix_post_g[l])),
                                     *ffn2_w, *dec_args, base=half, n_dec=bd - half, tm=tmp)
        for i, t in enumerate((buf_a, s_b, buf_c, kt, vt, lft[:, 0:N_HEADS, :])):
            outs[i].append(t)

        s_yd = jnp.concatenate([s_yd0, s_yd1], axis=0).reshape(bd, VD)
        hs = _mix_ffn_call(l, hs, s_ya, s_yb, s_yc, s_yd, wo, row(mix_post_g[l]), *ffn2_w, tm=tms)
        for i, t in enumerate((s_bufa, s_state, s_bufc, s_kt, s_vt, s_lft[0:N_HEADS, :])):
            outs[6 + i].append(t)

    p_ca, p_gla, p_cc, p_kt, p_vt, p_lft, s_ca, s_gla, s_cc, s_kt, s_vt, s_lft = (jnp.stack(o) for o in outs)
    heads = lambda t: t.reshape(t.shape[:-2] + (N_HEADS, HEAD_DIM, t.shape[-1]))
    return (hp.reshape(bp, seq, d), hs.reshape(bd, 1, d),
            p_ca, p_gla, p_cc,
            heads(p_kt).transpose(0, 1, 4, 2, 3), heads(p_vt).transpose(0, 1, 4, 2, 3),
            p_lft.transpose(0, 1, 3, 2),
            s_ca.transpose(0, 2, 1, 3),
            s_gla.reshape(depth, N_HEADS, DK_B, HEAD_DIM, bd).transpose(0, 4, 1, 2, 3),
            s_cc.reshape(depth, bd, CONV_C_W - 1, GROUP_W),
            heads(s_kt).transpose(0, 3, 1, 2)[:, :, None], heads(s_vt).transpose(0, 3, 1, 2)[:, :, None],
            s_lft.transpose(0, 2, 1)[:, :, None])
```

```python
import functools

import jax
import jax.numpy as jnp
from jax import lax
from jax.experimental import pallas as pl
from jax.experimental.pallas import tpu as pltpu

F32 = jnp.float32
BF16 = jnp.bfloat16

D_MODEL = 1024
GROUP_W = D_MODEL // 4
HEAD_DIM = 64
N_HEADS = GROUP_W // HEAD_DIM
DK_B = HEAD_DIM // 2
GLA_RANK = 16
GLA_TAU = 16.0
GLA_CHUNK = 64
CONV_A_W = 31
CONV_C_W = 3
PAGE_SIZE = 128
EPS = 1e-6
NEG_INF = -1e30

LANES = 128
SUBLANES = 8
VMEM_LIMIT_BYTES = 56 * 1024 * 1024

N_PA = 2 * GROUP_W
KD = N_HEADS * DK_B
VD = N_HEADS * HEAD_DIM
N_PB = 2 * KD + 2 * VD + LANES
N_PC = 3 * GROUP_W


def _cparams(*sem):
    return pltpu.CompilerParams(dimension_semantics=sem, vmem_limit_bytes=VMEM_LIMIT_BYTES)


def _dot(a, b):
    return jnp.dot(a, b, preferred_element_type=F32)


def _dot_nt(a, b):
    return lax.dot_general(a, b, (((1,), (1,)), ((), ())), preferred_element_type=F32)


def _split3(x):
    hi = x.astype(BF16)
    r = x - hi.astype(F32)
    mid = r.astype(BF16)
    lo = (r - mid.astype(F32)).astype(BF16)
    return hi, mid, lo


def _dot_sel(x, sel):
    hi, mid, lo = _split3(x)
    return _dot(hi, sel) + _dot(mid, sel) + _dot(lo, sel)


def _sel_dot(sel, x):
    hi, mid, lo = _split3(x)
    return _dot(sel, hi) + _dot(sel, mid) + _dot(sel, lo)


def _rms(x, g):
    return x * lax.rsqrt(jnp.mean(x * x, axis=-1, keepdims=True) + EPS) * g


def _sigmoid(x):
    return 1.0 / (1.0 + jnp.exp(-x))


def _silu(x):
    return x * _sigmoid(x)


def _log_sigmoid(x):
    return jnp.minimum(x, 0.0) - jnp.log1p(jnp.exp(-jnp.abs(x)))


def _iota(shape, dim):
    return lax.broadcasted_iota(jnp.int32, shape, dim)


def _idiv(x, n):
    assert n & (n - 1) == 0
    return lax.shift_right_logical(x, n.bit_length() - 1)


def _imod(x, n):
    assert n & (n - 1) == 0
    return x & (n - 1)


def _head_mean_matrix():
    r = _idiv(_iota((VD, VD), 0), HEAD_DIM)
    c = _idiv(_iota((VD, VD), 1), HEAD_DIM)
    return (r == c).astype(BF16)


def _swiglu_residual(x, pre_g, post_g, wg_ref, wu_ref, wd_ref, acc_ref, fc):
    xn = _rms(x, pre_g).astype(BF16)
    for c in range(wg_ref.shape[1] // fc):
        sl = slice(c * fc, (c + 1) * fc)
        g = _dot(xn, wg_ref[:, sl])
        u = _dot(xn, wu_ref[:, sl])
        hid = (_silu(g) * u).astype(BF16)
        part = _dot(hid, wd_ref[sl, :])
        if c == 0:
            acc_ref[...] = part
        else:
            acc_ref[...] += part
    return x + 0.5 * _rms(acc_ref[...], post_g)


def _ffn_body(x_ref, pre_ref, post_ref, wg_ref, wu_ref, wd_ref, o_ref, acc_ref, *, fc):
    o_ref[...] = _swiglu_residual(x_ref[...], pre_ref[...], post_ref[...],
                                  wg_ref.at[0], wu_ref.at[0], wd_ref.at[0], acc_ref, fc)


def _mix_ffn_body(h_ref, ya_ref, yb_ref, yc_ref, yd_ref, wo_ref, mpost_ref,
                  pre_ref, post_ref, wg_ref, wu_ref, wd_ref, o_ref, acc_ref, *, fc):
    ycat = jnp.concatenate([ya_ref[...], yb_ref[...], yc_ref[...], yd_ref[...]], axis=1)
    y = _dot(ycat.astype(BF16), wo_ref[0])
    h = h_ref[...] + _rms(y, mpost_ref[...])
    o_ref[...] = _swiglu_residual(h, pre_ref[...], post_ref[...],
                                  wg_ref.at[0], wu_ref.at[0], wd_ref.at[0], acc_ref, fc)


def _row_spec(tm, n):
    return pl.BlockSpec((tm, n), lambda i: (i, 0))


def _full_spec(shape):
    return pl.BlockSpec(shape, lambda i: (0,) * len(shape), pipeline_mode=pl.Buffered(1))


def _ffn_call(layer, x, pre_g, post_g, wg, wu, wd, tm, fc=256):
    m, d = x.shape
    return pl.pallas_call(
        functools.partial(_ffn_body, fc=fc),
        grid=(m // tm,),
        in_specs=[_row_spec(tm, d), _full_spec((1, d)), _full_spec((1, d)),
                  _layer_spec(wg.shape, layer), _layer_spec(wu.shape, layer), _layer_spec(wd.shape, layer)],
        out_specs=_row_spec(tm, d),
        out_shape=jax.ShapeDtypeStruct((m, d), F32),
        scratch_shapes=[pltpu.VMEM((tm, d), F32)],
        compiler_params=_cparams("arbitrary"),
        name="ffn",
    )(x, pre_g, post_g, wg, wu, wd)


def _mix_ffn_call(layer, h, ya, yb, yc, yd, wo, mpost, pre_g, post_g, wg, wu, wd, tm, fc=256):
    m, d = h.shape
    return pl.pallas_call(
        functools.partial(_mix_ffn_body, fc=fc),
        grid=(m // tm,),
        in_specs=[_row_spec(tm, d)] + [_row_spec(tm, GROUP_W)] * 4
                 + [_layer_spec(wo.shape, layer), _full_spec((1, d)), _full_spec((1, d)), _full_spec((1, d)),
                    _layer_spec(wg.shape, layer), _layer_spec(wu.shape, layer), _layer_spec(wd.shape, layer)],
        out_specs=_row_spec(tm, d),
        out_shape=jax.ShapeDtypeStruct((m, d), F32),
        scratch_shapes=[pltpu.VMEM((tm, d), F32)],
        compiler_params=_cparams("arbitrary"),
        name="mix_ffn",
    )(h, ya, yb, yc, yd, wo, mpost, pre_g, post_g, wg, wu, wd)


EXT = LANES
C_KEY = HEAD_DIM
C_QRY = HEAD_DIM + 3


def _inproj_prompt_body(*refs, per, n_alias):
    (h_ref, g_ref, wa_ref, wb_ref, wc_ref, wq_ref, wke_ref, wk_ref, wv_ref, wl_ref, bfc_ref) = refs[:11]
    (pa_ref, pb_ref, pc_ref, kt_ref, vt_ref, lft_ref, qe_ref, ke_ref, vtb_ref, carry_ref) = refs[11 + n_alias:]
    tm = h_ref.shape[0]
    u = _rms(h_ref[...], g_ref[...]).astype(BF16)
    pa_ref[...] = _dot_nt(u, wa_ref[...])
    pb_ref[...] = _dot_nt(u, wb_ref[...])
    pc_ref[...] = _dot_nt(u, wc_ref[...])
    kt_ref[0, 0] = _dot_nt(wk_ref[...], u)
    vt = _dot_nt(wv_ref[...], u)
    vt_ref[0, 0] = vt
    vtb_ref[0] = vt.astype(BF16)
    lft = _log_sigmoid(_dot_nt(wl_ref[0:SUBLANES, :], u) + bfc_ref[:, 0:1])
    lft_ref[0] = lft

    first = pl.program_id(0) % per == 0
    carry = jnp.where(first, 0.0, carry_ref[:, 0:1])
    upto = (_iota((tm, tm), 0) <= _iota((tm, tm), 1)).astype(BF16)
    lfm = jnp.where(_iota((SUBLANES, tm), 0) < N_HEADS, lft, 0.0)
    c = _dot_sel(lfm, upto) + carry
    carry_ref[...] = jnp.broadcast_to(c[:, tm - 1:tm], carry_ref.shape)
    ccol = jnp.concatenate([c, jnp.zeros((LANES - SUBLANES, tm), F32)], axis=0).T

    hi, mid, lo = _split3(ccol)
    pieces = (hi.astype(F32) + pltpu.roll(mid.astype(F32), N_HEADS, 1)
              + pltpu.roll(lo.astype(F32), 2 * N_HEADS, 1)).astype(BF16)
    r = _iota((LANES, N_HEADS * EXT), 0)
    col = _iota((LANES, N_HEADS * EXT), 1)
    j = _imod(col, EXT) - C_KEY
    place_k = ((j >= 0) & (j < 3) & (r == N_HEADS * j + _idiv(col, EXT))).astype(BF16)
    lane = _imod(_iota((1, N_HEADS * EXT), 1), EXT)
    ones_k = ((lane >= C_QRY) & (lane < C_QRY + 3)).astype(F32)
    ke = _dot_nt(u, wke_ref[...]) + _dot(pieces, place_k) + ones_k
    for h in range(N_HEADS):
        ke_ref[0, h] = ke[:, h * EXT:(h + 1) * EXT].astype(BF16)

    hi, mid, lo = _split3(c)
    pieces_t = jnp.concatenate([hi.astype(F32), mid.astype(F32), lo.astype(F32),
                                jnp.zeros((LANES - 3 * SUBLANES, tm), F32)], axis=0).astype(BF16)
    r = _iota((VD, LANES), 0)
    col = _iota((VD, LANES), 1)
    j = _imod(r, HEAD_DIM) - (C_QRY - HEAD_DIM)
    place_q = ((j >= 0) & (j < 3) & (col == SUBLANES * j + _idiv(r, HEAD_DIM))).astype(BF16)
    rowi = _imod(_iota((VD, 1), 0), HEAD_DIM)
    neg_q = jnp.where(rowi < 3, -1.0, 0.0)
    extra = (_dot(place_q, pieces_t) + neg_q).astype(BF16)
    qt = (_dot_nt(wq_ref[...], u) * (HEAD_DIM ** -0.5)).astype(BF16)
    for h in range(N_HEADS):
        rows = slice(h * HEAD_DIM, (h + 1) * HEAD_DIM)
        qe_ref[0, h, 0:HEAD_DIM, :] = qt[rows, :]
        qe_ref[0, h, HEAD_DIM:, :] = extra[rows, :]


def _inproj_sample_body(h_ref, g_ref, wa_ref, wb_ref, wc_ref, wq_ref, wk_ref, wv_ref, wl_ref,
                        bfc_ref, bfr_ref,
                        pa_ref, pbt_ref, pc_ref, q_ref, k_ref, v_ref, lf_ref, kt_ref, vt_ref, lft_ref, qt_ref):
    u = _rms(h_ref[...], g_ref[...]).astype(BF16)
    pa_ref[...] = _dot_nt(u, wa_ref[...])
    pbt_ref[...] = _dot_nt(wb_ref[...], u)
    pc_ref[...] = _dot_nt(u, wc_ref[...])
    q_ref[...] = _dot_nt(u, wq_ref[...])
    qt_ref[...] = _dot_nt(wq_ref[...], u)
    k_ref[...] = _dot_nt(u, wk_ref[...])
    v_ref[...] = _dot_nt(u, wv_ref[...])
    lf_ref[...] = _log_sigmoid(_dot_nt(u, wl_ref[...]) + bfr_ref[...])
    kt_ref[...] = _dot_nt(wk_ref[...], u)
    vt_ref[...] = _dot_nt(wv_ref[...], u)
    lft_ref[...] = _log_sigmoid(_dot_nt(wl_ref[0:SUBLANES, :], u) + bfc_ref[:, 0:1])


def _inproj_weight_specs(d):
    return [_full_spec((n, d)) for n in (N_PA, N_PB, N_PC, VD, VD, VD, LANES)]


def _inproj_prompt_call(layer, depth, h, g, ws, bfc, bp, seq, tm, kv_prev):
    m, d = h.shape
    per = seq // tm
    tspec = lambda n: pl.BlockSpec((1, n, tm), lambda i: (i // per, 0, i % per))
    lspec = pl.BlockSpec((1, 1, VD, tm), lambda i: (layer, i // per, 0, i % per))
    hx = N_HEADS * EXT
    alias_args = tuple(kv_prev)
    n_in = 11
    return pl.pallas_call(
        functools.partial(_inproj_prompt_body, per=per, n_alias=len(alias_args)),
        grid=(m // tm,),
        in_specs=[_row_spec(tm, d), _full_spec((1, d))]
                 + [_full_spec((n, d)) for n in (N_PA, N_PB, N_PC, VD, hx, VD, VD, LANES)]
                 + [_full_spec((SUBLANES, LANES))]
                 + [pl.BlockSpec(memory_space=pl.ANY)] * len(alias_args),
        input_output_aliases={n_in + a: 3 + a for a in range(len(alias_args))},
        out_specs=[_row_spec(tm, N_PA), _row_spec(tm, N_PB), _row_spec(tm, N_PC),
                   lspec, lspec, tspec(SUBLANES),
                   pl.BlockSpec((1, N_HEADS, EXT, tm), lambda i: (i // per, 0, 0, i % per)),
                   pl.BlockSpec((1, N_HEADS, tm, EXT), lambda i: (i // per, 0, i % per, 0)),
                   tspec(VD)],
        out_shape=[jax.ShapeDtypeStruct((m, N_PA), F32), jax.ShapeDtypeStruct((m, N_PB), F32),
                   jax.ShapeDtypeStruct((m, N_PC), F32),
                   jax.ShapeDtypeStruct((depth, bp, VD, seq), F32),
                   jax.ShapeDtypeStruct((depth, bp, VD, seq), F32),
                   jax.ShapeDtypeStruct((bp, SUBLANES, seq), F32),
                   jax.ShapeDtypeStruct((bp, N_HEADS, EXT, seq), BF16),
                   jax.ShapeDtypeStruct((bp, N_HEADS, seq, EXT), BF16),
                   jax.ShapeDtypeStruct((bp, VD, seq), BF16)],
        scratch_shapes=[pltpu.VMEM((SUBLANES, LANES), F32)],
        compiler_params=_cparams("arbitrary"),
        name="inproj_prompt",
    )(h, g, *ws, bfc, *alias_args)


def _inproj_sample_call(h, g, ws, bfc, bfr):
    n, d = h.shape
    shapes = [(n, N_PA), (N_PB, n), (n, N_PC), (n, VD), (n, VD), (n, VD), (n, LANES),
              (VD, n), (VD, n), (SUBLANES, n), (VD, n)]
    return pl.pallas_call(
        _inproj_sample_body,
        grid=(1,),
        in_specs=[_full_spec((n, d)), _full_spec((1, d))] + _inproj_weight_specs(d)
                 + [_full_spec((SUBLANES, LANES)), _full_spec((1, LANES))],
        out_specs=[pl.BlockSpec(s, lambda i: (0, 0)) for s in shapes],
        out_shape=[jax.ShapeDtypeStruct(s, F32) for s in shapes],
        compiler_params=_cparams("arbitrary"),
        name="inproj_sample",
    )(h, g, *ws, bfc, bfr)


A_PAD = 32
C_PAD = 8
CONV_ROWS = 256


def _conv_prompt_body(pa_ref, pc_ref, aw_ref, ab_ref, lg_ref, lb_ref, cw_ref,
                      ya_ref, yc_ref, bufa_ref, bufc_ref, apad_ref, zpad_ref):
    seq = pa_ref.shape[1]
    c = GROUP_W
    apad_ref[0:A_PAD, :] = jnp.zeros((A_PAD, c), F32)
    apad_ref[A_PAD:, :] = pa_ref[0, :, 0:c] * _sigmoid(pa_ref[0, :, c:])
    zpad_ref[0:C_PAD, :] = jnp.zeros((C_PAD, c), F32)
    zpad_ref[C_PAD:, :] = pc_ref[0, :, c:2 * c] * pc_ref[0, :, 2 * c:]

    def step(i, carry):
        r0 = pl.multiple_of(i * CONV_ROWS, CONV_ROWS)
        win = apad_ref[pl.ds(r0, CONV_ROWS + A_PAD), :]
        acc = jnp.zeros((CONV_ROWS, c), F32) + ab_ref[...]
        for r in range(SUBLANES):
            nrow = CONV_ROWS if r == 0 else CONV_ROWS + SUBLANES
            u = None
            for a8 in range(0, A_PAD + 1, SUBLANES):
                j = a8 + r - (A_PAD - (CONV_A_W - 1))
                if 0 <= j < CONV_A_W:
                    term = aw_ref[j:j + 1, :] * win[a8:a8 + nrow, :]
                    u = term if u is None else u + term
            acc = acc + u[r:r + CONV_ROWS, :]
        mu = jnp.mean(acc, axis=-1, keepdims=True)
        var = jnp.mean(jnp.square(acc - mu), axis=-1, keepdims=True)
        yn = (acc - mu) * lax.rsqrt(var + EPS) * lg_ref[...] + lb_ref[...]
        ya_ref[0, pl.ds(r0, CONV_ROWS), :] = _silu(yn)
        zwin = zpad_ref[pl.ds(r0, CONV_ROWS + C_PAD), :]
        accc = jnp.zeros((CONV_ROWS, c), F32)
        for j in range(CONV_C_W):
            off = C_PAD - (CONV_C_W - 1) + j
            accc = accc + cw_ref[j:j + 1, :] * zwin[off:off + CONV_ROWS, :]
        yc_ref[0, pl.ds(r0, CONV_ROWS), :] = pc_ref[0, pl.ds(r0, CONV_ROWS), 0:c] * accc
        return carry

    lax.fori_loop(0, seq // CONV_ROWS, step, 0)
    na = CONV_A_W - 1
    nc = CONV_C_W - 1
    bufa_ref[0] = apad_ref[seq:A_PAD + seq, :][A_PAD - na:, :]
    bufc_ref[0] = zpad_ref[seq:C_PAD + seq, :][C_PAD - nc:, :]


def _conv_prompt_call(pa, pc, aw, ab, lg, lb, cw):
    b, seq, _ = pa.shape
    c = GROUP_W
    bspec = lambda n: pl.BlockSpec((1, seq, n), lambda i: (i, 0, 0))
    return pl.pallas_call(
        _conv_prompt_body,
        grid=(b,),
        in_specs=[bspec(N_PA), bspec(N_PC), _full_spec((CONV_A_W, c)), _full_spec((1, c)),
                  _full_spec((1, c)), _full_spec((1, c)), _full_spec((CONV_C_W, c))],
        out_specs=[bspec(c), bspec(c),
                   pl.BlockSpec((1, CONV_A_W - 1, c), lambda i: (i, 0, 0)),
                   pl.BlockSpec((1, CONV_C_W - 1, c), lambda i: (i, 0, 0))],
        out_shape=[jax.ShapeDtypeStruct((b, seq, c), F32), jax.ShapeDtypeStruct((b, seq, c), F32),
                   jax.ShapeDtypeStruct((b, CONV_A_W - 1, c), F32),
                   jax.ShapeDtypeStruct((b, CONV_C_W - 1, c), F32)],
        scratch_shapes=[pltpu.VMEM((A_PAD + seq, c), F32), pltpu.VMEM((C_PAD + seq, c), F32)],
        compiler_params=_cparams("arbitrary"),
        name="conv_prompt",
    )(pa, pc, aw, ab, lg, lb, cw)


def _gla_gate(alr, wa2_ref, ba_ref):
    return _log_sigmoid(_dot(alr.astype(BF16), wa2_ref[...]) + ba_ref[...]) * (1.0 / GLA_TAU)


def _gla_out(o, g, ng, hm):
    ms = _dot_sel(o * o, hm) * (1.0 / HEAD_DIM)
    return o * lax.rsqrt(ms + EPS) * ng * _silu(g)


GLA_SLAB = 256


def _group_row(x, g, r):
    n, c = x.shape
    x3 = x.reshape(n // g, g, c)
    return jnp.broadcast_to(x3[:, r:r + 1, :], (n // g, g, c)).reshape(n, c)


def _gla_prompt_body(pb_ref, wa2_ref, ba_ref, ng_ref, y_ref, st_ref, la_ref, s_ref):
    seq = pb_ref.shape[1]
    ck = GLA_CHUNK
    la_ref[...] = _gla_gate(pb_ref[0, :, 2 * KD + 2 * VD:], wa2_ref, ba_ref)
    s_ref[...] = jnp.zeros((KD, VD), F32)

    rs = min(GLA_SLAB, seq)
    nch = rs // ck
    tt = _iota((rs, rs), 0)
    ss = _iota((rs, rs), 1)
    tri = ((ss <= tt) & (_idiv(ss, ck) == _idiv(tt, ck))).astype(BF16)
    blk = _idiv(_iota((KD, VD), 0), DK_B) == _idiv(_iota((KD, VD), 1), HEAD_DIM)
    blk_bf = blk.astype(BF16)
    blk_f = blk.astype(F32)
    hm = _head_mean_matrix()
    scale = DK_B ** -0.5
    levels = [g for g in (2 * SUBLANES, 4 * SUBLANES, 8 * SUBLANES) if g <= ck]
    assert ck == 8 * SUBLANES
    key_head = _idiv(_iota((1, KD), 1), DK_B)
    val_head = _idiv(_iota((1, VD), 1), HEAD_DIM)
    tloc = _iota((rs, KD), 0)

    def slab(si, carry):
        r0 = pl.multiple_of(si * rs, rs)
        rows = pl.ds(r0, rs)
        q = pb_ref[0, rows, 0:KD] * scale
        k = pb_ref[0, rows, KD:2 * KD]
        v = pb_ref[0, rows, 2 * KD:2 * KD + VD]
        g = pb_ref[0, rows, 2 * KD + VD:2 * KD + 2 * VD]
        b = _sel_dot(tri, la_ref[rows, :])
        vb3 = v.astype(BF16).reshape(nch, ck, VD)

        t8 = _imod(tloc, SUBLANES)
        acc = jnp.zeros((rs, VD), F32)
        for s in range(SUBLANES):
            diff = jnp.where(t8 >= s, b - _group_row(b, SUBLANES, s), -jnp.inf)
            e = (q * _group_row(k, SUBLANES, s) * jnp.exp(diff)).astype(BF16)
            acc = acc + _dot(e, blk_bf) * _group_row(v, SUBLANES, s)

        sc = jnp.zeros((nch, N_HEADS * ck, ck), F32)
        for gsz in levels:
            r = _group_row(b, gsz, gsz // 2)
            upper = _imod(tloc, gsz) >= gsz // 2
            qf = q * jnp.exp(jnp.where(upper, b - r, -jnp.inf))
            kf = (k * jnp.exp(jnp.where(upper, -jnp.inf, r - b))).astype(BF16).reshape(nch, ck, KD)
            qstack = jnp.concatenate(
                [jnp.where(key_head == h, qf, 0.0).astype(BF16).reshape(nch, ck, KD)
                 for h in range(N_HEADS)], axis=1)
            sc_g = jnp.einsum('cqk,csk->cqs', qstack, kf, preferred_element_type=F32)
            same_group = _idiv(_imod(_iota((N_HEADS * ck, ck), 0), ck), gsz) == _idiv(_iota((N_HEADS * ck, ck), 1), gsz)
            sc = sc + jnp.where(same_group[None], sc_g, 0.0)
        res = jnp.einsum('cqs,csv->cqv', sc.astype(BF16), vb3, preferred_element_type=F32)
        for h in range(N_HEADS):
            part = res[:, h * ck:(h + 1) * ck, :].reshape(rs, VD)
            acc = acc + jnp.where(val_head == h, part, 0.0)

        b_last = _group_row(b, ck, ck - 1)
        kdec = k * jnp.exp(b_last - b)
        lastb = jnp.concatenate([b[(c + 1) * ck - 1:(c + 1) * ck, :] for c in range(nch)]
                                + [jnp.zeros((LANES - nch, KD), F32)], axis=0)
        dcols = jnp.exp(lastb.T)
        s_cur = s_ref[...]
        s_before = []
        for c in range(nch):
            s_before.append(s_cur.astype(BF16))
            crow = slice(c * ck, (c + 1) * ck)
            upd = _dot(kdec[crow, :].T.astype(BF16), vb3[c])
            s_cur = dcols[:, c:c + 1] * s_cur + upd * blk_f
        s_ref[...] = s_cur
        qb3 = (q * jnp.exp(b)).astype(BF16).reshape(nch, ck, KD)
        inter = jnp.einsum('cqk,ckv->cqv', qb3, jnp.stack(s_before), preferred_element_type=F32)
        o = inter.reshape(rs, VD) + acc
        y_ref[0, rows, :] = _gla_out(o, g, ng_ref[...], hm)
        return carry

    lax.fori_loop(0, seq // rs, slab, 0)
    for h in range(N_HEADS):
        st_ref[0, h] = s_ref[h * DK_B:(h + 1) * DK_B, h * HEAD_DIM:(h + 1) * HEAD_DIM]


def _gla_prompt_call(pb, wa2, ba, ng):
    b, seq, _ = pb.shape
    return pl.pallas_call(
        _gla_prompt_body,
        grid=(b,),
        in_specs=[pl.BlockSpec((1, seq, N_PB), lambda i: (i, 0, 0)), _full_spec((LANES, KD)),
                  _full_spec((1, KD)), _full_spec((1, VD))],
        out_specs=[pl.BlockSpec((1, seq, VD), lambda i: (i, 0, 0)),
                   pl.BlockSpec((1, N_HEADS, DK_B, HEAD_DIM), lambda i: (i, 0, 0, 0))],
        out_shape=[jax.ShapeDtypeStruct((b, seq, VD), F32),
                   jax.ShapeDtypeStruct((b, N_HEADS, DK_B, HEAD_DIM), F32)],
        scratch_shapes=[pltpu.VMEM((seq, KD), F32), pltpu.VMEM((KD, VD), F32)],
        compiler_params=_cparams("arbitrary"),
        name="gla_prompt",
    )(pb, wa2, ba, ng)


FOX_GROUP = 4


def _fox_prompt_body(qe_ref, ke_ref, vtb_ref, o_ref, *, tq):
    qi = pl.program_id(1)
    key_pos = _iota((tq, tq), 0)
    qry_pos = _iota((tq, tq), 1)

    def tile(ki):
        return pl.ds(pl.multiple_of(ki * tq, tq), tq)

    def consume(tiles, stats):
        scores = [[_dot(ke_ref[0, h, tile(ki), :], qe_ref[0, h]) for h in range(N_HEADS)]
                  for ki, _ in tiles]
        for (ki, masked), sc in zip(tiles, scores):
            soft = []
            for h in range(N_HEADS):
                m, l, _ = stats[h]
                s = sc[h]
                if masked:
                    s = jnp.where(key_pos <= qry_pos, s, NEG_INF)
                m_new = jnp.maximum(m, jnp.max(s, axis=0, keepdims=True))
                alpha = jnp.exp(m - m_new)
                p = jnp.exp(s - m_new)
                soft.append((m_new, alpha * l + jnp.sum(p, axis=0, keepdims=True), alpha, p.astype(BF16)))
            out = []
            for h in range(N_HEADS):
                m_new, l, alpha, p = soft[h]
                vh = vtb_ref[0, h * HEAD_DIM:(h + 1) * HEAD_DIM, tile(ki)]
                out.append((m_new, l, alpha * stats[h][2] + _dot(vh, p)))
            stats = tuple(out)
        return stats

    init = tuple((jnp.full((1, tq), NEG_INF, F32), jnp.zeros((1, tq), F32),
                  jnp.zeros((HEAD_DIM, tq), F32)) for _ in range(N_HEADS))
    g = FOX_GROUP
    stats = lax.fori_loop(
        0, qi // g, lambda k, st: consume([(g * k + t, False) for t in range(g)], st), init)
    left = qi % g
    tails = [functools.partial(consume, [(qi - r + t, False) for t in range(r)] + [(qi, True)])
             for r in range(g)]
    stats = lax.switch(left, tails, stats)
    o_ref[0] = jnp.concatenate([acc / l for _, l, acc in stats], axis=0).T


def _fox_prompt_call(qe, ke, vtb, tq):
    b, _, _, seq = qe.shape
    return pl.pallas_call(
        functools.partial(_fox_prompt_body, tq=tq),
        grid=(b, seq // tq),
        in_specs=[pl.BlockSpec((1, N_HEADS, EXT, tq), lambda i, j: (i, 0, 0, j)),
                  pl.BlockSpec((1, N_HEADS, seq, EXT), lambda i, j: (i, 0, 0, 0)),
                  pl.BlockSpec((1, VD, seq), lambda i, j: (i, 0, 0))],
        out_specs=pl.BlockSpec((1, tq, VD), lambda i, j: (i, j, 0)),
        out_shape=jax.ShapeDtypeStruct((b, seq, VD), F32),
        compiler_params=_cparams("arbitrary", "arbitrary"),
        name="fox_prompt",
    )(qe, ke, vtb)


def _mix_sample_body(pa_ref, pbt_ref, pc_ref, bufa_ref, sg_ref, bufc_ref,
                     aw_ref, ab_ref, lg_ref, lb_ref, wa2t_ref, bac_ref, ngc_ref, cw_ref,
                     ya_ref, yb_ref, yc_ref, bufa_o, sg_o, bufc_o, q_s, k_s, dec_s):
    c = GROUP_W
    a = pa_ref[:, 0:c] * _sigmoid(pa_ref[:, c:])
    na = CONV_A_W - 1
    acc = aw_ref[na:na + 1, :] * a + ab_ref[...]
    for j in range(na):
        acc = acc + aw_ref[j:j + 1, :] * bufa_ref[0, j]
    mu = jnp.mean(acc, axis=-1, keepdims=True)
    var = jnp.mean(jnp.square(acc - mu), axis=-1, keepdims=True)
    ya_ref[...] = _silu((acc - mu) * lax.rsqrt(var + EPS) * lg_ref[...] + lb_ref[...])
    for j in range(na - 1):
        bufa_o[j] = bufa_ref[0, j + 1]
    bufa_o[na - 1] = a
    z = pc_ref[:, c:2 * c] * pc_ref[:, 2 * c:]
    conv = (cw_ref[0:1, :] * bufc_ref[0, :, 0:c] + cw_ref[1:2, :] * bufc_ref[0, :, c:]
            + cw_ref[2:3, :] * z)
    yc_ref[...] = pc_ref[:, 0:c] * conv
    bufc_o[:, 0:c] = bufc_ref[0, :, c:]
    bufc_o[:, c:] = z
    q_s[...] = pbt_ref[0:KD, :] * (DK_B ** -0.5)
    k_s[...] = pbt_ref[KD:2 * KD, :]
    gate = _dot(wa2t_ref[...], pbt_ref[2 * KD + 2 * VD:, :].astype(BF16)) + bac_ref[...]
    dec_s[...] = jnp.exp(_log_sigmoid(gate) * (1.0 / GLA_TAU))
    n = pa_ref.shape[0]
    ys = []
    for h in range(N_HEADS):
        vrows = slice(2 * KD + h * HEAD_DIM, 2 * KD + (h + 1) * HEAD_DIM)
        grows = slice(2 * KD + VD + h * HEAD_DIM, 2 * KD + VD + (h + 1) * HEAD_DIM)
        vh = pbt_ref[vrows, :]

        def key_step(kk, o, h=h, vh=vh):
            hk = h * DK_B + kk
            one = pl.ds(hk, 1)
            s_new = dec_s[one, :] * sg_ref[0, hk] + k_s[one, :] * vh
            sg_o[hk] = s_new
            return o + q_s[one, :] * s_new

        o = lax.fori_loop(0, DK_B, key_step, jnp.zeros((HEAD_DIM, n), F32))
        ms = jnp.mean(o * o, axis=0, keepdims=True)
        ng = ngc_ref[h * HEAD_DIM:(h + 1) * HEAD_DIM, :]
        ys.append(o * lax.rsqrt(ms + EPS) * ng * _silu(pbt_ref[grows, :]))
    yb_ref[...] = jnp.concatenate(ys, axis=0).T


def _layer_spec(shape, l):
    return pl.BlockSpec((1,) + tuple(shape[1:]), lambda i: (l,) + (0,) * (len(shape) - 1),
                        pipeline_mode=pl.Buffered(1))


def _mix_sample_call(l, pa, pbt, pc, bufa, sg, bufc, aw, ab, lg, lb, wa2t, bac, ngc, cw):
    n = pa.shape[0]
    c = GROUP_W
    small = (aw, ab, lg, lb, wa2t, bac, ngc, cw)
    out_shape = [jax.ShapeDtypeStruct((n, c), F32)] * 3 + [
        jax.ShapeDtypeStruct(bufa.shape[1:], F32), jax.ShapeDtypeStruct(sg.shape[1:], F32),
        jax.ShapeDtypeStruct(bufc.shape[1:], F32)]
    return pl.pallas_call(
        _mix_sample_body,
        grid=(1,),
        in_specs=[_full_spec(pa.shape), _full_spec(pbt.shape), _full_spec(pc.shape),
                  _layer_spec(bufa.shape, l), _layer_spec(sg.shape, l), _layer_spec(bufc.shape, l)]
                 + [_full_spec(x.shape) for x in small],
        out_specs=[pl.BlockSpec(s.shape, lambda i, nd=len(s.shape): (0,) * nd) for s in out_shape],
        out_shape=out_shape,
        scratch_shapes=[pltpu.VMEM((KD, n), F32)] * 3,
        compiler_params=_cparams("arbitrary"),
        name="mix_sample",
    )(pa, pbt, pc, bufa, sg, bufc, *small)


def _logf_pages_body(x_ref, o_ref):
    n = x_ref.shape[0]
    x = x_ref[...]
    later = (_iota((PAGE_SIZE, PAGE_SIZE), 0) > _iota((PAGE_SIZE, PAGE_SIZE), 1)).astype(BF16)
    ones = jnp.ones((PAGE_SIZE, PAGE_SIZE), BF16)
    within = _dot_sel(x, later)
    total = pltpu.roll(_dot_sel(x, ones), N_HEADS, 0)
    o_ref[...] = jnp.where(_imod(_iota((n, PAGE_SIZE), 0), SUBLANES) < N_HEADS, within, total)


def _logf_pages_call(x, tm):
    m = x.shape[0]
    return pl.pallas_call(
        _logf_pages_body,
        grid=(m // tm,),
        in_specs=[_row_spec(tm, PAGE_SIZE)],
        out_specs=_row_spec(tm, PAGE_SIZE),
        out_shape=jax.ShapeDtypeStruct((m, PAGE_SIZE), F32),
        compiler_params=_cparams("arbitrary"),
        name="logf_pages",
    )(x)


def _decode_sample(b, slot, q_ref, qt_ref, kn_ref, vn_ref, lfn_ref, kbuf, vbuf, lfbuf, n_pages):
    past = n_pages * PAGE_SIZE
    scale = HEAD_DIM ** -0.5
    heads = [slice(h * HEAD_DIM, (h + 1) * HEAD_DIM) for h in range(N_HEADS)]
    hm = (_idiv(_iota((SUBLANES, VD), 1), HEAD_DIM) == _iota((SUBLANES, VD), 0)).astype(F32)
    qbd = jnp.broadcast_to(q_ref[pl.ds(b, 1), :], (SUBLANES, VD)) * hm
    onehot = (_iota((1, qt_ref.shape[1]), 1) == b).astype(F32)
    qcol = jnp.sum(qt_ref[...] * onehot, axis=1, keepdims=True) * scale
    qcb = jnp.broadcast_to(qcol, (VD, PAGE_SIZE))
    parts = [[] for _ in heads]
    for p in range(n_pages):
        prod = kbuf[slot, p] * qcb
        for h, rows in enumerate(heads):
            parts[h].append(prod[rows, :].reshape(HEAD_DIM // SUBLANES, SUBLANES, PAGE_SIZE).sum(axis=0))
    rowid = _iota((SUBLANES, past), 0)
    s = jnp.zeros((SUBLANES, past), F32)
    for h in range(N_HEADS):
        t = jnp.concatenate(parts[h], axis=1)
        for sh in (4, 2, 1):
            t = t + pltpu.roll(t, sh, 0)
        s = jnp.where(rowid == h, t, s)
    lf = lfbuf[slot]
    incl = pltpu.roll(lf, N_HEADS, 0)
    sh = PAGE_SIZE
    while sh < past:
        incl = incl + jnp.concatenate([incl[:, sh:], jnp.zeros((SUBLANES, sh), F32)], axis=1)
        sh *= 2
    later_pages = jnp.concatenate([incl[:, PAGE_SIZE:], jnp.zeros((SUBLANES, PAGE_SIZE), F32)], axis=1)
    valid = _iota((SUBLANES, past), 0) < N_HEADS
    logits = jnp.where(valid, s + lf + later_pages, 0.0)
    s_self = jnp.sum(qbd * kn_ref[pl.ds(b, 1), :], axis=1, keepdims=True) * scale
    pick = (_iota((SUBLANES, LANES), 1) == _iota((SUBLANES, LANES), 0)).astype(F32)
    c_new = jnp.sum(pick * lfn_ref[pl.ds(b, 1), :], axis=1, keepdims=True)
    self_logit = jnp.where(_iota((SUBLANES, 1), 0) < N_HEADS, s_self - c_new, 0.0)
    m = jnp.maximum(jnp.max(logits, axis=1, keepdims=True), self_logit)
    p = jnp.exp(logits - m)
    p_self = jnp.exp(self_logit - m)
    l = jnp.sum(p, axis=1, keepdims=True) + p_self
    acc = [jnp.zeros((HEAD_DIM, PAGE_SIZE), F32) for _ in heads]
    for pg in range(n_pages):
        vt = vbuf[slot, pg]
        cols = slice(pg * PAGE_SIZE, (pg + 1) * PAGE_SIZE)
        for h, rows in enumerate(heads):
            acc[h] = acc[h] + vt[rows, :] * jnp.broadcast_to(p[h:h + 1, cols], (HEAD_DIM, PAGE_SIZE))
    ocol = jnp.concatenate([jnp.sum(a, axis=1, keepdims=True) for a in acc], axis=0)
    eye = (_iota((VD, VD), 0) == _iota((VD, VD), 1)).astype(F32)
    orow = jnp.sum(eye * ocol, axis=0, keepdims=True)
    p_self_row = jnp.sum(p_self * hm, axis=0, keepdims=True)
    l_row = jnp.sum(l * hm, axis=0, keepdims=True)
    return (orow + p_self_row * vn_ref[pl.ds(b, 1), :]) / l_row


def _decode_gather(pt_ref, kc_hbm, vc_hbm, lfc_hbm, kbuf, vbuf, lfbuf, sem, *, layer, n_pages, per_step, base):
    i = pl.program_id(0)
    n_steps = pl.num_programs(0)

    def copies(sample, slot):
        out = []
        for p in range(n_pages):
            page = pt_ref[sample, p]
            cols = pl.ds(p * PAGE_SIZE, PAGE_SIZE)
            out.append(pltpu.make_async_copy(kc_hbm.at[layer, page], kbuf.at[slot, p], sem.at[slot, 0]))
            out.append(pltpu.make_async_copy(vc_hbm.at[layer, page], vbuf.at[slot, p], sem.at[slot, 1]))
            out.append(pltpu.make_async_copy(lfc_hbm.at[layer, page], lfbuf.at[slot, :, cols], sem.at[slot, 2]))
        return out

    cur = (i % 2) * per_step
    nxt = per_step - cur
    first = base + i * per_step

    @pl.when(i == 0)
    def _():
        for j in range(per_step):
            for cp in copies(base + j, j):
                cp.start()

    for j in range(per_step):
        for cp in copies(first + j, cur + j):
            cp.wait()

    @pl.when(i + 1 < n_steps)
    def _():
        for j in range(per_step):
            for cp in copies(first + per_step + j, nxt + j):
                cp.start()

    return [(first + j, cur + j) for j in range(per_step)]


def _ffn_decode_body(*refs, fc, n_pages, layer, per_step, base, mix):
    pt_ref, refs = refs[0], refs[1:]
    if mix:
        h_ref, ya_ref, yb_ref, yc_ref, ydp_ref, wo_ref, mpost_ref = refs[:7]
        refs = refs[7:]
    else:
        x_ref, refs = refs[0], refs[1:]
    (pre_ref, post_ref, wg_ref, wu_ref, wd_ref, q_ref, qt_ref, kn_ref, vn_ref, lfn_ref, kc_hbm, vc_hbm, lfc_hbm,
     o_ref, yd_ref, acc_ref, kbuf, vbuf, lfbuf, sem) = refs
    pairs = _decode_gather(pt_ref, kc_hbm, vc_hbm, lfc_hbm, kbuf, vbuf, lfbuf, sem,
                           layer=layer, n_pages=n_pages, per_step=per_step, base=base)
    if mix:
        ycat = jnp.concatenate([ya_ref[...], yb_ref[...], yc_ref[...], ydp_ref[...]], axis=1)
        x = h_ref[...] + _rms(_dot(ycat.astype(BF16), wo_ref[0]), mpost_ref[...])
    else:
        x = x_ref[...]
    for j, (sample, slot) in enumerate(pairs):
        yd_ref[j] = _decode_sample(sample, slot, q_ref, qt_ref, kn_ref, vn_ref, lfn_ref,
                                   kbuf, vbuf, lfbuf, n_pages)
    o_ref[...] = _swiglu_residual(x, pre_ref[...], post_ref[...],
                                  wg_ref.at[0], wu_ref.at[0], wd_ref.at[0], acc_ref, fc)


def _ffn_decode_call(layer, x, mix_in, pre_g, post_g, wg, wu, wd, page_table, q, qt, kn, vn, lfn, kc, vc, lfc,
                     base, n_dec, tm, fc=256):
    m, d = x.shape
    f = wg.shape[2]
    n, n_pages = page_table.shape
    steps = m // tm
    assert n_dec % steps == 0 and base + n_dec <= n
    per_step = n_dec // steps
    past = n_pages * PAGE_SIZE
    const = lambda shape: pl.BlockSpec(shape, lambda i, pt: (0,) * len(shape), pipeline_mode=pl.Buffered(1))
    lyr = lambda shape: pl.BlockSpec((1,) + shape, lambda i, pt: (layer, 0, 0), pipeline_mode=pl.Buffered(1))
    rows = lambda w: pl.BlockSpec((tm, w), lambda i, pt: (i, 0))
    anyspace = pl.BlockSpec(memory_space=pl.ANY)
    mix = mix_in is not None
    lead_specs, lead_args = [rows(d)], [x]
    if mix:
        ya, yb, yc, yd, wo, mpost = mix_in
        lead_specs += [rows(GROUP_W)] * 4 + [lyr((d, d)), const((1, d))]
        lead_args += [ya, yb, yc, yd, wo, mpost]
    grid_spec = pltpu.PrefetchScalarGridSpec(
        num_scalar_prefetch=1,
        grid=(steps,),
        in_specs=lead_specs + [const((1, d)), const((1, d)), lyr((d, f)), lyr((d, f)), lyr((f, d)),
                               const((n, VD)), const((VD, n)), const((n, VD)), const((n, VD)), const((n, LANES)),
                               anyspace, anyspace, anyspace],
        out_specs=[rows(d), pl.BlockSpec((per_step, 1, VD), lambda i, pt: (i, 0, 0))],
        scratch_shapes=[pltpu.VMEM((tm, d), F32),
                        pltpu.VMEM((2 * per_step, n_pages, VD, PAGE_SIZE), F32),
                        pltpu.VMEM((2 * per_step, n_pages, VD, PAGE_SIZE), F32),
                        pltpu.VMEM((2 * per_step, SUBLANES, past), F32),
                        pltpu.SemaphoreType.DMA((2 * per_step, 3))],
    )
    return pl.pallas_call(
        functools.partial(_ffn_decode_body, fc=fc, n_pages=n_pages, layer=layer, per_step=per_step,
                          base=base, mix=mix),
        grid_spec=grid_spec,
        out_shape=[jax.ShapeDtypeStruct((m, d), F32), jax.ShapeDtypeStruct((n_dec, 1, VD), F32)],
        compiler_params=_cparams("arbitrary"),
        name="mix_ffn_decode" if mix else "ffn_decode",
    )(page_table, *lead_args, pre_g, post_g, wg, wu, wd, q, qt, kn, vn, lfn, kc, vc, lfc)


ATTN_TILE = 256


def _row_tile(m, cap=512):
    tm = cap
    while tm >= SUBLANES:
        if m % tm == 0:
            return tm
        tm //= 2
    raise ValueError(f"row count {m} is not a multiple of {SUBLANES}")


def kernel(x_prompt, x_sample, state_conv_a, state_gla, state_conv_c, cache_k, cache_v, cache_logf, page_table, ffn1_pre_g, ffn1_post_g, ffn1_w_gate, ffn1_w_up, ffn1_w_down, mix_pre_g, mix_post_g, w_in, w_out, a_conv_w, a_conv_b, a_ln_g, a_ln_b, b_gate_w2, b_gate_b, b_out_norm_g, c_conv_w, d_forget_b, ffn2_pre_g, ffn2_post_g, ffn2_w_gate, ffn2_w_up, ffn2_w_down):
    depth = w_in.shape[0]
    bp, seq, d = x_prompt.shape
    bd = x_sample.shape[0]
    n_pool = cache_k.shape[1]
    assert x_sample.shape[1] == 1 and d == D_MODEL
    assert seq % ATTN_TILE == 0 and seq % GLA_CHUNK == 0 and seq >= CONV_A_W - 1

    hp = x_prompt.reshape(bp * seq, d)
    hs = x_sample.reshape(bd, d)
    tmp = _row_tile(bp * seq)
    tms = _row_tile(bd)
    tq = ATTN_TILE

    off_b = N_PA
    off_c = off_b + 2 * KD + 2 * VD + GLA_RANK
    off_d = off_c + N_PC
    row = lambda v: v.reshape(1, -1)

    lanes_of = lambda v, n=LANES: jnp.broadcast_to(v[:, None], (v.shape[0], n))

    ck = cache_k.transpose(0, 1, 3, 4, 2).reshape(depth, n_pool, VD, PAGE_SIZE)
    cv = cache_v.transpose(0, 1, 3, 4, 2).reshape(depth, n_pool, VD, PAGE_SIZE)
    sca = state_conv_a.transpose(0, 2, 1, 3)
    sgl = state_gla.transpose(0, 2, 3, 4, 1).reshape(depth, KD, HEAD_DIM, bd)
    scc = state_conv_c.reshape(depth, bd, (CONV_C_W - 1) * GROUP_W)
    w_in_t = w_in.transpose(0, 2, 1)

    lf_t = jnp.pad(cache_logf.transpose(0, 1, 3, 2), ((0, 0), (0, 0), (0, SUBLANES - N_HEADS), (0, 0)))
    lf_rows = depth * n_pool * SUBLANES
    lf_pages = _logf_pages_call(lf_t.reshape(lf_rows, PAGE_SIZE), _row_tile(lf_rows, cap=4096))
    lf_pages = lf_pages.reshape(depth, n_pool, SUBLANES, PAGE_SIZE)

    ffn1_ws = tuple(w.astype(BF16) for w in (ffn1_w_gate, ffn1_w_up, ffn1_w_down))
    ffn2_ws = tuple(w.astype(BF16) for w in (ffn2_w_gate, ffn2_w_up, ffn2_w_down))
    wo = w_out.astype(BF16)

    outs = [[] for _ in range(12)]
    kv_prev = (jnp.zeros((depth, bp, VD, seq), F32), jnp.zeros((depth, bp, VD, seq), F32))
    for l in range(depth):
        wt = w_in_t[l].astype(BF16)
        pad_rows = lambda w, n: jnp.pad(w, ((0, n - w.shape[0]), (0, 0)))
        wq, wk, wv = (wt[off_d + i * VD:off_d + (i + 1) * VD] for i in range(3))
        in_w = (wt[0:off_b], pad_rows(wt[off_b:off_c], N_PB), wt[off_c:off_d],
                wq, wk, wv, pad_rows(wt[off_d + 3 * VD:], LANES))
        slabs = lambda w: jnp.pad(w.reshape(N_HEADS, HEAD_DIM, d),
                                  ((0, 0), (0, EXT - HEAD_DIM), (0, 0))).reshape(N_HEADS * EXT, d)
        in_w_prompt = in_w[0:3] + (wq, slabs(wk), wk, wv, in_w[6])
        bf_pad = jnp.pad(d_forget_b[l], (0, LANES - N_HEADS))
        bfr = row(bf_pad)
        bfc = lanes_of(bf_pad[0:SUBLANES])
        wa2 = jnp.pad(b_gate_w2[l], ((0, LANES - GLA_RANK), (0, 0))).astype(BF16)
        mix_w = (a_conv_w[l], row(a_conv_b[l]), row(a_ln_g[l]), row(a_ln_b[l]))
        ffn1_w = (row(ffn1_pre_g[l]), row(ffn1_post_g[l])) + ffn1_ws
        ffn2_w = (row(ffn2_pre_g[l]), row(ffn2_post_g[l])) + ffn2_ws

        hs = _ffn_call(l, hs, *ffn1_w, tm=tms)
        s_pa, s_pbt, s_pc, s_q, s_k, s_v, s_lf, s_kt, s_vt, s_lft, s_qt = _inproj_sample_call(
            hs, row(mix_pre_g[l]), in_w, bfc, bfr)
        s_ya, s_yb, s_yc, s_bufa, s_state, s_bufc = _mix_sample_call(
            l, s_pa, s_pbt, s_pc, sca, sgl, scc, *mix_w, wa2.T, lanes_of(b_gate_b[l], bd),
            lanes_of(b_out_norm_g[l], bd), c_conv_w[l])

        dec_args = (page_table, s_q, s_qt, s_k, s_v, s_lf, ck, cv, lf_pages)
        half = bd // 2
        hp, s_yd0 = _ffn_decode_call(l, hp, None, *ffn1_w, *dec_args, base=0, n_dec=half, tm=tmp)
        pa, pb, pc, kt_all, vt_all, lft, qe, ke, vtb = _inproj_prompt_call(
            l, depth, hp, row(mix_pre_g[l]), in_w_prompt, bfc, bp, seq, tm=tmp, kv_prev=kv_prev)
        kv_prev = (kt_all, vt_all)
        r3 = lambda t: t.reshape(bp, seq, t.shape[-1])
        ya, yc, buf_a, buf_c = _conv_prompt_call(r3(pa), r3(pc), *mix_w, c_conv_w[l])
        yb, s_b = _gla_prompt_call(r3(pb), wa2, row(b_gate_b[l]), row(b_out_norm_g[l]))
        yd = _fox_prompt_call(qe, ke, vtb, tq)
        f2 = lambda t: t.reshape(bp * seq, t.shape[-1])
        hp, s_yd1 = _ffn_decode_call(l, hp, (f2(ya), f2(yb), f2(yc), f2(yd), wo, row(mix_post_g[l])),
                                     *ffn2_w, *dec_args, base=half, n_dec=bd - half, tm=tmp)
        for i, t in ((0, buf_a), (1, s_b), (2, buf_c), (5, lft[:, 0:N_HEADS, :])):
            outs[i].append(t)

        s_yd = jnp.concatenate([s_yd0, s_yd1], axis=0).reshape(bd, VD)
        hs = _mix_ffn_call(l, hs, s_ya, s_yb, s_yc, s_yd, wo, row(mix_post_g[l]), *ffn2_w, tm=tms)
        for i, t in enumerate((s_bufa, s_state, s_bufc, s_kt, s_vt, s_lft[0:N_HEADS, :])):
            outs[6 + i].append(t)

    p_ca, p_gla, p_cc, _, _, p_lft, s_ca, s_gla, s_cc, s_kt, s_vt, s_lft = (
        jnp.stack(o) if o else None for o in outs)
    p_kt, p_vt = kv_prev
    heads = lambda t: t.reshape(t.shape[:-2] + (N_HEADS, HEAD_DIM, t.shape[-1]))
    return (hp.reshape(bp, seq, d), hs.reshape(bd, 1, d),
            p_ca, p_gla, p_cc,
            heads(p_kt).transpose(0, 1, 4, 2, 3), heads(p_vt).transpose(0, 1, 4, 2, 3),
            p_lft.transpose(0, 1, 3, 2),
            s_ca.transpose(0, 2, 1, 3),
            s_gla.reshape(depth, N_HEADS, DK_B, HEAD_DIM, bd).transpose(0, 4, 1, 2, 3),
            s_cc.reshape(depth, bd, CONV_C_W - 1, GROUP_W),
            heads(s_kt).transpose(0, 3, 1, 2)[:, :, None], heads(s_vt).transpose(0, 3, 1, 2)[:, :, None],
            s_lft.transpose(0, 2, 1)[:, :, None])
```

```python
import functools

import jax
import jax.numpy as jnp
from jax import lax
from jax.experimental import pallas as pl
from jax.experimental.pallas import tpu as pltpu

F32 = jnp.float32
BF16 = jnp.bfloat16

D_MODEL = 1024
GROUP_W = D_MODEL // 4
HEAD_DIM = 64
N_HEADS = GROUP_W // HEAD_DIM
DK_B = HEAD_DIM // 2
GLA_RANK = 16
GLA_TAU = 16.0
GLA_CHUNK = 64
CONV_A_W = 31
CONV_C_W = 3
PAGE_SIZE = 128
EPS = 1e-6
NEG_INF = -1e30

LANES = 128
SUBLANES = 8
VMEM_LIMIT_BYTES = 56 * 1024 * 1024

N_PA = 2 * GROUP_W
KD = N_HEADS * DK_B
VD = N_HEADS * HEAD_DIM
N_PB = 2 * KD + 2 * VD + LANES
N_PC = 3 * GROUP_W


def _cparams(*sem):
    return pltpu.CompilerParams(dimension_semantics=sem, vmem_limit_bytes=VMEM_LIMIT_BYTES)


def _dot(a, b):
    return jnp.dot(a, b, preferred_element_type=F32)


def _dot_nt(a, b):
    return lax.dot_general(a, b, (((1,), (1,)), ((), ())), preferred_element_type=F32)


def _split3(x):
    hi = x.astype(BF16)
    r = x - hi.astype(F32)
    mid = r.astype(BF16)
    lo = (r - mid.astype(F32)).astype(BF16)
    return hi, mid, lo


def _dot_sel(x, sel):
    hi, mid, lo = _split3(x)
    return _dot(hi, sel) + _dot(mid, sel) + _dot(lo, sel)


def _sel_dot(sel, x):
    hi, mid, lo = _split3(x)
    return _dot(sel, hi) + _dot(sel, mid) + _dot(sel, lo)


def _rms(x, g):
    return x * lax.rsqrt(jnp.mean(x * x, axis=-1, keepdims=True) + EPS) * g


def _sigmoid(x):
    return 1.0 / (1.0 + jnp.exp(-x))


def _silu(x):
    return x * _sigmoid(x)


def _log_sigmoid(x):
    return jnp.minimum(x, 0.0) - jnp.log1p(jnp.exp(-jnp.abs(x)))


def _iota(shape, dim):
    return lax.broadcasted_iota(jnp.int32, shape, dim)


def _idiv(x, n):
    assert n & (n - 1) == 0
    return lax.shift_right_logical(x, n.bit_length() - 1)


def _imod(x, n):
    assert n & (n - 1) == 0
    return x & (n - 1)


def _head_mean_matrix():
    r = _idiv(_iota((VD, VD), 0), HEAD_DIM)
    c = _idiv(_iota((VD, VD), 1), HEAD_DIM)
    return (r == c).astype(BF16)


def _swiglu_residual(x, pre_g, post_g, wg_ref, wu_ref, wd_ref, acc_ref, fc):
    xn = _rms(x, pre_g).astype(BF16)
    for c in range(wg_ref.shape[1] // fc):
        sl = slice(c * fc, (c + 1) * fc)
        g = _dot(xn, wg_ref[:, sl])
        u = _dot(xn, wu_ref[:, sl])
        hid = (_silu(g) * u).astype(BF16)
        part = _dot(hid, wd_ref[sl, :])
        if c == 0:
            acc_ref[...] = part
        else:
            acc_ref[...] += part
    return x + 0.5 * _rms(acc_ref[...], post_g)


def _ffn_body(x_ref, pre_ref, post_ref, wg_ref, wu_ref, wd_ref, o_ref, acc_ref, *, fc):
    o_ref[...] = _swiglu_residual(x_ref[...], pre_ref[...], post_ref[...],
                                  wg_ref.at[0], wu_ref.at[0], wd_ref.at[0], acc_ref, fc)


def _mix_ffn_body(h_ref, ya_ref, yb_ref, yc_ref, yd_ref, wo_ref, mpost_ref,
                  pre_ref, post_ref, wg_ref, wu_ref, wd_ref, o_ref, acc_ref, *, fc):
    ycat = jnp.concatenate([ya_ref[...], yb_ref[...], yc_ref[...], yd_ref[...]], axis=1)
    y = _dot(ycat.astype(BF16), wo_ref[0])
    h = h_ref[...] + _rms(y, mpost_ref[...])
    o_ref[...] = _swiglu_residual(h, pre_ref[...], post_ref[...],
                                  wg_ref.at[0], wu_ref.at[0], wd_ref.at[0], acc_ref, fc)


def _row_spec(tm, n):
    return pl.BlockSpec((tm, n), lambda i: (i, 0))


def _full_spec(shape):
    return pl.BlockSpec(shape, lambda i: (0,) * len(shape), pipeline_mode=pl.Buffered(1))


def _ffn_call(layer, x, pre_g, post_g, wg, wu, wd, tm, fc=256):
    m, d = x.shape
    return pl.pallas_call(
        functools.partial(_ffn_body, fc=fc),
        grid=(m // tm,),
        in_specs=[_row_spec(tm, d), _full_spec((1, d)), _full_spec((1, d)),
                  _layer_spec(wg.shape, layer), _layer_spec(wu.shape, layer), _layer_spec(wd.shape, layer)],
        out_specs=_row_spec(tm, d),
        out_shape=jax.ShapeDtypeStruct((m, d), F32),
        scratch_shapes=[pltpu.VMEM((tm, d), F32)],
        compiler_params=_cparams("arbitrary"),
        name="ffn",
    )(x, pre_g, post_g, wg, wu, wd)


def _mix_ffn_call(layer, h, ya, yb, yc, yd, wo, mpost, pre_g, post_g, wg, wu, wd, tm, fc=256):
    m, d = h.shape
    return pl.pallas_call(
        functools.partial(_mix_ffn_body, fc=fc),
        grid=(m // tm,),
        in_specs=[_row_spec(tm, d)] + [_row_spec(tm, GROUP_W)] * 4
                 + [_layer_spec(wo.shape, layer), _full_spec((1, d)), _full_spec((1, d)), _full_spec((1, d)),
                    _layer_spec(wg.shape, layer), _layer_spec(wu.shape, layer), _layer_spec(wd.shape, layer)],
        out_specs=_row_spec(tm, d),
        out_shape=jax.ShapeDtypeStruct((m, d), F32),
        scratch_shapes=[pltpu.VMEM((tm, d), F32)],
        compiler_params=_cparams("arbitrary"),
        name="mix_ffn",
    )(h, ya, yb, yc, yd, wo, mpost, pre_g, post_g, wg, wu, wd)


EXT = LANES
C_KEY = HEAD_DIM
C_QRY = HEAD_DIM + 3


def _inproj_prompt_body(*refs, per, n_alias):
    (h_ref, g_ref, wa_ref, wb_ref, wc_ref, wq_ref, wke_ref, wk_ref, wv_ref, wl_ref, bfc_ref) = refs[:11]
    (pa_ref, pb_ref, pc_ref, kt_ref, vt_ref, lft_ref, qe_ref, ke_ref, vtb_ref, carry_ref) = refs[11 + n_alias:]
    tm = h_ref.shape[0]
    u = _rms(h_ref[...], g_ref[...]).astype(BF16)
    pa_ref[...] = _dot_nt(u, wa_ref[...])
    pb_ref[...] = _dot_nt(u, wb_ref[...])
    pc_ref[...] = _dot_nt(u, wc_ref[...])
    kt_ref[0, 0] = _dot_nt(wk_ref[...], u)
    vt = _dot_nt(wv_ref[...], u)
    vt_ref[0, 0] = vt
    vtb_ref[0] = vt.astype(BF16)
    lft = _log_sigmoid(_dot_nt(wl_ref[0:SUBLANES, :], u) + bfc_ref[:, 0:1])
    lft_ref[0] = lft

    first = pl.program_id(0) % per == 0
    carry = jnp.where(first, 0.0, carry_ref[:, 0:1])
    upto = (_iota((tm, tm), 0) <= _iota((tm, tm), 1)).astype(BF16)
    lfm = jnp.where(_iota((SUBLANES, tm), 0) < N_HEADS, lft, 0.0)
    c = _dot_sel(lfm, upto) + carry
    carry_ref[...] = jnp.broadcast_to(c[:, tm - 1:tm], carry_ref.shape)
    ccol = jnp.concatenate([c, jnp.zeros((LANES - SUBLANES, tm), F32)], axis=0).T

    hi, mid, lo = _split3(ccol)
    pieces = (hi.astype(F32) + pltpu.roll(mid.astype(F32), N_HEADS, 1)
              + pltpu.roll(lo.astype(F32), 2 * N_HEADS, 1)).astype(BF16)
    r = _iota((LANES, N_HEADS * EXT), 0)
    col = _iota((LANES, N_HEADS * EXT), 1)
    j = _imod(col, EXT) - C_KEY
    place_k = ((j >= 0) & (j < 3) & (r == N_HEADS * j + _idiv(col, EXT))).astype(BF16)
    lane = _imod(_iota((1, N_HEADS * EXT), 1), EXT)
    ones_k = ((lane >= C_QRY) & (lane < C_QRY + 3)).astype(F32)
    ke = _dot_nt(u, wke_ref[...]) + _dot(pieces, place_k) + ones_k
    for h in range(N_HEADS):
        ke_ref[0, h] = ke[:, h * EXT:(h + 1) * EXT].astype(BF16)

    hi, mid, lo = _split3(c)
    pieces_t = jnp.concatenate([hi.astype(F32), mid.astype(F32), lo.astype(F32),
                                jnp.zeros((LANES - 3 * SUBLANES, tm), F32)], axis=0).astype(BF16)
    r = _iota((VD, LANES), 0)
    col = _iota((VD, LANES), 1)
    j = _imod(r, HEAD_DIM) - (C_QRY - HEAD_DIM)
    place_q = ((j >= 0) & (j < 3) & (col == SUBLANES * j + _idiv(r, HEAD_DIM))).astype(BF16)
    rowi = _imod(_iota((VD, 1), 0), HEAD_DIM)
    neg_q = jnp.where(rowi < 3, -1.0, 0.0)
    extra = (_dot(place_q, pieces_t) + neg_q).astype(BF16)
    qt = (_dot_nt(wq_ref[...], u) * (HEAD_DIM ** -0.5)).astype(BF16)
    for h in range(N_HEADS):
        rows = slice(h * HEAD_DIM, (h + 1) * HEAD_DIM)
        qe_ref[0, h, 0:HEAD_DIM, :] = qt[rows, :]
        qe_ref[0, h, HEAD_DIM:, :] = extra[rows, :]


def _inproj_sample_body(h_ref, g_ref, wa_ref, wb_ref, wc_ref, wq_ref, wk_ref, wv_ref, wl_ref,
                        bfc_ref, bfr_ref,
                        pa_ref, pbt_ref, pc_ref, q_ref, k_ref, v_ref, lf_ref, kt_ref, vt_ref, lft_ref, qt_ref):
    u = _rms(h_ref[...], g_ref[...]).astype(BF16)
    pa_ref[...] = _dot_nt(u, wa_ref[...])
    pbt_ref[...] = _dot_nt(wb_ref[...], u)
    pc_ref[...] = _dot_nt(u, wc_ref[...])
    q_ref[...] = _dot_nt(u, wq_ref[...])
    qt_ref[...] = _dot_nt(wq_ref[...], u)
    k_ref[...] = _dot_nt(u, wk_ref[...])
    v_ref[...] = _dot_nt(u, wv_ref[...])
    lf_ref[...] = _log_sigmoid(_dot_nt(u, wl_ref[...]) + bfr_ref[...])
    kt_ref[...] = _dot_nt(wk_ref[...], u)
    vt_ref[...] = _dot_nt(wv_ref[...], u)
    lft_ref[...] = _log_sigmoid(_dot_nt(wl_ref[0:SUBLANES, :], u) + bfc_ref[:, 0:1])


def _inproj_weight_specs(d):
    return [_full_spec((n, d)) for n in (N_PA, N_PB, N_PC, VD, VD, VD, LANES)]


def _inproj_prompt_call(layer, depth, h, g, ws, bfc, bp, seq, tm, kv_prev):
    m, d = h.shape
    per = seq // tm
    tspec = lambda n: pl.BlockSpec((1, n, tm), lambda i: (i // per, 0, i % per))
    lspec = pl.BlockSpec((1, 1, VD, tm), lambda i: (layer, i // per, 0, i % per))
    hx = N_HEADS * EXT
    alias_args = tuple(kv_prev)
    n_in = 11
    return pl.pallas_call(
        functools.partial(_inproj_prompt_body, per=per, n_alias=len(alias_args)),
        grid=(m // tm,),
        in_specs=[_row_spec(tm, d), _full_spec((1, d))]
                 + [_full_spec((n, d)) for n in (N_PA, N_PB, N_PC, VD, hx, VD, VD, LANES)]
                 + [_full_spec((SUBLANES, LANES))]
                 + [pl.BlockSpec(memory_space=pl.ANY)] * len(alias_args),
        input_output_aliases={n_in + a: 3 + a for a in range(len(alias_args))},
        out_specs=[_row_spec(tm, N_PA), _row_spec(tm, N_PB), _row_spec(tm, N_PC),
                   lspec, lspec, tspec(SUBLANES),
                   pl.BlockSpec((1, N_HEADS, EXT, tm), lambda i: (i // per, 0, 0, i % per)),
                   pl.BlockSpec((1, N_HEADS, tm, EXT), lambda i: (i // per, 0, i % per, 0)),
                   tspec(VD)],
        out_shape=[jax.ShapeDtypeStruct((m, N_PA), F32), jax.ShapeDtypeStruct((m, N_PB), F32),
                   jax.ShapeDtypeStruct((m, N_PC), F32),
                   jax.ShapeDtypeStruct((depth, bp, VD, seq), F32),
                   jax.ShapeDtypeStruct((depth, bp, VD, seq), F32),
                   jax.ShapeDtypeStruct((bp, SUBLANES, seq), F32),
                   jax.ShapeDtypeStruct((bp, N_HEADS, EXT, seq), BF16),
                   jax.ShapeDtypeStruct((bp, N_HEADS, seq, EXT), BF16),
                   jax.ShapeDtypeStruct((bp, VD, seq), BF16)],
        scratch_shapes=[pltpu.VMEM((SUBLANES, LANES), F32)],
        compiler_params=_cparams("arbitrary"),
        name="inproj_prompt",
    )(h, g, *ws, bfc, *alias_args)


def _inproj_sample_call(h, g, ws, bfc, bfr):
    n, d = h.shape
    shapes = [(n, N_PA), (N_PB, n), (n, N_PC), (n, VD), (n, VD), (n, VD), (n, LANES),
              (VD, n), (VD, n), (SUBLANES, n), (VD, n)]
    return pl.pallas_call(
        _inproj_sample_body,
        grid=(1,),
        in_specs=[_full_spec((n, d)), _full_spec((1, d))] + _inproj_weight_specs(d)
                 + [_full_spec((SUBLANES, LANES)), _full_spec((1, LANES))],
        out_specs=[pl.BlockSpec(s, lambda i: (0, 0)) for s in shapes],
        out_shape=[jax.ShapeDtypeStruct(s, F32) for s in shapes],
        compiler_params=_cparams("arbitrary"),
        name="inproj_sample",
    )(h, g, *ws, bfc, bfr)


A_PAD = 32
C_PAD = 8
CONV_ROWS = 256


def _conv_prompt_body(pa_ref, pc_ref, aw_ref, ab_ref, lg_ref, lb_ref, cw_ref,
                      ya_ref, yc_ref, bufa_ref, bufc_ref, apad_ref, zpad_ref):
    seq = pa_ref.shape[1]
    c = GROUP_W
    apad_ref[0:A_PAD, :] = jnp.zeros((A_PAD, c), F32)
    apad_ref[A_PAD:, :] = pa_ref[0, :, 0:c] * _sigmoid(pa_ref[0, :, c:])
    zpad_ref[0:C_PAD, :] = jnp.zeros((C_PAD, c), F32)
    zpad_ref[C_PAD:, :] = pc_ref[0, :, c:2 * c] * pc_ref[0, :, 2 * c:]

    def step(i, carry):
        r0 = pl.multiple_of(i * CONV_ROWS, CONV_ROWS)
        win = apad_ref[pl.ds(r0, CONV_ROWS + A_PAD), :]
        acc = jnp.zeros((CONV_ROWS, c), F32) + ab_ref[...]
        for r in range(SUBLANES):
            nrow = CONV_ROWS if r == 0 else CONV_ROWS + SUBLANES
            u = None
            for a8 in range(0, A_PAD + 1, SUBLANES):
                j = a8 + r - (A_PAD - (CONV_A_W - 1))
                if 0 <= j < CONV_A_W:
                    term = aw_ref[j:j + 1, :] * win[a8:a8 + nrow, :]
                    u = term if u is None else u + term
            acc = acc + u[r:r + CONV_ROWS, :]
        mu = jnp.mean(acc, axis=-1, keepdims=True)
        var = jnp.mean(jnp.square(acc - mu), axis=-1, keepdims=True)
        yn = (acc - mu) * lax.rsqrt(var + EPS) * lg_ref[...] + lb_ref[...]
        ya_ref[0, pl.ds(r0, CONV_ROWS), :] = _silu(yn)
        zwin = zpad_ref[pl.ds(r0, CONV_ROWS + C_PAD), :]
        accc = jnp.zeros((CONV_ROWS, c), F32)
        for j in range(CONV_C_W):
            off = C_PAD - (CONV_C_W - 1) + j
            accc = accc + cw_ref[j:j + 1, :] * zwin[off:off + CONV_ROWS, :]
        yc_ref[0, pl.ds(r0, CONV_ROWS), :] = pc_ref[0, pl.ds(r0, CONV_ROWS), 0:c] * accc
        return carry

    lax.fori_loop(0, seq // CONV_ROWS, step, 0)
    na = CONV_A_W - 1
    nc = CONV_C_W - 1
    bufa_ref[0] = apad_ref[seq:A_PAD + seq, :][A_PAD - na:, :]
    bufc_ref[0] = zpad_ref[seq:C_PAD + seq, :][C_PAD - nc:, :]


def _conv_prompt_call(pa, pc, aw, ab, lg, lb, cw):
    b, seq, _ = pa.shape
    c = GROUP_W
    bspec = lambda n: pl.BlockSpec((1, seq, n), lambda i: (i, 0, 0))
    return pl.pallas_call(
        _conv_prompt_body,
        grid=(b,),
        in_specs=[bspec(N_PA), bspec(N_PC), _full_spec((CONV_A_W, c)), _full_spec((1, c)),
                  _full_spec((1, c)), _full_spec((1, c)), _full_spec((CONV_C_W, c))],
        out_specs=[bspec(c), bspec(c),
                   pl.BlockSpec((1, CONV_A_W - 1, c), lambda i: (i, 0, 0)),
                   pl.BlockSpec((1, CONV_C_W - 1, c), lambda i: (i, 0, 0))],
        out_shape=[jax.ShapeDtypeStruct((b, seq, c), F32), jax.ShapeDtypeStruct((b, seq, c), F32),
                   jax.ShapeDtypeStruct((b, CONV_A_W - 1, c), F32),
                   jax.ShapeDtypeStruct((b, CONV_C_W - 1, c), F32)],
        scratch_shapes=[pltpu.VMEM((A_PAD + seq, c), F32), pltpu.VMEM((C_PAD + seq, c), F32)],
        compiler_params=_cparams("arbitrary"),
        name="conv_prompt",
    )(pa, pc, aw, ab, lg, lb, cw)


def _gla_gate(alr, wa2_ref, ba_ref):
    return _log_sigmoid(_dot(alr.astype(BF16), wa2_ref[...]) + ba_ref[...]) * (1.0 / GLA_TAU)


def _gla_out(o, g, ng, hm):
    ms = _dot_sel(o * o, hm) * (1.0 / HEAD_DIM)
    return o * lax.rsqrt(ms + EPS) * ng * _silu(g)


GLA_SLAB = 256


def _group_row(x, g, r):
    n, c = x.shape
    x3 = x.reshape(n // g, g, c)
    return jnp.broadcast_to(x3[:, r:r + 1, :], (n // g, g, c)).reshape(n, c)


def _gla_prompt_body(pb_ref, wa2_ref, ba_ref, ng_ref, y_ref, st_ref, la_ref, s_ref):
    seq = pb_ref.shape[1]
    ck = GLA_CHUNK
    la_ref[...] = _gla_gate(pb_ref[0, :, 2 * KD + 2 * VD:], wa2_ref, ba_ref)
    s_ref[...] = jnp.zeros((KD, VD), F32)

    rs = min(GLA_SLAB, seq)
    nch = rs // ck
    tt = _iota((rs, rs), 0)
    ss = _iota((rs, rs), 1)
    tri = ((ss <= tt) & (_idiv(ss, ck) == _idiv(tt, ck))).astype(BF16)
    blk = _idiv(_iota((KD, VD), 0), DK_B) == _idiv(_iota((KD, VD), 1), HEAD_DIM)
    blk_bf = blk.astype(BF16)
    blk_f = blk.astype(F32)
    hm = _head_mean_matrix()
    scale = DK_B ** -0.5
    levels = [g for g in (2 * SUBLANES, 4 * SUBLANES, 8 * SUBLANES) if g <= ck]
    assert ck == 8 * SUBLANES
    key_head = _idiv(_iota((1, KD), 1), DK_B)
    val_head = _idiv(_iota((1, VD), 1), HEAD_DIM)
    tloc = _iota((rs, KD), 0)

    def slab(si, carry):
        r0 = pl.multiple_of(si * rs, rs)
        rows = pl.ds(r0, rs)
        q = pb_ref[0, rows, 0:KD] * scale
        k = pb_ref[0, rows, KD:2 * KD]
        v = pb_ref[0, rows, 2 * KD:2 * KD + VD]
        g = pb_ref[0, rows, 2 * KD + VD:2 * KD + 2 * VD]
        b = _sel_dot(tri, la_ref[rows, :])
        vb3 = v.astype(BF16).reshape(nch, ck, VD)

        t8 = _imod(tloc, SUBLANES)
        acc = jnp.zeros((rs, VD), F32)
        for s in range(SUBLANES):
            diff = jnp.where(t8 >= s, b - _group_row(b, SUBLANES, s), -jnp.inf)
            e = (q * _group_row(k, SUBLANES, s) * jnp.exp(diff)).astype(BF16)
            acc = acc + _dot(e, blk_bf) * _group_row(v, SUBLANES, s)

        sc = jnp.zeros((nch, N_HEADS * ck, ck), F32)
        for gsz in levels:
            r = _group_row(b, gsz, gsz // 2)
            upper = _imod(tloc, gsz) >= gsz // 2
            qf = q * jnp.exp(jnp.where(upper, b - r, -jnp.inf))
            kf = (k * jnp.exp(jnp.where(upper, -jnp.inf, r - b))).astype(BF16).reshape(nch, ck, KD)
            qstack = jnp.concatenate(
                [jnp.where(key_head == h, qf, 0.0).astype(BF16).reshape(nch, ck, KD)
                 for h in range(N_HEADS)], axis=1)
            sc_g = jnp.einsum('cqk,csk->cqs', qstack, kf, preferred_element_type=F32)
            same_group = _idiv(_imod(_iota((N_HEADS * ck, ck), 0), ck), gsz) == _idiv(_iota((N_HEADS * ck, ck), 1), gsz)
            sc = sc + jnp.where(same_group[None], sc_g, 0.0)
        res = jnp.einsum('cqs,csv->cqv', sc.astype(BF16), vb3, preferred_element_type=F32)
        for h in range(N_HEADS):
            part = res[:, h * ck:(h + 1) * ck, :].reshape(rs, VD)
            acc = acc + jnp.where(val_head == h, part, 0.0)

        b_last = _group_row(b, ck, ck - 1)
        kdec = k * jnp.exp(b_last - b)
        lastb = jnp.concatenate([b[(c + 1) * ck - 1:(c + 1) * ck, :] for c in range(nch)]
                                + [jnp.zeros((LANES - nch, KD), F32)], axis=0)
        dcols = jnp.exp(lastb.T)
        s_cur = s_ref[...]
        s_before = []
        for c in range(nch):
            s_before.append(s_cur.astype(BF16))
            crow = slice(c * ck, (c + 1) * ck)
            upd = _dot(kdec[crow, :].T.astype(BF16), vb3[c])
            s_cur = dcols[:, c:c + 1] * s_cur + upd * blk_f
        s_ref[...] = s_cur
        qb3 = (q * jnp.exp(b)).astype(BF16).reshape(nch, ck, KD)
        inter = jnp.einsum('cqk,ckv->cqv', qb3, jnp.stack(s_before), preferred_element_type=F32)
        o = inter.reshape(rs, VD) + acc
        y_ref[0, rows, :] = _gla_out(o, g, ng_ref[...], hm)
        return carry

    lax.fori_loop(0, seq // rs, slab, 0)
    for h in range(N_HEADS):
        st_ref[0, h] = s_ref[h * DK_B:(h + 1) * DK_B, h * HEAD_DIM:(h + 1) * HEAD_DIM]


def _gla_prompt_call(pb, wa2, ba, ng):
    b, seq, _ = pb.shape
    return pl.pallas_call(
        _gla_prompt_body,
        grid=(b,),
        in_specs=[pl.BlockSpec((1, seq, N_PB), lambda i: (i, 0, 0)), _full_spec((LANES, KD)),
                  _full_spec((1, KD)), _full_spec((1, VD))],
        out_specs=[pl.BlockSpec((1, seq, VD), lambda i: (i, 0, 0)),
                   pl.BlockSpec((1, N_HEADS, DK_B, HEAD_DIM), lambda i: (i, 0, 0, 0))],
        out_shape=[jax.ShapeDtypeStruct((b, seq, VD), F32),
                   jax.ShapeDtypeStruct((b, N_HEADS, DK_B, HEAD_DIM), F32)],
        scratch_shapes=[pltpu.VMEM((seq, KD), F32), pltpu.VMEM((KD, VD), F32)],
        compiler_params=_cparams("arbitrary"),
        name="gla_prompt",
    )(pb, wa2, ba, ng)


FOX_GROUP = 4


def _fox_prompt_body(qe_ref, ke_ref, vtb_ref, o_ref, *, tq):
    qi = pl.program_id(1)
    key_pos = _iota((tq, tq), 0)
    qry_pos = _iota((tq, tq), 1)

    def tile(ki):
        return pl.ds(pl.multiple_of(ki * tq, tq), tq)

    def consume(tiles, stats):
        scores = [[_dot(ke_ref[0, h, tile(ki), :], qe_ref[0, h]) for h in range(N_HEADS)]
                  for ki, _ in tiles]
        for (ki, masked), sc in zip(tiles, scores):
            soft = []
            for h in range(N_HEADS):
                m, l, _ = stats[h]
                s = sc[h]
                if masked:
                    s = jnp.where(key_pos <= qry_pos, s, NEG_INF)
                m_new = jnp.maximum(m, jnp.max(s, axis=0, keepdims=True))
                alpha = jnp.exp(m - m_new)
                p = jnp.exp(s - m_new)
                soft.append((m_new, alpha * l + jnp.sum(p, axis=0, keepdims=True), alpha, p.astype(BF16)))
            out = []
            for h in range(N_HEADS):
                m_new, l, alpha, p = soft[h]
                vh = vtb_ref[0, h * HEAD_DIM:(h + 1) * HEAD_DIM, tile(ki)]
                out.append((m_new, l, alpha * stats[h][2] + _dot(vh, p)))
            stats = tuple(out)
        return stats

    init = tuple((jnp.full((1, tq), NEG_INF, F32), jnp.zeros((1, tq), F32),
                  jnp.zeros((HEAD_DIM, tq), F32)) for _ in range(N_HEADS))
    g = FOX_GROUP
    stats = lax.fori_loop(
        0, qi // g, lambda k, st: consume([(g * k + t, False) for t in range(g)], st), init)
    left = qi % g
    tails = [functools.partial(consume, [(qi - r + t, False) for t in range(r)] + [(qi, True)])
             for r in range(g)]
    stats = lax.switch(left, tails, stats)
    o_ref[0] = jnp.concatenate([acc / l for _, l, acc in stats], axis=0).T


def _fox_prompt_call(qe, ke, vtb, tq):
    b, _, _, seq = qe.shape
    return pl.pallas_call(
        functools.partial(_fox_prompt_body, tq=tq),
        grid=(b, seq // tq),
        in_specs=[pl.BlockSpec((1, N_HEADS, EXT, tq), lambda i, j: (i, 0, 0, j)),
                  pl.BlockSpec((1, N_HEADS, seq, EXT), lambda i, j: (i, 0, 0, 0)),
                  pl.BlockSpec((1, VD, seq), lambda i, j: (i, 0, 0))],
        out_specs=pl.BlockSpec((1, tq, VD), lambda i, j: (i, j, 0)),
        out_shape=jax.ShapeDtypeStruct((b, seq, VD), F32),
        compiler_params=_cparams("arbitrary", "arbitrary"),
        name="fox_prompt",
    )(qe, ke, vtb)


def _mix_sample_body(pa_ref, pbt_ref, pc_ref, bufa_ref, sg_ref, bufc_ref,
                     aw_ref, ab_ref, lg_ref, lb_ref, wa2t_ref, bac_ref, ngc_ref, cw_ref,
                     ya_ref, yb_ref, yc_ref, bufa_o, sg_o, bufc_o, q_s, k_s, dec_s):
    c = GROUP_W
    a = pa_ref[:, 0:c] * _sigmoid(pa_ref[:, c:])
    na = CONV_A_W - 1
    acc = aw_ref[na:na + 1, :] * a + ab_ref[...]
    for j in range(na):
        acc = acc + aw_ref[j:j + 1, :] * bufa_ref[0, j]
    mu = jnp.mean(acc, axis=-1, keepdims=True)
    var = jnp.mean(jnp.square(acc - mu), axis=-1, keepdims=True)
    ya_ref[...] = _silu((acc - mu) * lax.rsqrt(var + EPS) * lg_ref[...] + lb_ref[...])
    for j in range(na - 1):
        bufa_o[j] = bufa_ref[0, j + 1]
    bufa_o[na - 1] = a
    z = pc_ref[:, c:2 * c] * pc_ref[:, 2 * c:]
    conv = (cw_ref[0:1, :] * bufc_ref[0, :, 0:c] + cw_ref[1:2, :] * bufc_ref[0, :, c:]
            + cw_ref[2:3, :] * z)
    yc_ref[...] = pc_ref[:, 0:c] * conv
    bufc_o[:, 0:c] = bufc_ref[0, :, c:]
    bufc_o[:, c:] = z
    q_s[...] = pbt_ref[0:KD, :] * (DK_B ** -0.5)
    k_s[...] = pbt_ref[KD:2 * KD, :]
    gate = _dot(wa2t_ref[...], pbt_ref[2 * KD + 2 * VD:, :].astype(BF16)) + bac_ref[...]
    dec_s[...] = jnp.exp(_log_sigmoid(gate) * (1.0 / GLA_TAU))
    n = pa_ref.shape[0]
    ys = []
    for h in range(N_HEADS):
        vrows = slice(2 * KD + h * HEAD_DIM, 2 * KD + (h + 1) * HEAD_DIM)
        grows = slice(2 * KD + VD + h * HEAD_DIM, 2 * KD + VD + (h + 1) * HEAD_DIM)
        vh = pbt_ref[vrows, :]

        def key_step(kk, o, h=h, vh=vh):
            hk = h * DK_B + kk
            one = pl.ds(hk, 1)
            s_new = dec_s[one, :] * sg_ref[0, hk] + k_s[one, :] * vh
            sg_o[hk] = s_new
            return o + q_s[one, :] * s_new

        o = lax.fori_loop(0, DK_B, key_step, jnp.zeros((HEAD_DIM, n), F32))
        ms = jnp.mean(o * o, axis=0, keepdims=True)
        ng = ngc_ref[h * HEAD_DIM:(h + 1) * HEAD_DIM, :]
        ys.append(o * lax.rsqrt(ms + EPS) * ng * _silu(pbt_ref[grows, :]))
    yb_ref[...] = jnp.concatenate(ys, axis=0).T


def _layer_spec(shape, l):
    return pl.BlockSpec((1,) + tuple(shape[1:]), lambda i: (l,) + (0,) * (len(shape) - 1),
                        pipeline_mode=pl.Buffered(1))


def _mix_sample_call(l, pa, pbt, pc, bufa, sg, bufc, aw, ab, lg, lb, wa2t, bac, ngc, cw):
    n = pa.shape[0]
    c = GROUP_W
    small = (aw, ab, lg, lb, wa2t, bac, ngc, cw)
    out_shape = [jax.ShapeDtypeStruct((n, c), F32)] * 3 + [
        jax.ShapeDtypeStruct(bufa.shape[1:], F32), jax.ShapeDtypeStruct(sg.shape[1:], F32),
        jax.ShapeDtypeStruct(bufc.shape[1:], F32)]
    return pl.pallas_call(
        _mix_sample_body,
        grid=(1,),
        in_specs=[_full_spec(pa.shape), _full_spec(pbt.shape), _full_spec(pc.shape),
                  _layer_spec(bufa.shape, l), _layer_spec(sg.shape, l), _layer_spec(bufc.shape, l)]
                 + [_full_spec(x.shape) for x in small],
        out_specs=[pl.BlockSpec(s.shape, lambda i, nd=len(s.shape): (0,) * nd) for s in out_shape],
        out_shape=out_shape,
        scratch_shapes=[pltpu.VMEM((KD, n), F32)] * 3,
        compiler_params=_cparams("arbitrary"),
        name="mix_sample",
    )(pa, pbt, pc, bufa, sg, bufc, *small)


def _logf_pages_body(x_ref, o_ref):
    n = x_ref.shape[0]
    x = x_ref[...]
    later = (_iota((PAGE_SIZE, PAGE_SIZE), 0) > _iota((PAGE_SIZE, PAGE_SIZE), 1)).astype(BF16)
    ones = jnp.ones((PAGE_SIZE, PAGE_SIZE), BF16)
    within = _dot_sel(x, later)
    total = pltpu.roll(_dot_sel(x, ones), N_HEADS, 0)
    o_ref[...] = jnp.where(_imod(_iota((n, PAGE_SIZE), 0), SUBLANES) < N_HEADS, within, total)


def _logf_pages_call(x, tm):
    m = x.shape[0]
    return pl.pallas_call(
        _logf_pages_body,
        grid=(m // tm,),
        in_specs=[_row_spec(tm, PAGE_SIZE)],
        out_specs=_row_spec(tm, PAGE_SIZE),
        out_shape=jax.ShapeDtypeStruct((m, PAGE_SIZE), F32),
        compiler_params=_cparams("arbitrary"),
        name="logf_pages",
    )(x)


def _decode_sample(b, slot, q_ref, qt_ref, kn_ref, vn_ref, lfn_ref, kbuf, vbuf, lfbuf, n_pages):
    past = n_pages * PAGE_SIZE
    scale = HEAD_DIM ** -0.5
    heads = [slice(h * HEAD_DIM, (h + 1) * HEAD_DIM) for h in range(N_HEADS)]
    hm = (_idiv(_iota((SUBLANES, VD), 1), HEAD_DIM) == _iota((SUBLANES, VD), 0)).astype(F32)
    qbd = jnp.broadcast_to(q_ref[pl.ds(b, 1), :], (SUBLANES, VD)) * hm
    onehot = (_iota((1, qt_ref.shape[1]), 1) == b).astype(F32)
    qcol = jnp.sum(qt_ref[...] * onehot, axis=1, keepdims=True) * scale
    qcb = jnp.broadcast_to(qcol, (VD, PAGE_SIZE))
    parts = [[] for _ in heads]
    for p in range(n_pages):
        prod = kbuf[slot, p] * qcb
        for h, rows in enumerate(heads):
            parts[h].append(prod[rows, :].reshape(HEAD_DIM // SUBLANES, SUBLANES, PAGE_SIZE).sum(axis=0))
    rowid = _iota((SUBLANES, past), 0)
    s = jnp.zeros((SUBLANES, past), F32)
    for h in range(N_HEADS):
        t = jnp.concatenate(parts[h], axis=1)
        for sh in (4, 2, 1):
            t = t + pltpu.roll(t, sh, 0)
        s = jnp.where(rowid == h, t, s)
    lf = lfbuf[slot]
    incl = pltpu.roll(lf, N_HEADS, 0)
    sh = PAGE_SIZE
    while sh < past:
        incl = incl + jnp.concatenate([incl[:, sh:], jnp.zeros((SUBLANES, sh), F32)], axis=1)
        sh *= 2
    later_pages = jnp.concatenate([incl[:, PAGE_SIZE:], jnp.zeros((SUBLANES, PAGE_SIZE), F32)], axis=1)
    valid = _iota((SUBLANES, past), 0) < N_HEADS
    logits = jnp.where(valid, s + lf + later_pages, 0.0)
    s_self = jnp.sum(qbd * kn_ref[pl.ds(b, 1), :], axis=1, keepdims=True) * scale
    pick = (_iota((SUBLANES, LANES), 1) == _iota((SUBLANES, LANES), 0)).astype(F32)
    c_new = jnp.sum(pick * lfn_ref[pl.ds(b, 1), :], axis=1, keepdims=True)
    self_logit = jnp.where(_iota((SUBLANES, 1), 0) < N_HEADS, s_self - c_new, 0.0)
    m = jnp.maximum(jnp.max(logits, axis=1, keepdims=True), self_logit)
    p = jnp.exp(logits - m)
    p_self = jnp.exp(self_logit - m)
    l = jnp.sum(p, axis=1, keepdims=True) + p_self
    acc = [jnp.zeros((HEAD_DIM, PAGE_SIZE), F32) for _ in heads]
    for pg in range(n_pages):
        vt = vbuf[slot, pg]
        cols = slice(pg * PAGE_SIZE, (pg + 1) * PAGE_SIZE)
        for h, rows in enumerate(heads):
            acc[h] = acc[h] + vt[rows, :] * jnp.broadcast_to(p[h:h + 1, cols], (HEAD_DIM, PAGE_SIZE))
    ocol = jnp.concatenate([jnp.sum(a, axis=1, keepdims=True) for a in acc], axis=0)
    eye = (_iota((VD, VD), 0) == _iota((VD, VD), 1)).astype(F32)
    orow = jnp.sum(eye * ocol, axis=0, keepdims=True)
    p_self_row = jnp.sum(p_self * hm, axis=0, keepdims=True)
    l_row = jnp.sum(l * hm, axis=0, keepdims=True)
    return (orow + p_self_row * vn_ref[pl.ds(b, 1), :]) / l_row


def _decode_gather(pt_ref, kc_hbm, vc_hbm, lfc_hbm, kbuf, vbuf, lfbuf, sem, *, layer, n_pages, per_step, base):
    i = pl.program_id(0)
    n_steps = pl.num_programs(0)

    def logf_copy(page, slot, p):
        cols = pl.ds(p * PAGE_SIZE, PAGE_SIZE)
        return pltpu.make_async_copy(lfc_hbm.at[layer, page], lfbuf.at[slot, :, cols], sem.at[slot, 2])

    def copies(sample, slot):
        out = []
        for p in range(n_pages):
            page = pt_ref[sample, p]
            out.append(pltpu.make_async_copy(kc_hbm.at[layer, page], kbuf.at[slot, p], sem.at[slot, 0]))
            out.append(pltpu.make_async_copy(vc_hbm.at[layer, page], vbuf.at[slot, p], sem.at[slot, 1]))
            out.append(logf_copy(page, slot, p))
        return out

    cur = (i % 2) * per_step
    nxt = per_step - cur
    first = base + i * per_step

    @pl.when(i == 0)
    def _():
        for j in range(per_step):
            for cp in copies(base + j, j):
                cp.start()

    for j in range(per_step):
        slot = cur + j
        pltpu.make_async_copy(kc_hbm.at[layer, pl.ds(0, n_pages)], kbuf.at[slot], sem.at[slot, 0]).wait()
        pltpu.make_async_copy(vc_hbm.at[layer, pl.ds(0, n_pages)], vbuf.at[slot], sem.at[slot, 1]).wait()
        for p in range(n_pages):
            logf_copy(pt_ref[first + j, p], slot, p).wait()

    @pl.when(i + 1 < n_steps)
    def _():
        for j in range(per_step):
            for cp in copies(first + per_step + j, nxt + j):
                cp.start()

    return [(first + j, cur + j) for j in range(per_step)]


def _ffn_decode_body(*refs, fc, n_pages, layer, per_step, base, mix):
    pt_ref, refs = refs[0], refs[1:]
    if mix:
        h_ref, ya_ref, yb_ref, yc_ref, ydp_ref, wo_ref, mpost_ref = refs[:7]
        refs = refs[7:]
    else:
        x_ref, refs = refs[0], refs[1:]
    (pre_ref, post_ref, wg_ref, wu_ref, wd_ref, q_ref, qt_ref, kn_ref, vn_ref, lfn_ref, kc_hbm, vc_hbm, lfc_hbm,
     o_ref, yd_ref, acc_ref, kbuf, vbuf, lfbuf, sem) = refs
    pairs = _decode_gather(pt_ref, kc_hbm, vc_hbm, lfc_hbm, kbuf, vbuf, lfbuf, sem,
                           layer=layer, n_pages=n_pages, per_step=per_step, base=base)
    if mix:
        ycat = jnp.concatenate([ya_ref[...], yb_ref[...], yc_ref[...], ydp_ref[...]], axis=1)
        x = h_ref[...] + _rms(_dot(ycat.astype(BF16), wo_ref[0]), mpost_ref[...])
    else:
        x = x_ref[...]
    for j, (sample, slot) in enumerate(pairs):
        yd_ref[j] = _decode_sample(sample, slot, q_ref, qt_ref, kn_ref, vn_ref, lfn_ref,
                                   kbuf, vbuf, lfbuf, n_pages)
    o_ref[...] = _swiglu_residual(x, pre_ref[...], post_ref[...],
                                  wg_ref.at[0], wu_ref.at[0], wd_ref.at[0], acc_ref, fc)


def _ffn_decode_call(layer, x, mix_in, pre_g, post_g, wg, wu, wd, page_table, q, qt, kn, vn, lfn, kc, vc, lfc,
                     base, n_dec, tm, fc=256):
    m, d = x.shape
    f = wg.shape[2]
    n, n_pages = page_table.shape
    steps = m // tm
    assert n_dec % steps == 0 and base + n_dec <= n
    per_step = n_dec // steps
    past = n_pages * PAGE_SIZE
    const = lambda shape: pl.BlockSpec(shape, lambda i, pt: (0,) * len(shape), pipeline_mode=pl.Buffered(1))
    lyr = lambda shape: pl.BlockSpec((1,) + shape, lambda i, pt: (layer, 0, 0), pipeline_mode=pl.Buffered(1))
    rows = lambda w: pl.BlockSpec((tm, w), lambda i, pt: (i, 0))
    anyspace = pl.BlockSpec(memory_space=pl.ANY)
    mix = mix_in is not None
    lead_specs, lead_args = [rows(d)], [x]
    if mix:
        ya, yb, yc, yd, wo, mpost = mix_in
        lead_specs += [rows(GROUP_W)] * 4 + [lyr((d, d)), const((1, d))]
        lead_args += [ya, yb, yc, yd, wo, mpost]
    grid_spec = pltpu.PrefetchScalarGridSpec(
        num_scalar_prefetch=1,
        grid=(steps,),
        in_specs=lead_specs + [const((1, d)), const((1, d)), lyr((d, f)), lyr((d, f)), lyr((f, d)),
                               const((n, VD)), const((VD, n)), const((n, VD)), const((n, VD)), const((n, LANES)),
                               anyspace, anyspace, anyspace],
        out_specs=[rows(d), pl.BlockSpec((per_step, 1, VD), lambda i, pt: (i, 0, 0))],
        scratch_shapes=[pltpu.VMEM((tm, d), F32),
                        pltpu.VMEM((2 * per_step, n_pages, VD, PAGE_SIZE), F32),
                        pltpu.VMEM((2 * per_step, n_pages, VD, PAGE_SIZE), F32),
                        pltpu.VMEM((2 * per_step, SUBLANES, past), F32),
                        pltpu.SemaphoreType.DMA((2 * per_step, 3))],
    )
    return pl.pallas_call(
        functools.partial(_ffn_decode_body, fc=fc, n_pages=n_pages, layer=layer, per_step=per_step,
                          base=base, mix=mix),
        grid_spec=grid_spec,
        out_shape=[jax.ShapeDtypeStruct((m, d), F32), jax.ShapeDtypeStruct((n_dec, 1, VD), F32)],
        compiler_params=_cparams("arbitrary"),
        name="mix_ffn_decode" if mix else "ffn_decode",
    )(page_table, *lead_args, pre_g, post_g, wg, wu, wd, q, qt, kn, vn, lfn, kc, vc, lfc)


ATTN_TILE = 256


def _row_tile(m, cap=512):
    tm = cap
    while tm >= SUBLANES:
        if m % tm == 0:
            return tm
        tm //= 2
    raise ValueError(f"row count {m} is not a multiple of {SUBLANES}")


def kernel(x_prompt, x_sample, state_conv_a, state_gla, state_conv_c, cache_k, cache_v, cache_logf, page_table, ffn1_pre_g, ffn1_post_g, ffn1_w_gate, ffn1_w_up, ffn1_w_down, mix_pre_g, mix_post_g, w_in, w_out, a_conv_w, a_conv_b, a_ln_g, a_ln_b, b_gate_w2, b_gate_b, b_out_norm_g, c_conv_w, d_forget_b, ffn2_pre_g, ffn2_post_g, ffn2_w_gate, ffn2_w_up, ffn2_w_down):
    depth = w_in.shape[0]
    bp, seq, d = x_prompt.shape
    bd = x_sample.shape[0]
    n_pool = cache_k.shape[1]
    assert x_sample.shape[1] == 1 and d == D_MODEL
    assert seq % ATTN_TILE == 0 and seq % GLA_CHUNK == 0 and seq >= CONV_A_W - 1

    hp = x_prompt.reshape(bp * seq, d)
    hs = x_sample.reshape(bd, d)
    tmp = _row_tile(bp * seq)
    tms = _row_tile(bd)
    tq = ATTN_TILE

    off_b = N_PA
    off_c = off_b + 2 * KD + 2 * VD + GLA_RANK
    off_d = off_c + N_PC
    row = lambda v: v.reshape(1, -1)

    lanes_of = lambda v, n=LANES: jnp.broadcast_to(v[:, None], (v.shape[0], n))

    ck = cache_k.transpose(0, 1, 3, 4, 2).reshape(depth, n_pool, VD, PAGE_SIZE)
    cv = cache_v.transpose(0, 1, 3, 4, 2).reshape(depth, n_pool, VD, PAGE_SIZE)
    sca = state_conv_a.transpose(0, 2, 1, 3)
    sgl = state_gla.transpose(0, 2, 3, 4, 1).reshape(depth, KD, HEAD_DIM, bd)
    scc = state_conv_c.reshape(depth, bd, (CONV_C_W - 1) * GROUP_W)
    w_in_t = w_in.transpose(0, 2, 1)

    lf_t = jnp.pad(cache_logf.transpose(0, 1, 3, 2), ((0, 0), (0, 0), (0, SUBLANES - N_HEADS), (0, 0)))
    lf_rows = depth * n_pool * SUBLANES
    lf_pages = _logf_pages_call(lf_t.reshape(lf_rows, PAGE_SIZE), _row_tile(lf_rows, cap=4096))
    lf_pages = lf_pages.reshape(depth, n_pool, SUBLANES, PAGE_SIZE)

    ffn1_ws = tuple(w.astype(BF16) for w in (ffn1_w_gate, ffn1_w_up, ffn1_w_down))
    ffn2_ws = tuple(w.astype(BF16) for w in (ffn2_w_gate, ffn2_w_up, ffn2_w_down))
    wo = w_out.astype(BF16)

    outs = [[] for _ in range(12)]
    kv_prev = (jnp.zeros((depth, bp, VD, seq), F32), jnp.zeros((depth, bp, VD, seq), F32))
    for l in range(depth):
        wt = w_in_t[l].astype(BF16)
        pad_rows = lambda w, n: jnp.pad(w, ((0, n - w.shape[0]), (0, 0)))
        wq, wk, wv = (wt[off_d + i * VD:off_d + (i + 1) * VD] for i in range(3))
        in_w = (wt[0:off_b], pad_rows(wt[off_b:off_c], N_PB), wt[off_c:off_d],
                wq, wk, wv, pad_rows(wt[off_d + 3 * VD:], LANES))
        slabs = lambda w: jnp.pad(w.reshape(N_HEADS, HEAD_DIM, d),
                                  ((0, 0), (0, EXT - HEAD_DIM), (0, 0))).reshape(N_HEADS * EXT, d)
        in_w_prompt = in_w[0:3] + (wq, slabs(wk), wk, wv, in_w[6])
        bf_pad = jnp.pad(d_forget_b[l], (0, LANES - N_HEADS))
        bfr = row(bf_pad)
        bfc = lanes_of(bf_pad[0:SUBLANES])
        wa2 = jnp.pad(b_gate_w2[l], ((0, LANES - GLA_RANK), (0, 0))).astype(BF16)
        mix_w = (a_conv_w[l], row(a_conv_b[l]), row(a_ln_g[l]), row(a_ln_b[l]))
        ffn1_w = (row(ffn1_pre_g[l]), row(ffn1_post_g[l])) + ffn1_ws
        ffn2_w = (row(ffn2_pre_g[l]), row(ffn2_post_g[l])) + ffn2_ws

        hs = _ffn_call(l, hs, *ffn1_w, tm=tms)
        s_pa, s_pbt, s_pc, s_q, s_k, s_v, s_lf, s_kt, s_vt, s_lft, s_qt = _inproj_sample_call(
            hs, row(mix_pre_g[l]), in_w, bfc, bfr)
        s_ya, s_yb, s_yc, s_bufa, s_state, s_bufc = _mix_sample_call(
            l, s_pa, s_pbt, s_pc, sca, sgl, scc, *mix_w, wa2.T, lanes_of(b_gate_b[l], bd),
            lanes_of(b_out_norm_g[l], bd), c_conv_w[l])

        dec_args = (page_table, s_q, s_qt, s_k, s_v, s_lf, ck, cv, lf_pages)
        half = bd // 2
        hp, s_yd0 = _ffn_decode_call(l, hp, None, *ffn1_w, *dec_args, base=0, n_dec=half, tm=tmp)
        pa, pb, pc, kt_all, vt_all, lft, qe, ke, vtb = _inproj_prompt_call(
            l, depth, hp, row(mix_pre_g[l]), in_w_prompt, bfc, bp, seq, tm=tmp, kv_prev=kv_prev)
        kv_prev = (kt_all, vt_all)
        r3 = lambda t: t.reshape(bp, seq, t.shape[-1])
        ya, yc, buf_a, buf_c = _conv_prompt_call(r3(pa), r3(pc), *mix_w, c_conv_w[l])
        yb, s_b = _gla_prompt_call(r3(pb), wa2, row(b_gate_b[l]), row(b_out_norm_g[l]))
        yd = _fox_prompt_call(qe, ke, vtb, tq)
        f2 = lambda t: t.reshape(bp * seq, t.shape[-1])
        hp, s_yd1 = _ffn_decode_call(l, hp, (f2(ya), f2(yb), f2(yc), f2(yd), wo, row(mix_post_g[l])),
                                     *ffn2_w, *dec_args, base=half, n_dec=bd - half, tm=tmp)
        for i, t in ((0, buf_a), (1, s_b), (2, buf_c), (5, lft[:, 0:N_HEADS, :])):
            outs[i].append(t)

        s_yd = jnp.concatenate([s_yd0, s_yd1], axis=0).reshape(bd, VD)
        hs = _mix_ffn_call(l, hs, s_ya, s_yb, s_yc, s_yd, wo, row(mix_post_g[l]), *ffn2_w, tm=tms)
        for i, t in enumerate((s_bufa, s_state, s_bufc, s_kt, s_vt, s_lft[0:N_HEADS, :])):
            outs[6 + i].append(t)

    p_ca, p_gla, p_cc, _, _, p_lft, s_ca, s_gla, s_cc, s_kt, s_vt, s_lft = (
        jnp.stack(o) if o else None for o in outs)
    p_kt, p_vt = kv_prev
    heads = lambda t: t.reshape(t.shape[:-2] + (N_HEADS, HEAD_DIM, t.shape[-1]))
    return (hp.reshape(bp, seq, d), hs.reshape(bd, 1, d),
            p_ca, p_gla, p_cc,
            heads(p_kt).transpose(0, 1, 4, 2, 3), heads(p_vt).transpose(0, 1, 4, 2, 3),
            p_lft.transpose(0, 1, 3, 2),
            s_ca.transpose(0, 2, 1, 3),
            s_gla.reshape(depth, N_HEADS, DK_B, HEAD_DIM, bd).transpose(0, 4, 1, 2, 3),
            s_cc.reshape(depth, bd, CONV_C_W - 1, GROUP_W),
            heads(s_kt).transpose(0, 3, 1, 2)[:, :, None], heads(s_vt).transpose(0, 3, 1, 2)[:, :, None],
            s_lft.transpose(0, 2, 1)[:, :, None])
```

```python
import functools

import jax
import jax.numpy as jnp
from jax import lax
from jax.experimental import pallas as pl
from jax.experimental.pallas import tpu as pltpu

F32 = jnp.float32
BF16 = jnp.bfloat16

D_MODEL = 1024
GROUP_W = D_MODEL // 4
HEAD_DIM = 64
N_HEADS = GROUP_W // HEAD_DIM
DK_B = HEAD_DIM // 2
GLA_RANK = 16
GLA_TAU = 16.0
GLA_CHUNK = 64
CONV_A_W = 31
CONV_C_W = 3
PAGE_SIZE = 128
EPS = 1e-6
NEG_INF = -1e30

LANES = 128
SUBLANES = 8
VMEM_LIMIT_BYTES = 56 * 1024 * 1024

N_PA = 2 * GROUP_W
KD = N_HEADS * DK_B
VD = N_HEADS * HEAD_DIM
N_PB = 2 * KD + 2 * VD + LANES
N_PC = 3 * GROUP_W


def _cparams(*sem):
    return pltpu.CompilerParams(dimension_semantics=sem, vmem_limit_bytes=VMEM_LIMIT_BYTES)


def _dot(a, b):
    return jnp.dot(a, b, preferred_element_type=F32)


def _dot_nt(a, b):
    return lax.dot_general(a, b, (((1,), (1,)), ((), ())), preferred_element_type=F32)


def _split3(x):
    hi = x.astype(BF16)
    r = x - hi.astype(F32)
    mid = r.astype(BF16)
    lo = (r - mid.astype(F32)).astype(BF16)
    return hi, mid, lo


def _dot_sel(x, sel):
    hi, mid, lo = _split3(x)
    return _dot(hi, sel) + _dot(mid, sel) + _dot(lo, sel)


def _sel_dot(sel, x):
    hi, mid, lo = _split3(x)
    return _dot(sel, hi) + _dot(sel, mid) + _dot(sel, lo)


def _rms(x, g):
    return x * lax.rsqrt(jnp.mean(x * x, axis=-1, keepdims=True) + EPS) * g


def _sigmoid(x):
    return 1.0 / (1.0 + jnp.exp(-x))


def _silu(x):
    return x * _sigmoid(x)


def _log_sigmoid(x):
    return jnp.minimum(x, 0.0) - jnp.log1p(jnp.exp(-jnp.abs(x)))


def _iota(shape, dim):
    return lax.broadcasted_iota(jnp.int32, shape, dim)


def _idiv(x, n):
    assert n & (n - 1) == 0
    return lax.shift_right_logical(x, n.bit_length() - 1)


def _imod(x, n):
    assert n & (n - 1) == 0
    return x & (n - 1)


def _head_mean_matrix():
    r = _idiv(_iota((VD, VD), 0), HEAD_DIM)
    c = _idiv(_iota((VD, VD), 1), HEAD_DIM)
    return (r == c).astype(BF16)


def _swiglu_residual(x, pre_g, post_g, wg_ref, wu_ref, wd_ref, acc_ref, fc):
    xn = _rms(x, pre_g).astype(BF16)
    for c in range(wg_ref.shape[1] // fc):
        sl = slice(c * fc, (c + 1) * fc)
        g = _dot(xn, wg_ref[:, sl])
        u = _dot(xn, wu_ref[:, sl])
        hid = (_silu(g) * u).astype(BF16)
        part = _dot(hid, wd_ref[sl, :])
        if c == 0:
            acc_ref[...] = part
        else:
            acc_ref[...] += part
    return x + 0.5 * _rms(acc_ref[...], post_g)


def _ffn_body(x_ref, pre_ref, post_ref, wg_ref, wu_ref, wd_ref, o_ref, acc_ref, *, fc):
    o_ref[...] = _swiglu_residual(x_ref[...], pre_ref[...], post_ref[...],
                                  wg_ref.at[0], wu_ref.at[0], wd_ref.at[0], acc_ref, fc)


def _mix_ffn_body(h_ref, ya_ref, yb_ref, yc_ref, yd_ref, wo_ref, mpost_ref,
                  pre_ref, post_ref, wg_ref, wu_ref, wd_ref, o_ref, acc_ref, *, fc):
    ycat = jnp.concatenate([ya_ref[...], yb_ref[...], yc_ref[...], yd_ref[...]], axis=1)
    y = _dot(ycat.astype(BF16), wo_ref[0])
    h = h_ref[...] + _rms(y, mpost_ref[...])
    o_ref[...] = _swiglu_residual(h, pre_ref[...], post_ref[...],
                                  wg_ref.at[0], wu_ref.at[0], wd_ref.at[0], acc_ref, fc)


def _row_spec(tm, n):
    return pl.BlockSpec((tm, n), lambda i: (i, 0))


def _full_spec(shape):
    return pl.BlockSpec(shape, lambda i: (0,) * len(shape), pipeline_mode=pl.Buffered(1))


def _ffn_call(layer, x, pre_g, post_g, wg, wu, wd, tm, fc=256):
    m, d = x.shape
    return pl.pallas_call(
        functools.partial(_ffn_body, fc=fc),
        grid=(m // tm,),
        in_specs=[_row_spec(tm, d), _full_spec((1, d)), _full_spec((1, d)),
                  _layer_spec(wg.shape, layer), _layer_spec(wu.shape, layer), _layer_spec(wd.shape, layer)],
        out_specs=_row_spec(tm, d),
        out_shape=jax.ShapeDtypeStruct((m, d), F32),
        scratch_shapes=[pltpu.VMEM((tm, d), F32)],
        compiler_params=_cparams("arbitrary"),
        name="ffn",
    )(x, pre_g, post_g, wg, wu, wd)


def _mix_ffn_call(layer, h, ya, yb, yc, yd, wo, mpost, pre_g, post_g, wg, wu, wd, tm, fc=256):
    m, d = h.shape
    return pl.pallas_call(
        functools.partial(_mix_ffn_body, fc=fc),
        grid=(m // tm,),
        in_specs=[_row_spec(tm, d)] + [_row_spec(tm, GROUP_W)] * 4
                 + [_layer_spec(wo.shape, layer), _full_spec((1, d)), _full_spec((1, d)), _full_spec((1, d)),
                    _layer_spec(wg.shape, layer), _layer_spec(wu.shape, layer), _layer_spec(wd.shape, layer)],
        out_specs=_row_spec(tm, d),
        out_shape=jax.ShapeDtypeStruct((m, d), F32),
        scratch_shapes=[pltpu.VMEM((tm, d), F32)],
        compiler_params=_cparams("arbitrary"),
        name="mix_ffn",
    )(h, ya, yb, yc, yd, wo, mpost, pre_g, post_g, wg, wu, wd)


EXT = LANES
C_KEY = HEAD_DIM
C_QRY = HEAD_DIM + 3


def _inproj_prompt_body(*refs, per, n_alias):
    (h_ref, g_ref, wa_ref, wb_ref, wc_ref, wq_ref, wke_ref, wk_ref, wv_ref, wl_ref, bfc_ref) = refs[:11]
    (pa_ref, pb_ref, pc_ref, kt_ref, vt_ref, lft_ref, qe_ref, ke_ref, vtb_ref, carry_ref) = refs[11 + n_alias:]
    tm = h_ref.shape[0]
    u = _rms(h_ref[...], g_ref[...]).astype(BF16)
    pa_ref[...] = _dot_nt(u, wa_ref[...])
    pb_ref[...] = _dot_nt(u, wb_ref[...])
    pc_ref[...] = _dot_nt(u, wc_ref[...])
    kt_ref[0, 0] = _dot_nt(wk_ref[...], u)
    vt = _dot_nt(wv_ref[...], u)
    vt_ref[0, 0] = vt
    vtb_ref[0] = vt.astype(BF16)
    lft = _log_sigmoid(_dot_nt(wl_ref[0:SUBLANES, :], u) + bfc_ref[:, 0:1])
    lft_ref[0] = lft

    first = pl.program_id(0) % per == 0
    carry = jnp.where(first, 0.0, carry_ref[:, 0:1])
    upto = (_iota((tm, tm), 0) <= _iota((tm, tm), 1)).astype(BF16)
    lfm = jnp.where(_iota((SUBLANES, tm), 0) < N_HEADS, lft, 0.0)
    c = _dot_sel(lfm, upto) + carry
    carry_ref[...] = jnp.broadcast_to(c[:, tm - 1:tm], carry_ref.shape)
    ccol = jnp.concatenate([c, jnp.zeros((LANES - SUBLANES, tm), F32)], axis=0).T

    hi, mid, lo = _split3(ccol)
    pieces = (hi.astype(F32) + pltpu.roll(mid.astype(F32), N_HEADS, 1)
              + pltpu.roll(lo.astype(F32), 2 * N_HEADS, 1)).astype(BF16)
    r = _iota((LANES, N_HEADS * EXT), 0)
    col = _iota((LANES, N_HEADS * EXT), 1)
    j = _imod(col, EXT) - C_KEY
    place_k = ((j >= 0) & (j < 3) & (r == N_HEADS * j + _idiv(col, EXT))).astype(BF16)
    lane = _imod(_iota((1, N_HEADS * EXT), 1), EXT)
    ones_k = ((lane >= C_QRY) & (lane < C_QRY + 3)).astype(F32)
    ke = _dot_nt(u, wke_ref[...]) + _dot(pieces, place_k) + ones_k
    for h in range(N_HEADS):
        ke_ref[0, h] = ke[:, h * EXT:(h + 1) * EXT].astype(BF16)

    hi, mid, lo = _split3(c)
    pieces_t = jnp.concatenate([hi.astype(F32), mid.astype(F32), lo.astype(F32),
                                jnp.zeros((LANES - 3 * SUBLANES, tm), F32)], axis=0).astype(BF16)
    r = _iota((VD, LANES), 0)
    col = _iota((VD, LANES), 1)
    j = _imod(r, HEAD_DIM) - (C_QRY - HEAD_DIM)
    place_q = ((j >= 0) & (j < 3) & (col == SUBLANES * j + _idiv(r, HEAD_DIM))).astype(BF16)
    rowi = _imod(_iota((VD, 1), 0), HEAD_DIM)
    neg_q = jnp.where(rowi < 3, -1.0, 0.0)
    extra = (_dot(place_q, pieces_t) + neg_q).astype(BF16)
    qt = (_dot_nt(wq_ref[...], u) * (HEAD_DIM ** -0.5)).astype(BF16)
    for h in range(N_HEADS):
        rows = slice(h * HEAD_DIM, (h + 1) * HEAD_DIM)
        qe_ref[0, h, 0:HEAD_DIM, :] = qt[rows, :]
        qe_ref[0, h, HEAD_DIM:, :] = extra[rows, :]


def _inproj_sample_body(h_ref, g_ref, wa_ref, wb_ref, wc_ref, wq_ref, wk_ref, wv_ref, wl_ref,
                        bfc_ref, bfr_ref,
                        pa_ref, pbt_ref, pc_ref, q_ref, k_ref, v_ref, lf_ref, kt_ref, vt_ref, lft_ref, qt_ref):
    u = _rms(h_ref[...], g_ref[...]).astype(BF16)
    pa_ref[...] = _dot_nt(u, wa_ref[...])
    pbt_ref[...] = _dot_nt(wb_ref[...], u)
    pc_ref[...] = _dot_nt(u, wc_ref[...])
    q_ref[...] = _dot_nt(u, wq_ref[...])
    qt_ref[...] = _dot_nt(wq_ref[...], u)
    k_ref[...] = _dot_nt(u, wk_ref[...])
    v_ref[...] = _dot_nt(u, wv_ref[...])
    lf_ref[...] = _log_sigmoid(_dot_nt(u, wl_ref[...]) + bfr_ref[...])
    kt_ref[...] = _dot_nt(wk_ref[...], u)
    vt_ref[...] = _dot_nt(wv_ref[...], u)
    lft_ref[...] = _log_sigmoid(_dot_nt(wl_ref[0:SUBLANES, :], u) + bfc_ref[:, 0:1])


def _inproj_weight_specs(d):
    return [_full_spec((n, d)) for n in (N_PA, N_PB, N_PC, VD, VD, VD, LANES)]


def _inproj_prompt_call(layer, depth, h, g, ws, bfc, bp, seq, tm, kv_prev):
    m, d = h.shape
    per = seq // tm
    tspec = lambda n: pl.BlockSpec((1, n, tm), lambda i: (i // per, 0, i % per))
    lspec = pl.BlockSpec((1, 1, VD, tm), lambda i: (layer, i // per, 0, i % per))
    hx = N_HEADS * EXT
    alias_args = tuple(kv_prev)
    n_in = 11
    return pl.pallas_call(
        functools.partial(_inproj_prompt_body, per=per, n_alias=len(alias_args)),
        grid=(m // tm,),
        in_specs=[_row_spec(tm, d), _full_spec((1, d))]
                 + [_full_spec((n, d)) for n in (N_PA, N_PB, N_PC, VD, hx, VD, VD, LANES)]
                 + [_full_spec((SUBLANES, LANES))]
                 + [pl.BlockSpec(memory_space=pl.ANY)] * len(alias_args),
        input_output_aliases={n_in + a: 3 + a for a in range(len(alias_args))},
        out_specs=[_row_spec(tm, N_PA), _row_spec(tm, N_PB), _row_spec(tm, N_PC),
                   lspec, lspec, tspec(SUBLANES),
                   pl.BlockSpec((1, N_HEADS, EXT, tm), lambda i: (i // per, 0, 0, i % per)),
                   pl.BlockSpec((1, N_HEADS, tm, EXT), lambda i: (i // per, 0, i % per, 0)),
                   tspec(VD)],
        out_shape=[jax.ShapeDtypeStruct((m, N_PA), F32), jax.ShapeDtypeStruct((m, N_PB), F32),
                   jax.ShapeDtypeStruct((m, N_PC), F32),
                   jax.ShapeDtypeStruct((depth, bp, VD, seq), F32),
                   jax.ShapeDtypeStruct((depth, bp, VD, seq), F32),
                   jax.ShapeDtypeStruct((bp, SUBLANES, seq), F32),
                   jax.ShapeDtypeStruct((bp, N_HEADS, EXT, seq), BF16),
                   jax.ShapeDtypeStruct((bp, N_HEADS, seq, EXT), BF16),
                   jax.ShapeDtypeStruct((bp, VD, seq), BF16)],
        scratch_shapes=[pltpu.VMEM((SUBLANES, LANES), F32)],
        compiler_params=_cparams("arbitrary"),
        name="inproj_prompt",
    )(h, g, *ws, bfc, *alias_args)


def _inproj_sample_call(h, g, ws, bfc, bfr):
    n, d = h.shape
    shapes = [(n, N_PA), (N_PB, n), (n, N_PC), (n, VD), (n, VD), (n, VD), (n, LANES),
              (VD, n), (VD, n), (SUBLANES, n), (VD, n)]
    return pl.pallas_call(
        _inproj_sample_body,
        grid=(1,),
        in_specs=[_full_spec((n, d)), _full_spec((1, d))] + _inproj_weight_specs(d)
                 + [_full_spec((SUBLANES, LANES)), _full_spec((1, LANES))],
        out_specs=[pl.BlockSpec(s, lambda i: (0, 0)) for s in shapes],
        out_shape=[jax.ShapeDtypeStruct(s, F32) for s in shapes],
        compiler_params=_cparams("arbitrary"),
        name="inproj_sample",
    )(h, g, *ws, bfc, bfr)


A_PAD = 32
C_PAD = 8
CONV_ROWS = 256


def _conv_prompt_body(pa_ref, pc_ref, aw_ref, ab_ref, lg_ref, lb_ref, cw_ref,
                      ya_ref, yc_ref, bufa_ref, bufc_ref, apad_ref, zpad_ref):
    seq = pa_ref.shape[1]
    c = GROUP_W
    apad_ref[0:A_PAD, :] = jnp.zeros((A_PAD, c), F32)
    apad_ref[A_PAD:, :] = pa_ref[0, :, 0:c] * _sigmoid(pa_ref[0, :, c:])
    zpad_ref[0:C_PAD, :] = jnp.zeros((C_PAD, c), F32)
    zpad_ref[C_PAD:, :] = pc_ref[0, :, c:2 * c] * pc_ref[0, :, 2 * c:]

    def step(i, carry):
        r0 = pl.multiple_of(i * CONV_ROWS, CONV_ROWS)
        win = apad_ref[pl.ds(r0, CONV_ROWS + A_PAD), :]
        acc = jnp.zeros((CONV_ROWS, c), F32) + ab_ref[...]
        for r in range(SUBLANES):
            nrow = CONV_ROWS if r == 0 else CONV_ROWS + SUBLANES
            u = None
            for a8 in range(0, A_PAD + 1, SUBLANES):
                j = a8 + r - (A_PAD - (CONV_A_W - 1))
                if 0 <= j < CONV_A_W:
                    term = aw_ref[j:j + 1, :] * win[a8:a8 + nrow, :]
                    u = term if u is None else u + term
            acc = acc + u[r:r + CONV_ROWS, :]
        mu = jnp.mean(acc, axis=-1, keepdims=True)
        var = jnp.mean(jnp.square(acc - mu), axis=-1, keepdims=True)
        yn = (acc - mu) * lax.rsqrt(var + EPS) * lg_ref[...] + lb_ref[...]
        ya_ref[0, pl.ds(r0, CONV_ROWS), :] = _silu(yn)
        zwin = zpad_ref[pl.ds(r0, CONV_ROWS + C_PAD), :]
        accc = jnp.zeros((CONV_ROWS, c), F32)
        for j in range(CONV_C_W):
            off = C_PAD - (CONV_C_W - 1) + j
            accc = accc + cw_ref[j:j + 1, :] * zwin[off:off + CONV_ROWS, :]
        yc_ref[0, pl.ds(r0, CONV_ROWS), :] = pc_ref[0, pl.ds(r0, CONV_ROWS), 0:c] * accc
        return carry

    lax.fori_loop(0, seq // CONV_ROWS, step, 0)
    na = CONV_A_W - 1
    nc = CONV_C_W - 1
    bufa_ref[0] = apad_ref[seq:A_PAD + seq, :][A_PAD - na:, :]
    bufc_ref[0] = zpad_ref[seq:C_PAD + seq, :][C_PAD - nc:, :]


def _conv_prompt_call(pa, pc, aw, ab, lg, lb, cw):
    b, seq, _ = pa.shape
    c = GROUP_W
    bspec = lambda n: pl.BlockSpec((1, seq, n), lambda i: (i, 0, 0))
    return pl.pallas_call(
        _conv_prompt_body,
        grid=(b,),
        in_specs=[bspec(N_PA), bspec(N_PC), _full_spec((CONV_A_W, c)), _full_spec((1, c)),
                  _full_spec((1, c)), _full_spec((1, c)), _full_spec((CONV_C_W, c))],
        out_specs=[bspec(c), bspec(c),
                   pl.BlockSpec((1, CONV_A_W - 1, c), lambda i: (i, 0, 0)),
                   pl.BlockSpec((1, CONV_C_W - 1, c), lambda i: (i, 0, 0))],
        out_shape=[jax.ShapeDtypeStruct((b, seq, c), F32), jax.ShapeDtypeStruct((b, seq, c), F32),
                   jax.ShapeDtypeStruct((b, CONV_A_W - 1, c), F32),
                   jax.ShapeDtypeStruct((b, CONV_C_W - 1, c), F32)],
        scratch_shapes=[pltpu.VMEM((A_PAD + seq, c), F32), pltpu.VMEM((C_PAD + seq, c), F32)],
        compiler_params=_cparams("arbitrary"),
        name="conv_prompt",
    )(pa, pc, aw, ab, lg, lb, cw)


def _gla_gate(alr, wa2_ref, ba_ref):
    return _log_sigmoid(_dot(alr.astype(BF16), wa2_ref[...]) + ba_ref[...]) * (1.0 / GLA_TAU)


def _gla_out(o, g, ng, hm):
    ms = _dot_sel(o * o, hm) * (1.0 / HEAD_DIM)
    return o * lax.rsqrt(ms + EPS) * ng * _silu(g)


GLA_SLAB = 256


def _group_row(x, g, r):
    n, c = x.shape
    x3 = x.reshape(n // g, g, c)
    return jnp.broadcast_to(x3[:, r:r + 1, :], (n // g, g, c)).reshape(n, c)


def _gla_prompt_body(pb_ref, wa2_ref, ba_ref, ng_ref, y_ref, st_ref, la_ref, s_ref):
    seq = pb_ref.shape[1]
    ck = GLA_CHUNK
    la_ref[...] = _gla_gate(pb_ref[0, :, 2 * KD + 2 * VD:], wa2_ref, ba_ref)
    s_ref[...] = jnp.zeros((KD, VD), F32)

    rs = min(GLA_SLAB, seq)
    nch = rs // ck
    tt = _iota((rs, rs), 0)
    ss = _iota((rs, rs), 1)
    tri = ((ss <= tt) & (_idiv(ss, ck) == _idiv(tt, ck))).astype(BF16)
    blk = _idiv(_iota((KD, VD), 0), DK_B) == _idiv(_iota((KD, VD), 1), HEAD_DIM)
    blk_bf = blk.astype(BF16)
    blk_f = blk.astype(F32)
    hm = _head_mean_matrix()
    scale = DK_B ** -0.5
    levels = [g for g in (2 * SUBLANES, 4 * SUBLANES, 8 * SUBLANES) if g <= ck]
    assert ck == 8 * SUBLANES
    key_head = _idiv(_iota((1, KD), 1), DK_B)
    val_head = _idiv(_iota((1, VD), 1), HEAD_DIM)
    tloc = _iota((rs, KD), 0)

    def slab(si, carry):
        r0 = pl.multiple_of(si * rs, rs)
        rows = pl.ds(r0, rs)
        q = pb_ref[0, rows, 0:KD] * scale
        k = pb_ref[0, rows, KD:2 * KD]
        v = pb_ref[0, rows, 2 * KD:2 * KD + VD]
        g = pb_ref[0, rows, 2 * KD + VD:2 * KD + 2 * VD]
        b = _sel_dot(tri, la_ref[rows, :])
        vb3 = v.astype(BF16).reshape(nch, ck, VD)

        t8 = _imod(tloc, SUBLANES)
        acc = jnp.zeros((rs, VD), F32)
        for s in range(SUBLANES):
            diff = jnp.where(t8 >= s, b - _group_row(b, SUBLANES, s), -jnp.inf)
            e = (q * _group_row(k, SUBLANES, s) * jnp.exp(diff)).astype(BF16)
            acc = acc + _dot(e, blk_bf) * _group_row(v, SUBLANES, s)

        sc = jnp.zeros((nch, N_HEADS * ck, ck), F32)
        for gsz in levels:
            r = _group_row(b, gsz, gsz // 2)
            upper = _imod(tloc, gsz) >= gsz // 2
            qf = q * jnp.exp(jnp.where(upper, b - r, -jnp.inf))
            kf = (k * jnp.exp(jnp.where(upper, -jnp.inf, r - b))).astype(BF16).reshape(nch, ck, KD)
            qstack = jnp.concatenate(
                [jnp.where(key_head == h, qf, 0.0).astype(BF16).reshape(nch, ck, KD)
                 for h in range(N_HEADS)], axis=1)
            sc_g = jnp.einsum('cqk,csk->cqs', qstack, kf, preferred_element_type=F32)
            same_group = _idiv(_imod(_iota((N_HEADS * ck, ck), 0), ck), gsz) == _idiv(_iota((N_HEADS * ck, ck), 1), gsz)
            sc = sc + jnp.where(same_group[None], sc_g, 0.0)
        res = jnp.einsum('cqs,csv->cqv', sc.astype(BF16), vb3, preferred_element_type=F32)
        for h in range(N_HEADS):
            part = res[:, h * ck:(h + 1) * ck, :].reshape(rs, VD)
            acc = acc + jnp.where(val_head == h, part, 0.0)

        b_last = _group_row(b, ck, ck - 1)
        kdec = k * jnp.exp(b_last - b)
        lastb = jnp.concatenate([b[(c + 1) * ck - 1:(c + 1) * ck, :] for c in range(nch)]
                                + [jnp.zeros((LANES - nch, KD), F32)], axis=0)
        dcols = jnp.exp(lastb.T)
        s_cur = s_ref[...]
        s_before = []
        for c in range(nch):
            s_before.append(s_cur.astype(BF16))
            crow = slice(c * ck, (c + 1) * ck)
            upd = _dot(kdec[crow, :].T.astype(BF16), vb3[c])
            s_cur = dcols[:, c:c + 1] * s_cur + upd * blk_f
        s_ref[...] = s_cur
        qb3 = (q * jnp.exp(b)).astype(BF16).reshape(nch, ck, KD)
        inter = jnp.einsum('cqk,ckv->cqv', qb3, jnp.stack(s_before), preferred_element_type=F32)
        o = inter.reshape(rs, VD) + acc
        y_ref[0, rows, :] = _gla_out(o, g, ng_ref[...], hm)
        return carry

    lax.fori_loop(0, seq // rs, slab, 0)
    for h in range(N_HEADS):
        st_ref[0, h] = s_ref[h * DK_B:(h + 1) * DK_B, h * HEAD_DIM:(h + 1) * HEAD_DIM]


def _gla_prompt_call(pb, wa2, ba, ng):
    b, seq, _ = pb.shape
    return pl.pallas_call(
        _gla_prompt_body,
        grid=(b,),
        in_specs=[pl.BlockSpec((1, seq, N_PB), lambda i: (i, 0, 0)), _full_spec((LANES, KD)),
                  _full_spec((1, KD)), _full_spec((1, VD))],
        out_specs=[pl.BlockSpec((1, seq, VD), lambda i: (i, 0, 0)),
                   pl.BlockSpec((1, N_HEADS, DK_B, HEAD_DIM), lambda i: (i, 0, 0, 0))],
        out_shape=[jax.ShapeDtypeStruct((b, seq, VD), F32),
                   jax.ShapeDtypeStruct((b, N_HEADS, DK_B, HEAD_DIM), F32)],
        scratch_shapes=[pltpu.VMEM((seq, KD), F32), pltpu.VMEM((KD, VD), F32)],
        compiler_params=_cparams("arbitrary"),
        name="gla_prompt",
    )(pb, wa2, ba, ng)


FOX_GROUP = 4


def _fox_prompt_body(qe_ref, ke_ref, vtb_ref, o_ref, *, tq):
    qi = pl.program_id(1)
    key_pos = _iota((tq, tq), 0)
    qry_pos = _iota((tq, tq), 1)

    def tile(ki):
        return pl.ds(pl.multiple_of(ki * tq, tq), tq)

    def consume(tiles, stats):
        scores = [[_dot(ke_ref[0, h, tile(ki), :], qe_ref[0, h]) for h in range(N_HEADS)]
                  for ki, _ in tiles]
        for (ki, masked), sc in zip(tiles, scores):
            soft = []
            for h in range(N_HEADS):
                m, l, _ = stats[h]
                s = sc[h]
                if masked:
                    s = jnp.where(key_pos <= qry_pos, s, NEG_INF)
                m_new = jnp.maximum(m, jnp.max(s, axis=0, keepdims=True))
                alpha = jnp.exp(m - m_new)
                p = jnp.exp(s - m_new)
                soft.append((m_new, alpha * l + jnp.sum(p, axis=0, keepdims=True), alpha, p.astype(BF16)))
            out = []
            for h in range(N_HEADS):
                m_new, l, alpha, p = soft[h]
                vh = vtb_ref[0, h * HEAD_DIM:(h + 1) * HEAD_DIM, tile(ki)]
                out.append((m_new, l, alpha * stats[h][2] + _dot(vh, p)))
            stats = tuple(out)
        return stats

    init = tuple((jnp.full((1, tq), NEG_INF, F32), jnp.zeros((1, tq), F32),
                  jnp.zeros((HEAD_DIM, tq), F32)) for _ in range(N_HEADS))
    g = FOX_GROUP
    stats = lax.fori_loop(
        0, qi // g, lambda k, st: consume([(g * k + t, False) for t in range(g)], st), init)
    left = qi % g
    tails = [functools.partial(consume, [(qi - r + t, False) for t in range(r)] + [(qi, True)])
             for r in range(g)]
    stats = lax.switch(left, tails, stats)
    o_ref[0] = jnp.concatenate([acc / l for _, l, acc in stats], axis=0).T


def _fox_prompt_call(qe, ke, vtb, tq):
    b, _, _, seq = qe.shape
    return pl.pallas_call(
        functools.partial(_fox_prompt_body, tq=tq),
        grid=(b, seq // tq),
        in_specs=[pl.BlockSpec((1, N_HEADS, EXT, tq), lambda i, j: (i, 0, 0, j)),
                  pl.BlockSpec((1, N_HEADS, seq, EXT), lambda i, j: (i, 0, 0, 0)),
                  pl.BlockSpec((1, VD, seq), lambda i, j: (i, 0, 0))],
        out_specs=pl.BlockSpec((1, tq, VD), lambda i, j: (i, j, 0)),
        out_shape=jax.ShapeDtypeStruct((b, seq, VD), F32),
        compiler_params=_cparams("arbitrary", "arbitrary"),
        name="fox_prompt",
    )(qe, ke, vtb)


def _mix_sample_body(pa_ref, pbt_ref, pc_ref, bufa_ref, sg_ref, bufc_ref,
                     aw_ref, ab_ref, lg_ref, lb_ref, wa2t_ref, bac_ref, ngc_ref, cw_ref,
                     ya_ref, yb_ref, yc_ref, bufa_o, sg_o, bufc_o, q_s, k_s, dec_s):
    c = GROUP_W
    a = pa_ref[:, 0:c] * _sigmoid(pa_ref[:, c:])
    na = CONV_A_W - 1
    acc = aw_ref[na:na + 1, :] * a + ab_ref[...]
    for j in range(na):
        acc = acc + aw_ref[j:j + 1, :] * bufa_ref[0, j]
    mu = jnp.mean(acc, axis=-1, keepdims=True)
    var = jnp.mean(jnp.square(acc - mu), axis=-1, keepdims=True)
    ya_ref[...] = _silu((acc - mu) * lax.rsqrt(var + EPS) * lg_ref[...] + lb_ref[...])
    for j in range(na - 1):
        bufa_o[j] = bufa_ref[0, j + 1]
    bufa_o[na - 1] = a
    z = pc_ref[:, c:2 * c] * pc_ref[:, 2 * c:]
    conv = (cw_ref[0:1, :] * bufc_ref[0, :, 0:c] + cw_ref[1:2, :] * bufc_ref[0, :, c:]
            + cw_ref[2:3, :] * z)
    yc_ref[...] = pc_ref[:, 0:c] * conv
    bufc_o[:, 0:c] = bufc_ref[0, :, c:]
    bufc_o[:, c:] = z
    q_s[...] = pbt_ref[0:KD, :] * (DK_B ** -0.5)
    k_s[...] = pbt_ref[KD:2 * KD, :]
    gate = _dot(wa2t_ref[...], pbt_ref[2 * KD + 2 * VD:, :].astype(BF16)) + bac_ref[...]
    dec_s[...] = jnp.exp(_log_sigmoid(gate) * (1.0 / GLA_TAU))
    n = pa_ref.shape[0]
    ys = []
    for h in range(N_HEADS):
        vrows = slice(2 * KD + h * HEAD_DIM, 2 * KD + (h + 1) * HEAD_DIM)
        grows = slice(2 * KD + VD + h * HEAD_DIM, 2 * KD + VD + (h + 1) * HEAD_DIM)
        vh = pbt_ref[vrows, :]

        def key_step(kk, o, h=h, vh=vh):
            hk = h * DK_B + kk
            one = pl.ds(hk, 1)
            s_new = dec_s[one, :] * sg_ref[0, hk] + k_s[one, :] * vh
            sg_o[hk] = s_new
            return o + q_s[one, :] * s_new

        o = lax.fori_loop(0, DK_B, key_step, jnp.zeros((HEAD_DIM, n), F32))
        ms = jnp.mean(o * o, axis=0, keepdims=True)
        ng = ngc_ref[h * HEAD_DIM:(h + 1) * HEAD_DIM, :]
        ys.append(o * lax.rsqrt(ms + EPS) * ng * _silu(pbt_ref[grows, :]))
    yb_ref[...] = jnp.concatenate(ys, axis=0).T


def _layer_spec(shape, l):
    return pl.BlockSpec((1,) + tuple(shape[1:]), lambda i: (l,) + (0,) * (len(shape) - 1),
                        pipeline_mode=pl.Buffered(1))


def _mix_sample_call(l, pa, pbt, pc, bufa, sg, bufc, aw, ab, lg, lb, wa2t, bac, ngc, cw):
    n = pa.shape[0]
    c = GROUP_W
    small = (aw, ab, lg, lb, wa2t, bac, ngc, cw)
    out_shape = [jax.ShapeDtypeStruct((n, c), F32)] * 3 + [
        jax.ShapeDtypeStruct(bufa.shape[1:], F32), jax.ShapeDtypeStruct(sg.shape[1:], F32),
        jax.ShapeDtypeStruct(bufc.shape[1:], F32)]
    return pl.pallas_call(
        _mix_sample_body,
        grid=(1,),
        in_specs=[_full_spec(pa.shape), _full_spec(pbt.shape), _full_spec(pc.shape),
                  _layer_spec(bufa.shape, l), _layer_spec(sg.shape, l), _layer_spec(bufc.shape, l)]
                 + [_full_spec(x.shape) for x in small],
        out_specs=[pl.BlockSpec(s.shape, lambda i, nd=len(s.shape): (0,) * nd) for s in out_shape],
        out_shape=out_shape,
        scratch_shapes=[pltpu.VMEM((KD, n), F32)] * 3,
        compiler_params=_cparams("arbitrary"),
        name="mix_sample",
    )(pa, pbt, pc, bufa, sg, bufc, *small)


def _logf_pages_body(x_ref, o_ref):
    n = x_ref.shape[0]
    x = x_ref[...]
    later = (_iota((PAGE_SIZE, PAGE_SIZE), 0) > _iota((PAGE_SIZE, PAGE_SIZE), 1)).astype(BF16)
    ones = jnp.ones((PAGE_SIZE, PAGE_SIZE), BF16)
    within = _dot_sel(x, later)
    total = pltpu.roll(_dot_sel(x, ones), N_HEADS, 0)
    o_ref[...] = jnp.where(_imod(_iota((n, PAGE_SIZE), 0), SUBLANES) < N_HEADS, within, total)


def _logf_pages_call(x, tm):
    m = x.shape[0]
    return pl.pallas_call(
        _logf_pages_body,
        grid=(m // tm,),
        in_specs=[_row_spec(tm, PAGE_SIZE)],
        out_specs=_row_spec(tm, PAGE_SIZE),
        out_shape=jax.ShapeDtypeStruct((m, PAGE_SIZE), F32),
        compiler_params=_cparams("arbitrary"),
        name="logf_pages",
    )(x)


def _decode_sample(b, slot, q_ref, qt_ref, kn_ref, vn_ref, lfn_ref, kbuf, vbuf, lfbuf, n_pages):
    past = n_pages * PAGE_SIZE
    scale = HEAD_DIM ** -0.5
    heads = [slice(h * HEAD_DIM, (h + 1) * HEAD_DIM) for h in range(N_HEADS)]
    hm = (_idiv(_iota((SUBLANES, VD), 1), HEAD_DIM) == _iota((SUBLANES, VD), 0)).astype(F32)
    qbd = jnp.broadcast_to(q_ref[pl.ds(b, 1), :], (SUBLANES, VD)) * hm
    onehot = (_iota((1, qt_ref.shape[1]), 1) == b).astype(F32)
    qcol = jnp.sum(qt_ref[...] * onehot, axis=1, keepdims=True) * scale
    qcb = jnp.broadcast_to(qcol, (VD, PAGE_SIZE))
    parts = [[] for _ in heads]
    for p in range(n_pages):
        prod = kbuf[slot, p] * qcb
        for h, rows in enumerate(heads):
            parts[h].append(prod[rows, :].reshape(HEAD_DIM // SUBLANES, SUBLANES, PAGE_SIZE).sum(axis=0))
    rowid = _iota((SUBLANES, past), 0)
    s = jnp.zeros((SUBLANES, past), F32)
    for h in range(N_HEADS):
        t = jnp.concatenate(parts[h], axis=1)
        for sh in (4, 2, 1):
            t = t + pltpu.roll(t, sh, 0)
        s = jnp.where(rowid == h, t, s)
    lf = lfbuf[slot]
    incl = pltpu.roll(lf, N_HEADS, 0)
    sh = PAGE_SIZE
    while sh < past:
        incl = incl + jnp.concatenate([incl[:, sh:], jnp.zeros((SUBLANES, sh), F32)], axis=1)
        sh *= 2
    later_pages = jnp.concatenate([incl[:, PAGE_SIZE:], jnp.zeros((SUBLANES, PAGE_SIZE), F32)], axis=1)
    valid = _iota((SUBLANES, past), 0) < N_HEADS
    logits = jnp.where(valid, s + lf + later_pages, 0.0)
    s_self = jnp.sum(qbd * kn_ref[pl.ds(b, 1), :], axis=1, keepdims=True) * scale
    pick = (_iota((SUBLANES, LANES), 1) == _iota((SUBLANES, LANES), 0)).astype(F32)
    c_new = jnp.sum(pick * lfn_ref[pl.ds(b, 1), :], axis=1, keepdims=True)
    self_logit = jnp.where(_iota((SUBLANES, 1), 0) < N_HEADS, s_self - c_new, 0.0)
    m = jnp.maximum(jnp.max(logits, axis=1, keepdims=True), self_logit)
    p = jnp.exp(logits - m)
    p_self = jnp.exp(self_logit - m)
    l = jnp.sum(p, axis=1, keepdims=True) + p_self
    acc = [jnp.zeros((HEAD_DIM, PAGE_SIZE), F32) for _ in heads]
    for pg in range(n_pages):
        vt = vbuf[slot, pg]
        cols = slice(pg * PAGE_SIZE, (pg + 1) * PAGE_SIZE)
        for h, rows in enumerate(heads):
            acc[h] = acc[h] + vt[rows, :] * jnp.broadcast_to(p[h:h + 1, cols], (HEAD_DIM, PAGE_SIZE))
    ocol = jnp.concatenate([jnp.sum(a, axis=1, keepdims=True) for a in acc], axis=0)
    eye = (_iota((VD, VD), 0) == _iota((VD, VD), 1)).astype(F32)
    orow = jnp.sum(eye * ocol, axis=0, keepdims=True)
    p_self_row = jnp.sum(p_self * hm, axis=0, keepdims=True)
    l_row = jnp.sum(l * hm, axis=0, keepdims=True)
    return (orow + p_self_row * vn_ref[pl.ds(b, 1), :]) / l_row


def _decode_gather(pt_ref, kc_hbm, vc_hbm, lfc_hbm, kbuf, vbuf, lfbuf, sem, *, layer, n_pages, per_step, base):
    i = pl.program_id(0)
    n_steps = pl.num_programs(0)

    def logf_copy(page, slot, p):
        cols = pl.ds(p * PAGE_SIZE, PAGE_SIZE)
        return pltpu.make_async_copy(lfc_hbm.at[layer, page], lfbuf.at[slot, :, cols], sem.at[slot, 2])

    def copies(sample, slot):
        out = []
        for p in range(n_pages):
            page = pt_ref[sample, p]
            out.append(pltpu.make_async_copy(kc_hbm.at[layer, page], kbuf.at[slot, p], sem.at[slot, 0]))
            out.append(pltpu.make_async_copy(vc_hbm.at[layer, page], vbuf.at[slot, p], sem.at[slot, 1]))
            out.append(logf_copy(page, slot, p))
        return out

    cur = (i % 2) * per_step
    nxt = per_step - cur
    first = base + i * per_step

    def start_all(sample, slot):
        for n, cp in enumerate(copies(sample, slot)):
            cp.start(priority=1 if n % 3 == 1 else 0)

    @pl.when(i == 0)
    def _():
        for j in range(per_step):
            start_all(base + j, j)

    for j in range(per_step):
        slot = cur + j
        pltpu.make_async_copy(kc_hbm.at[layer, pl.ds(0, n_pages)], kbuf.at[slot], sem.at[slot, 0]).wait()
        pltpu.make_async_copy(vc_hbm.at[layer, pl.ds(0, n_pages)], vbuf.at[slot], sem.at[slot, 1]).wait()
        for p in range(n_pages):
            logf_copy(pt_ref[first + j, p], slot, p).wait()

    @pl.when(i + 1 < n_steps)
    def _():
        for j in range(per_step):
            start_all(first + per_step + j, nxt + j)

    return [(first + j, cur + j) for j in range(per_step)]


def _ffn_decode_body(*refs, fc, n_pages, layer, per_step, base, mix):
    pt_ref, refs = refs[0], refs[1:]
    if mix:
        h_ref, ya_ref, yb_ref, yc_ref, ydp_ref, wo_ref, mpost_ref = refs[:7]
        refs = refs[7:]
    else:
        x_ref, refs = refs[0], refs[1:]
    (pre_ref, post_ref, wg_ref, wu_ref, wd_ref, q_ref, qt_ref, kn_ref, vn_ref, lfn_ref, kc_hbm, vc_hbm, lfc_hbm,
     o_ref, yd_ref, acc_ref, kbuf, vbuf, lfbuf, sem) = refs
    pairs = _decode_gather(pt_ref, kc_hbm, vc_hbm, lfc_hbm, kbuf, vbuf, lfbuf, sem,
                           layer=layer, n_pages=n_pages, per_step=per_step, base=base)
    if mix:
        ycat = jnp.concatenate([ya_ref[...], yb_ref[...], yc_ref[...], ydp_ref[...]], axis=1)
        x = h_ref[...] + _rms(_dot(ycat.astype(BF16), wo_ref[0]), mpost_ref[...])
    else:
        x = x_ref[...]
    for j, (sample, slot) in enumerate(pairs):
        yd_ref[j] = _decode_sample(sample, slot, q_ref, qt_ref, kn_ref, vn_ref, lfn_ref,
                                   kbuf, vbuf, lfbuf, n_pages)
    o_ref[...] = _swiglu_residual(x, pre_ref[...], post_ref[...],
                                  wg_ref.at[0], wu_ref.at[0], wd_ref.at[0], acc_ref, fc)


def _ffn_decode_call(layer, x, mix_in, pre_g, post_g, wg, wu, wd, page_table, q, qt, kn, vn, lfn, kc, vc, lfc,
                     base, n_dec, tm, fc=256):
    m, d = x.shape
    f = wg.shape[2]
    n, n_pages = page_table.shape
    steps = m // tm
    assert n_dec % steps == 0 and base + n_dec <= n
    per_step = n_dec // steps
    past = n_pages * PAGE_SIZE
    const = lambda shape: pl.BlockSpec(shape, lambda i, pt: (0,) * len(shape), pipeline_mode=pl.Buffered(1))
    lyr = lambda shape: pl.BlockSpec((1,) + shape, lambda i, pt: (layer, 0, 0), pipeline_mode=pl.Buffered(1))
    rows = lambda w: pl.BlockSpec((tm, w), lambda i, pt: (i, 0))
    anyspace = pl.BlockSpec(memory_space=pl.ANY)
    mix = mix_in is not None
    lead_specs, lead_args = [rows(d)], [x]
    if mix:
        ya, yb, yc, yd, wo, mpost = mix_in
        lead_specs += [rows(GROUP_W)] * 4 + [lyr((d, d)), const((1, d))]
        lead_args += [ya, yb, yc, yd, wo, mpost]
    grid_spec = pltpu.PrefetchScalarGridSpec(
        num_scalar_prefetch=1,
        grid=(steps,),
        in_specs=lead_specs + [const((1, d)), const((1, d)), lyr((d, f)), lyr((d, f)), lyr((f, d)),
                               const((n, VD)), const((VD, n)), const((n, VD)), const((n, VD)), const((n, LANES)),
                               anyspace, anyspace, anyspace],
        out_specs=[rows(d), pl.BlockSpec((per_step, 1, VD), lambda i, pt: (i, 0, 0))],
        scratch_shapes=[pltpu.VMEM((tm, d), F32),
                        pltpu.VMEM((2 * per_step, n_pages, VD, PAGE_SIZE), F32),
                        pltpu.VMEM((2 * per_step, n_pages, VD, PAGE_SIZE), F32),
                        pltpu.VMEM((2 * per_step, SUBLANES, past), F32),
                        pltpu.SemaphoreType.DMA((2 * per_step, 3))],
    )
    return pl.pallas_call(
        functools.partial(_ffn_decode_body, fc=fc, n_pages=n_pages, layer=layer, per_step=per_step,
                          base=base, mix=mix),
        grid_spec=grid_spec,
        out_shape=[jax.ShapeDtypeStruct((m, d), F32), jax.ShapeDtypeStruct((n_dec, 1, VD), F32)],
        compiler_params=_cparams("arbitrary"),
        name="mix_ffn_decode" if mix else "ffn_decode",
    )(page_table, *lead_args, pre_g, post_g, wg, wu, wd, q, qt, kn, vn, lfn, kc, vc, lfc)


ATTN_TILE = 256


def _row_tile(m, cap=512):
    tm = cap
    while tm >= SUBLANES:
        if m % tm == 0:
            return tm
        tm //= 2
    raise ValueError(f"row count {m} is not a multiple of {SUBLANES}")


def kernel(x_prompt, x_sample, state_conv_a, state_gla, state_conv_c, cache_k, cache_v, cache_logf, page_table, ffn1_pre_g, ffn1_post_g, ffn1_w_gate, ffn1_w_up, ffn1_w_down, mix_pre_g, mix_post_g, w_in, w_out, a_conv_w, a_conv_b, a_ln_g, a_ln_b, b_gate_w2, b_gate_b, b_out_norm_g, c_conv_w, d_forget_b, ffn2_pre_g, ffn2_post_g, ffn2_w_gate, ffn2_w_up, ffn2_w_down):
    depth = w_in.shape[0]
    bp, seq, d = x_prompt.shape
    bd = x_sample.shape[0]
    n_pool = cache_k.shape[1]
    assert x_sample.shape[1] == 1 and d == D_MODEL
    assert seq % ATTN_TILE == 0 and seq % GLA_CHUNK == 0 and seq >= CONV_A_W - 1

    hp = x_prompt.reshape(bp * seq, d)
    hs = x_sample.reshape(bd, d)
    tmp = _row_tile(bp * seq)
    tms = _row_tile(bd)
    tq = ATTN_TILE

    off_b = N_PA
    off_c = off_b + 2 * KD + 2 * VD + GLA_RANK
    off_d = off_c + N_PC
    row = lambda v: v.reshape(1, -1)

    lanes_of = lambda v, n=LANES: jnp.broadcast_to(v[:, None], (v.shape[0], n))

    ck = cache_k.transpose(0, 1, 3, 4, 2).reshape(depth, n_pool, VD, PAGE_SIZE)
    cv = cache_v.transpose(0, 1, 3, 4, 2).reshape(depth, n_pool, VD, PAGE_SIZE)
    sca = state_conv_a.transpose(0, 2, 1, 3)
    sgl = state_gla.transpose(0, 2, 3, 4, 1).reshape(depth, KD, HEAD_DIM, bd)
    scc = state_conv_c.reshape(depth, bd, (CONV_C_W - 1) * GROUP_W)
    w_in_t = w_in.transpose(0, 2, 1)

    lf_t = jnp.pad(cache_logf.transpose(0, 1, 3, 2), ((0, 0), (0, 0), (0, SUBLANES - N_HEADS), (0, 0)))
    lf_rows = depth * n_pool * SUBLANES
    lf_pages = _logf_pages_call(lf_t.reshape(lf_rows, PAGE_SIZE), _row_tile(lf_rows, cap=4096))
    lf_pages = lf_pages.reshape(depth, n_pool, SUBLANES, PAGE_SIZE)

    ffn1_ws = tuple(w.astype(BF16) for w in (ffn1_w_gate, ffn1_w_up, ffn1_w_down))
    ffn2_ws = tuple(w.astype(BF16) for w in (ffn2_w_gate, ffn2_w_up, ffn2_w_down))
    wo = w_out.astype(BF16)

    outs = [[] for _ in range(12)]
    kv_prev = (jnp.zeros((depth, bp, VD, seq), F32), jnp.zeros((depth, bp, VD, seq), F32))
    for l in range(depth):
        wt = w_in_t[l].astype(BF16)
        pad_rows = lambda w, n: jnp.pad(w, ((0, n - w.shape[0]), (0, 0)))
        wq, wk, wv = (wt[off_d + i * VD:off_d + (i + 1) * VD] for i in range(3))
        in_w = (wt[0:off_b], pad_rows(wt[off_b:off_c], N_PB), wt[off_c:off_d],
                wq, wk, wv, pad_rows(wt[off_d + 3 * VD:], LANES))
        slabs = lambda w: jnp.pad(w.reshape(N_HEADS, HEAD_DIM, d),
                                  ((0, 0), (0, EXT - HEAD_DIM), (0, 0))).reshape(N_HEADS * EXT, d)
        in_w_prompt = in_w[0:3] + (wq, slabs(wk), wk, wv, in_w[6])
        bf_pad = jnp.pad(d_forget_b[l], (0, LANES - N_HEADS))
        bfr = row(bf_pad)
        bfc = lanes_of(bf_pad[0:SUBLANES])
        wa2 = jnp.pad(b_gate_w2[l], ((0, LANES - GLA_RANK), (0, 0))).astype(BF16)
        mix_w = (a_conv_w[l], row(a_conv_b[l]), row(a_ln_g[l]), row(a_ln_b[l]))
        ffn1_w = (row(ffn1_pre_g[l]), row(ffn1_post_g[l])) + ffn1_ws
        ffn2_w = (row(ffn2_pre_g[l]), row(ffn2_post_g[l])) + ffn2_ws

        hs = _ffn_call(l, hs, *ffn1_w, tm=tms)
        s_pa, s_pbt, s_pc, s_q, s_k, s_v, s_lf, s_kt, s_vt, s_lft, s_qt = _inproj_sample_call(
            hs, row(mix_pre_g[l]), in_w, bfc, bfr)
        s_ya, s_yb, s_yc, s_bufa, s_state, s_bufc = _mix_sample_call(
            l, s_pa, s_pbt, s_pc, sca, sgl, scc, *mix_w, wa2.T, lanes_of(b_gate_b[l], bd),
            lanes_of(b_out_norm_g[l], bd), c_conv_w[l])

        dec_args = (page_table, s_q, s_qt, s_k, s_v, s_lf, ck, cv, lf_pages)
        half = bd // 2
        hp, s_yd0 = _ffn_decode_call(l, hp, None, *ffn1_w, *dec_args, base=0, n_dec=half, tm=tmp)
        pa, pb, pc, kt_all, vt_all, lft, qe, ke, vtb = _inproj_prompt_call(
            l, depth, hp, row(mix_pre_g[l]), in_w_prompt, bfc, bp, seq, tm=tmp, kv_prev=kv_prev)
        kv_prev = (kt_all, vt_all)
        r3 = lambda t: t.reshape(bp, seq, t.shape[-1])
        ya, yc, buf_a, buf_c = _conv_prompt_call(r3(pa), r3(pc), *mix_w, c_conv_w[l])
        yb, s_b = _gla_prompt_call(r3(pb), wa2, row(b_gate_b[l]), row(b_out_norm_g[l]))
        yd = _fox_prompt_call(qe, ke, vtb, tq)
        f2 = lambda t: t.reshape(bp * seq, t.shape[-1])
        hp, s_yd1 = _ffn_decode_call(l, hp, (f2(ya), f2(yb), f2(yc), f2(yd), wo, row(mix_post_g[l])),
                                     *ffn2_w, *dec_args, base=half, n_dec=bd - half, tm=tmp)
        for i, t in ((0, buf_a), (1, s_b), (2, buf_c), (5, lft[:, 0:N_HEADS, :])):
            outs[i].append(t)

        s_yd = jnp.concatenate([s_yd0, s_yd1], axis=0).reshape(bd, VD)
        hs = _mix_ffn_call(l, hs, s_ya, s_yb, s_yc, s_yd, wo, row(mix_post_g[l]), *ffn2_w, tm=tms)
        for i, t in enumerate((s_bufa, s_state, s_bufc, s_kt, s_vt, s_lft[0:N_HEADS, :])):
            outs[6 + i].append(t)

    p_ca, p_gla, p_cc, _, _, p_lft, s_ca, s_gla, s_cc, s_kt, s_vt, s_lft = (
        jnp.stack(o) if o else None for o in outs)
    p_kt, p_vt = kv_prev
    heads = lambda t: t.reshape(t.shape[:-2] + (N_HEADS, HEAD_DIM, t.shape[-1]))
    return (hp.reshape(bp, seq, d), hs.reshape(bd, 1, d),
            p_ca, p_gla, p_cc,
            heads(p_kt).transpose(0, 1, 4, 2, 3), heads(p_vt).transpose(0, 1, 4, 2, 3),
            p_lft.transpose(0, 1, 3, 2),
            s_ca.transpose(0, 2, 1, 3),
            s_gla.reshape(depth, N_HEADS, DK_B, HEAD_DIM, bd).transpose(0, 4, 1, 2, 3),
            s_cc.reshape(depth, bd, CONV_C_W - 1, GROUP_W),
            heads(s_kt).transpose(0, 3, 1, 2)[:, :, None], heads(s_vt).transpose(0, 3, 1, 2)[:, :, None],
            s_lft.transpose(0, 2, 1)[:, :, None])
```
